```python
import math
import jax, jax.numpy as jnp
from jax import lax
import numpy as np

D_MODEL = 4096
BATCH = 4
SEQ = 2048
DEPTH = 2
DEC_BATCH = 8
DEC_SEQ = 4
PAST_LEN = 16384
PAGE_SIZE = 128

HEAD_DIM = 128
MIX_WIDTH = D_MODEL
FOX_WIDTH = D_MODEL // 2
FOX_HEADS = FOX_WIDTH // HEAD_DIM
SGU_WIDTH = D_MODEL // 2
SGU_GROUPS = SGU_WIDTH // HEAD_DIM
SGU_CHUNK = 128
CONV_WIDTH = D_MODEL // 2
CONV_K = 3
GLA_HEADS = 4
GLA_KEY = D_MODEL // 4
GLA_VAL = D_MODEL // 2
GLA_DK = GLA_KEY // GLA_HEADS
GLA_DV = GLA_VAL // GLA_HEADS
GLA_RANK = 16
GLA_TAU = 16.0
GLA_CHUNK = 16
Q_BLOCK = 128
D_FF = 4 * D_MODEL
N_EVEN = (DEPTH + 1) // 2
N_ODD = DEPTH // 2
IN0 = 3 * FOX_WIDTH + FOX_HEADS + 2 * SGU_WIDTH
IN1 = 3 * CONV_WIDTH + 2 * GLA_KEY + GLA_VAL + GLA_RANK + GLA_VAL
FORGET_BIAS_INIT = 3.0
CACHE_FORGET_LOGIT = 9.0
EPS = 1e-6
NEG_INF = -1e30

kernel_name = 'hybrid_fox_sgu_conv_gla_step'


def _rms(x, w=None):
    xf = x.astype(jnp.float32)
    y = xf * lax.rsqrt(jnp.mean(xf * xf, axis=-1, keepdims=True) + EPS)
    if w is not None:
        y = y * w.astype(jnp.float32)
    return y.astype(x.dtype)


def _split(x, sizes):
    idx = [int(i) for i in np.cumsum(sizes)[:-1]]
    return jnp.split(x, idx, axis=-1)


def _ada(c, w, b):
    mod = (c @ w + b)[:, None, :]
    return jnp.split(mod, 6, axis=-1)


def _modulate(x, shift, scale):
    return _rms(x) * (1 + scale) + shift


def _fox_prompt(q, k, v, logf):
    b, s, h, d = q.shape
    nb = s // Q_BLOCK
    scale = d ** -0.5
    F = jnp.cumsum(logf.astype(jnp.float32), axis=1).transpose(0, 2, 1)
    qb = q.reshape(b, nb, Q_BLOCK, h, d).swapaxes(0, 1)
    Fb = F.reshape(b, h, nb, Q_BLOCK).transpose(2, 0, 1, 3)
    key_pos = jnp.arange(s)

    def block(args):
        qi, Fi, i = args
        logits = jnp.einsum('bthd,bshd->bhts', qi, k, preferred_element_type=jnp.float32) * scale
        logits = logits + (Fi[..., :, None] - F[..., None, :])
        q_pos = i * Q_BLOCK + jnp.arange(Q_BLOCK)
        logits = jnp.where(key_pos[None, :] <= q_pos[:, None], logits, NEG_INF)
        p = jax.nn.softmax(logits, axis=-1)
        return jnp.einsum('bhts,bshd->bthd', p.astype(v.dtype), v)

    o = lax.map(block, (qb, Fb, jnp.arange(nb)))
    return o.swapaxes(0, 1).reshape(b, s, h * d)


def _fox_sample(q, k, v, logf, k_past, v_past, logf_past):
    b, t, h, d = q.shape
    P = k_past.shape[1]
    scale = d ** -0.5
    Fn = jnp.cumsum(logf.astype(jnp.float32), axis=1).transpose(0, 2, 1)
    lp = logf_past.astype(jnp.float32)
    R = (lax.cumsum(lp, axis=1, reverse=True) - lp).transpose(0, 2, 1)
    s_past = jnp.einsum('bthd,bshd->bhts', q, k_past, preferred_element_type=jnp.float32) * scale
    s_past = s_past + Fn[..., :, None] + R[..., None, :]
    s_new = jnp.einsum('bthd,bshd->bhts', q, k, preferred_element_type=jnp.float32) * scale
    s_new = s_new + Fn[..., :, None] - Fn[..., None, :]
    causal = jnp.tril(jnp.ones((t, t), dtype=bool))
    s_new = jnp.where(causal, s_new, NEG_INF)
    p = jax.nn.softmax(jnp.concatenate([s_past, s_new], axis=-1), axis=-1)
    o = (jnp.einsum('bhts,bshd->bthd', p[..., :P].astype(v.dtype), v_past)
         + jnp.einsum('bhts,bshd->bthd', p[..., P:].astype(v.dtype), v))
    return o.reshape(b, t, h * d)


def _sgu(u, v, sgu_norm, w_s, b_s):
    b, L, _ = u.shape
    rows = min(L, SGU_CHUNK)
    u = jax.nn.gelu(u)
    v = _rms(jax.nn.gelu(v), sgu_norm)
    mask = jnp.tril(jnp.ones((SGU_CHUNK, SGU_CHUNK), dtype=w_s.dtype))
    w = (w_s * mask)[:, :rows, :rows]
    vc = v.reshape(b, L // rows, rows, SGU_GROUPS, SGU_WIDTH // SGU_GROUPS)
    z = jnp.einsum('gts,bnsgc->bntgc', w, vc) + b_s[:, :rows].T[None, None, :, :, None]
    return u * z.reshape(b, L, SGU_WIDTH), v


def _short_conv(xin, gate_c, gate_b, conv_w, conv_prev):
    z = gate_c * xin
    L = z.shape[1]
    zp = jnp.concatenate([conv_prev.astype(z.dtype), z], axis=1)
    y = sum(conv_w[i] * zp[:, i:i + L] for i in range(CONV_K))
    return gate_b * y, zp[:, L:]


def _gla(q, k, v, log_a, s0):
    b, L, h, dk = q.shape
    c = math.gcd(L, GLA_CHUNK)
    n = L // c

    def chunks(x):
        return x.astype(jnp.float32).reshape(b, n, c, *x.shape[2:]).swapaxes(0, 1)

    mask = jnp.tril(jnp.ones((c, c), dtype=bool))

    def step(S, inp):
        qc, kc, vc, ac = inp
        bc = jnp.cumsum(ac, axis=1)
        qt = qc * jnp.exp(bc)
        kt = kc * jnp.exp(-bc)
        att = jnp.where(mask, jnp.einsum('bthk,bshk->bhts', qt, kt), 0.0)
        o = jnp.einsum('bthk,bhkv->bthv', qt, S) + jnp.einsum('bhts,bshv->bthv', att, vc)
        bl = bc[:, -1]
        kd = kc * jnp.exp(bl[:, None] - bc)
        S = S * jnp.exp(bl)[..., None] + jnp.einsum('bshk,bshv->bhkv', kd, vc)
        return S, o

    q = q * dk ** -0.5
    S, o = lax.scan(step, s0.astype(jnp.float32), (chunks(q), chunks(k), chunks(v), chunks(log_a)))
    return o.swapaxes(0, 1).reshape(b, L, h, -1), S


def _trunk(x, c, past, w_in0, f_bias, q_norm, k_norm, sgu_norm, sgu_w, sgu_b, w_out0,
           w_in1, conv_w, gla_a_w2, gla_a_b, gla_norm, w_out1, ada_w, ada_b, mlp_w1, mlp_w2):
    nb, L = x.shape[0], x.shape[1]
    st = {'k': [], 'v': [], 'logf': [], 'sgu': [], 'conv': [], 'gla': []}
    for layer in range(DEPTH):
        j = layer // 2
        sh1, sc1, g1, sh2, sc2, g2 = _ada(c, ada_w[layer], ada_b[layer])
        h = _modulate(x, sh1, sc1)
        if layer % 2 == 0:
            q, k, v, fl, u, vg = _split(h @ w_in0[j], (FOX_WIDTH, FOX_WIDTH, FOX_WIDTH, FOX_HEADS, SGU_WIDTH, SGU_WIDTH))
            q = _rms(q.reshape(nb, L, FOX_HEADS, HEAD_DIM), q_norm[j])
            k = _rms(k.reshape(nb, L, FOX_HEADS, HEAD_DIM), k_norm[j])
            v = v.reshape(nb, L, FOX_HEADS, HEAD_DIM)
            logf = jax.nn.log_sigmoid((fl + f_bias[j]).astype(jnp.float32))
            if past is None:
                o_a = _fox_prompt(q, k, v, logf)
            else:
                cache_k, cache_v, cache_logf, page_table = past[0], past[1], past[2], past[3]
                k_past = cache_k[j][page_table].reshape(nb, -1, FOX_HEADS, HEAD_DIM)
                v_past = cache_v[j][page_table].reshape(nb, -1, FOX_HEADS, HEAD_DIM)
                lf_past = cache_logf[j][page_table].reshape(nb, -1, FOX_HEADS)
                o_a = _fox_sample(q, k, v, logf, k_past, v_past, lf_past)
            o_b, v_rows = _sgu(u, vg, sgu_norm[j], sgu_w[j], sgu_b[j])
            mix = jnp.concatenate([o_a.astype(x.dtype), o_b], axis=-1) @ w_out0[j]
            st['k'].append(k)
            st['v'].append(v)
            st['logf'].append(logf.astype(x.dtype))
            if past is not None:
                st['sgu'].append(v_rows)
        else:
            xin, gc, gb, gq, gk, gv, ga, gg = _split(
                h @ w_in1[j], (CONV_WIDTH, CONV_WIDTH, CONV_WIDTH, GLA_KEY, GLA_KEY, GLA_VAL, GLA_RANK, GLA_VAL))
            if past is None:
                conv_prev = jnp.zeros((nb, CONV_K - 1, CONV_WIDTH), x.dtype)
                s0 = jnp.zeros((nb, GLA_HEADS, GLA_DK, GLA_DV), jnp.float32)
            else:
                conv_prev = past[4][j]
                s0 = past[5][j]
            o_c, conv_new = _short_conv(xin, gc, gb, conv_w[j], conv_prev)
            log_a = jax.nn.log_sigmoid((ga @ gla_a_w2[j] + gla_a_b[j]).astype(jnp.float32)) / GLA_TAU
            o_d, s_new = _gla(gq.reshape(nb, L, GLA_HEADS, GLA_DK), gk.reshape(nb, L, GLA_HEADS, GLA_DK),
                              gv.reshape(nb, L, GLA_HEADS, GLA_DV), log_a.reshape(nb, L, GLA_HEADS, GLA_DK), s0)
            o_d = _rms(o_d, gla_norm[j].reshape(GLA_HEADS, GLA_DV)).reshape(nb, L, GLA_VAL).astype(x.dtype)
            o_d = o_d * jax.nn.silu(gg)
            mix = jnp.concatenate([o_c, o_d], axis=-1) @ w_out1[j]
            st['conv'].append(conv_new)
            st['gla'].append(s_new.astype(x.dtype))
        x = x + g1 * mix
        h = _modulate(x, sh2, sc2)
        x = x + g2 * (jnp.square(jax.nn.relu(h @ mlp_w1[layer])) @ mlp_w2[layer])
    out = {name: jnp.stack(lst) for name, lst in st.items() if len(lst) > 0}
    return x, out


def setup_inputs(seed: int = 0) -> dict:
    key = jax.random.key(seed)
    ks = iter(jax.random.split(key, 48))

    def nrm(shape, s):
        return jax.random.normal(next(ks), shape, jnp.float32) * s

    n_pages = PAST_LEN // PAGE_SIZE
    n_used = DEC_BATCH * n_pages
    n_pool = n_used + (n_used + 3) // 4
    page_table = jax.random.permutation(next(ks), n_pool)[:n_used].reshape(DEC_BATCH, n_pages).astype(jnp.int32)
    return {
        'x_prompt': nrm((BATCH, SEQ, D_MODEL), 1.0),
        'x_sample': nrm((DEC_BATCH, DEC_SEQ, D_MODEL), 1.0),
        'cache_k': nrm((N_EVEN, n_pool, PAGE_SIZE, FOX_HEADS, HEAD_DIM), 1.0),
        'cache_v': nrm((N_EVEN, n_pool, PAGE_SIZE, FOX_HEADS, HEAD_DIM), 1.0),
        'cache_logf': jax.nn.log_sigmoid(CACHE_FORGET_LOGIT + nrm((N_EVEN, n_pool, PAGE_SIZE, FOX_HEADS), 0.5)),
        'state_conv': nrm((N_ODD, DEC_BATCH, CONV_K - 1, CONV_WIDTH), 0.5),
        'state_gla': nrm((N_ODD, DEC_BATCH, GLA_HEADS, GLA_DK, GLA_DV), 1.0),
        'page_table': page_table,
        'c_prompt': nrm((BATCH, D_MODEL), 1.0),
        'c_sample': nrm((DEC_BATCH, D_MODEL), 1.0),
        'w_in0': nrm((N_EVEN, D_MODEL, IN0), D_MODEL ** -0.5),
        'f_bias': FORGET_BIAS_INIT + nrm((N_EVEN, FOX_HEADS), 0.1),
        'q_norm': 1.0 + nrm((N_EVEN, HEAD_DIM), 0.02),
        'k_norm': 1.0 + nrm((N_EVEN, HEAD_DIM), 0.02),
        'sgu_norm': 1.0 + nrm((N_EVEN, SGU_WIDTH), 0.02),
        'sgu_w': nrm((N_EVEN, SGU_GROUPS, SGU_CHUNK, SGU_CHUNK), SGU_CHUNK ** -0.5),
        'sgu_b': 1.0 + nrm((N_EVEN, SGU_GROUPS, SGU_CHUNK), 0.02),
        'w_out0': nrm((N_EVEN, MIX_WIDTH, D_MODEL), MIX_WIDTH ** -0.5),
        'w_in1': nrm((N_ODD, D_MODEL, IN1), D_MODEL ** -0.5),
        'conv_w': nrm((N_ODD, CONV_K, CONV_WIDTH), CONV_K ** -0.5),
        'gla_a_w2': nrm((N_ODD, GLA_RANK, GLA_KEY), GLA_RANK ** -0.5),
        'gla_a_b': nrm((N_ODD, GLA_KEY), 0.01),
        'gla_norm': 1.0 + nrm((N_ODD, GLA_VAL), 0.02),
        'w_out1': nrm((N_ODD, MIX_WIDTH, D_MODEL), MIX_WIDTH ** -0.5),
        'ada_w': nrm((DEPTH, D_MODEL, 6 * D_MODEL), 0.1 * D_MODEL ** -0.5),
        'ada_b': nrm((DEPTH, 6 * D_MODEL), 0.01),
        'mlp_w1': nrm((DEPTH, D_MODEL, D_FF), D_MODEL ** -0.5),
        'mlp_w2': nrm((DEPTH, D_FF, D_MODEL), D_FF ** -0.5),
    }


def reference(x_prompt, x_sample, cache_k, cache_v, cache_logf, state_conv, state_gla, page_table,
              c_prompt, c_sample, w_in0, f_bias, q_norm, k_norm, sgu_norm, sgu_w, sgu_b, w_out0,
              w_in1, conv_w, gla_a_w2, gla_a_b, gla_norm, w_out1, ada_w, ada_b, mlp_w1, mlp_w2):
    weights = (w_in0, f_bias, q_norm, k_norm, sgu_norm, sgu_w, sgu_b, w_out0,
               w_in1, conv_w, gla_a_w2, gla_a_b, gla_norm, w_out1, ada_w, ada_b, mlp_w1, mlp_w2)
    y_prompt, sp = _trunk(x_prompt, c_prompt, None, *weights)
    y_sample, ss = _trunk(x_sample, c_sample, (cache_k, cache_v, cache_logf, page_table, state_conv, state_gla), *weights)
    return (y_prompt, y_sample, sp['k'], sp['v'], sp['logf'], ss['k'], ss['v'], ss['logf'], ss['sgu'],
            sp['conv'], ss['conv'], sp['gla'], ss['gla'])
```

```python
import functools

import jax
import jax.numpy as jnp
from jax import lax
from jax.experimental import pallas as pl
from jax.experimental.pallas import tpu as pltpu

F32 = jnp.float32
BF16 = jnp.bfloat16
EPS = 1e-6
NEG_INF = -1e30
HEAD_DIM = 128
LANES = 128
SGU_CHUNK = 128
GLA_TAU = 16.0
GLA_CHUNK_ROWS = 64
VMEM_LIMIT_BYTES = 56 * 1024 * 1024


def _params(*semantics):
    return pltpu.CompilerParams(dimension_semantics=semantics,
                                vmem_limit_bytes=VMEM_LIMIT_BYTES)


def _split3(x):
    hi = x.astype(BF16)
    r = x - hi.astype(F32)
    mid = r.astype(BF16)
    lo = (r - mid.astype(F32)).astype(BF16)
    return hi, mid, lo


def _log_sigmoid(x):
    return jnp.minimum(x, 0.0) - jnp.log1p(jnp.exp(-jnp.abs(x)))


def _gelu_tanh(x):
    c = 0.7978845608028654
    return 0.5 * x * (1.0 + jnp.tanh(c * (x + 0.044715 * (x * x * x))))


def _ada_kernel(c_ref, w_ref, b_ref, o_ref):
    o_ref[...] = jnp.dot(c_ref[...], w_ref[...].astype(BF16),
                         preferred_element_type=F32) + b_ref[...]


def _ada(c_rows, ada_w, ada_b, tn=512):
    n_layers, d, n = ada_w.shape
    r = c_rows.shape[0]
    tn = min(tn, n)
    return pl.pallas_call(
        _ada_kernel,
        grid=(n_layers, n // tn),
        in_specs=[pl.BlockSpec((r, d), lambda l, j: (0, 0)),
                  pl.BlockSpec((None, d, tn), lambda l, j: (l, 0, j)),
                  pl.BlockSpec((None, 1, tn), lambda l, j: (l, 0, j))],
        out_specs=pl.BlockSpec((None, r, tn), lambda l, j: (l, 0, j)),
        out_shape=jax.ShapeDtypeStruct((n_layers, r, n), F32),
        compiler_params=_params("arbitrary", "arbitrary"),
        name="ada",
    )(c_rows, ada_w, ada_b.reshape(n_layers, 1, n))


def _modulate_kernel(x_ref, sh_ref, sc_ref, o_ref):
    x = x_ref[...]
    y = x * lax.rsqrt(jnp.mean(x * x, axis=-1, keepdims=True) + EPS)
    o_ref[...] = (y * (1.0 + sc_ref[...]) + sh_ref[...]).astype(o_ref.dtype)


def _modulate(x, sh, sc, rows_per_group, tr=256):
    rows, d = x.shape
    tr = min(tr, rows)
    r = sh.shape[1]
    grp = lambda i: ((i * tr) // rows_per_group, 0, 0)
    return pl.pallas_call(
        _modulate_kernel,
        grid=(rows // tr,),
        in_specs=[pl.BlockSpec((tr, d), lambda i: (i, 0)),
                  pl.BlockSpec((None, r, d), grp),
                  pl.BlockSpec((None, r, d), grp)],
        out_specs=pl.BlockSpec((tr, d), lambda i: (i, 0)),
        out_shape=jax.ShapeDtypeStruct((rows, d), BF16),
        compiler_params=_params("arbitrary"),
        name="modulate",
    )(x, sh, sc)


def _proj_kernel(*refs, k_sizes, n_w, ext_names, n_out, epilogue, cast_rows):
    n_lhs = len(k_sizes)
    pos = 0
    xp = refs[pos:pos + n_lhs]; pos += n_lhs
    xs = refs[pos:pos + n_lhs]; pos += n_lhs
    w = refs[pos:pos + n_w]; pos += n_w
    ext = dict(zip(ext_names, refs[pos:pos + len(ext_names)])); pos += len(ext_names)
    out_p = refs[pos:pos + n_out]; pos += n_out
    out_s = refs[pos:pos + n_out]; pos += n_out
    wbf = refs[pos:pos + n_w]

    def accumulate(lhs):
        accs = []
        for m in range(n_w):
            acc = None
            off = 0
            for a, ka in enumerate(k_sizes):
                part = jnp.dot(lhs[a][...], wbf[m][off:off + ka, :],
                               preferred_element_type=F32)
                acc = part if acc is None else acc + part
                off += ka
            accs.append(acc)
        return accs

    @pl.when(pl.program_id(1) == 0)
    def _():
        k_total = sum(k_sizes)
        for m in range(n_w):
            for r0 in range(0, k_total, cast_rows):
                wbf[m][r0:r0 + cast_rows, :] = w[m][r0:r0 + cast_rows, :].astype(BF16)
        for o_ref, val in zip(out_s, epilogue(accumulate(xs), ext, True)):
            o_ref[...] = val.astype(o_ref.dtype)

    for o_ref, val in zip(out_p, epilogue(accumulate(xp), ext, False)):
        o_ref[...] = val.astype(o_ref.dtype)


def _proj(xp, xs, weights, n_cols, epilogue, out_dtypes, ext=(), *, tm=1024, tn=512,
          rows_per_group=None, name="proj"):
    mp, ms = xp[0].shape[0], xs[0].shape[0]
    k_sizes = tuple(int(a.shape[1]) for a in xp)
    k_total = sum(k_sizes)
    tm = min(tm, mp, rows_per_group or mp)
    tn = min(tn, n_cols)
    assert mp % tm == 0 and n_cols % tn == 0
    for warr, c0 in weights:
        assert warr.shape[0] == k_total and c0 % tn == 0
    in_specs, args = [], []
    for a in xp:
        in_specs.append(pl.BlockSpec((tm, a.shape[1]), lambda j, i: (i, 0)))
        args.append(a)
    for a in xs:
        in_specs.append(pl.BlockSpec((ms, a.shape[1]), lambda j, i: (0, 0)))
        args.append(a)
    for warr, c0 in weights:
        in_specs.append(pl.BlockSpec((k_total, tn), lambda j, i, cb=c0 // tn: (0, j + cb)))
        args.append(warr)
    ext_names = []
    for ename, kind, arr in ext:
        ext_names.append(ename)
        if kind == "col":
            spec = pl.BlockSpec((arr.shape[0], tn), lambda j, i: (0, j))
        elif kind == "prow":
            spec = pl.BlockSpec((tm, tn), lambda j, i: (i, j))
        elif kind == "pgrp":
            spec = pl.BlockSpec((None, 1, tn),
                                lambda j, i: ((i * tm) // rows_per_group, 0, j))
        elif kind == "srow":
            spec = pl.BlockSpec((ms, tn), lambda j, i: (0, j))
        else:
            raise ValueError(kind)
        in_specs.append(spec)
        args.append(arr)
    n_out = len(out_dtypes)
    out_specs = ([pl.BlockSpec((tm, tn), lambda j, i: (i, j))] * n_out
                 + [pl.BlockSpec((ms, tn), lambda j, i: (0, j))] * n_out)
    out_shape = ([jax.ShapeDtypeStruct((mp, n_cols), dt) for dt in out_dtypes]
                 + [jax.ShapeDtypeStruct((ms, n_cols), dt) for dt in out_dtypes])
    cast_rows = min(512, k_total)
    assert k_total % cast_rows == 0
    outs = pl.pallas_call(
        functools.partial(_proj_kernel, k_sizes=k_sizes, n_w=len(weights),
                          ext_names=tuple(ext_names), n_out=n_out, epilogue=epilogue,
                          cast_rows=cast_rows),
        grid=(n_cols // tn, mp // tm),
        in_specs=in_specs,
        out_specs=out_specs,
        out_shape=out_shape,
        scratch_shapes=[pltpu.VMEM((k_total, tn), BF16) for _ in weights],
        compiler_params=_params("arbitrary", "arbitrary"),
        name=name,
    )(*args)
    return outs[:n_out], outs[n_out:]


def _epi_plain(accs, ext, is_sample):
    return (accs[0],)


def _epi_head_norm(accs, ext, is_sample):
    acc = accs[0]
    wn = ext["norm"][...]
    parts = []
    for c0 in range(0, acc.shape[1], HEAD_DIM):
        a = acc[:, c0:c0 + HEAD_DIM]
        y = a * lax.rsqrt(jnp.mean(a * a, axis=-1, keepdims=True) + EPS)
        parts.append(y * wn[:, c0:c0 + HEAD_DIM])
    return (jnp.concatenate(parts, axis=1),)


def _epi_log_forget(accs, ext, is_sample):
    return (_log_sigmoid(accs[0] + ext["bias"][...]),)


def _epi_gelu(accs, ext, is_sample):
    return (_gelu_tanh(accs[0]),)


def _epi_silu(accs, ext, is_sample):
    a = accs[0]
    return (a * jax.nn.sigmoid(a),)


def _epi_relu2(accs, ext, is_sample):
    r = jnp.maximum(accs[0], 0.0)
    return (r * r,)


def _epi_product(accs, ext, is_sample):
    return (accs[1] * accs[0],)


def _epi_log_decay(accs, ext, is_sample):
    return (_log_sigmoid(accs[0] + ext["bias"][...]) / GLA_TAU,)


def _epi_residual(accs, ext, is_sample):
    if is_sample:
        return (ext["res_s"][...] + ext["gate_s"][...] * accs[0],)
    return (ext["res_p"][...] + ext["gate_p"][...] * accs[0],)


def _down_kernel(xp_ref, xs_ref, w_ref, resp_ref, gp_ref, ress_ref, gs_ref,
                 op_ref, os_ref, accs_ref, *, nk):
    i = pl.program_id(1)
    k = pl.program_id(2)
    wb = w_ref[...].astype(BF16)
    part = jnp.dot(xp_ref[...], wb, preferred_element_type=F32)

    @pl.when(k == 0)
    def _():
        op_ref[...] = part

    @pl.when(k > 0)
    def _():
        op_ref[...] += part

    @pl.when(k == nk - 1)
    def _():
        op_ref[...] = resp_ref[...] + gp_ref[...] * op_ref[...]

    @pl.when(i == 0)
    def _():
        ps = jnp.dot(xs_ref[...], wb, preferred_element_type=F32)

        @pl.when(k == 0)
        def _():
            accs_ref[...] = ps

        @pl.when(k > 0)
        def _():
            accs_ref[...] += ps

        @pl.when(k == nk - 1)
        def _():
            os_ref[...] = ress_ref[...] + gs_ref[...] * accs_ref[...]


def _down(xp, xs, w, res_p, gate_p, res_s, gate_s, rows_per_group, *, tm=2048, tn=1024, tk=512):
    mp, kdim = xp.shape
    ms = xs.shape[0]
    n = w.shape[1]
    tm, tn, tk = min(tm, rows_per_group, mp), min(tn, n), min(tk, kdim)
    nk = kdim // tk
    return pl.pallas_call(
        functools.partial(_down_kernel, nk=nk),
        grid=(n // tn, mp // tm, nk),
        in_specs=[pl.BlockSpec((tm, tk), lambda j, i, k: (i, k)),
                  pl.BlockSpec((ms, tk), lambda j, i, k: (0, k)),
                  pl.BlockSpec((tk, tn), lambda j, i, k: (k, j)),
                  pl.BlockSpec((tm, tn), lambda j, i, k: (i, j)),
                  pl.BlockSpec((None, 1, tn),
                               lambda j, i, k: ((i * tm) // rows_per_group, 0, j)),
                  pl.BlockSpec((ms, tn), lambda j, i, k: (0, j)),
                  pl.BlockSpec((ms, tn), lambda j, i, k: (0, j))],
        out_specs=[pl.BlockSpec((tm, tn), lambda j, i, k: (i, j)),
                   pl.BlockSpec((ms, tn), lambda j, i, k: (0, j))],
        out_shape=[jax.ShapeDtypeStruct((mp, n), F32),
                   jax.ShapeDtypeStruct((ms, n), F32)],
        scratch_shapes=[pltpu.VMEM((ms, tn), F32)],
        compiler_params=_params("arbitrary", "arbitrary", "arbitrary"),
        name="mlp_down",
    )(xp, xs, w, res_p, gate_p, res_s, gate_s)


def _cumsum_kernel(x_ref, o_ref, carry_ref):
    @pl.when(pl.program_id(1) == 0)
    def _():
        carry_ref[...] = jnp.zeros_like(carry_ref)

    x = x_ref[...]
    t, w = x.shape
    r = lax.broadcasted_iota(jnp.int32, (t, t), 0)
    c = lax.broadcasted_iota(jnp.int32, (t, t), 1)
    tri = jnp.where(c <= r, 1.0, 0.0).astype(BF16)
    y = jnp.dot(tri, jnp.concatenate(_split3(x), axis=1), preferred_element_type=F32)
    out = y[:, :w] + y[:, w:2 * w] + y[:, 2 * w:] + carry_ref[...]
    o_ref[...] = out
    carry_ref[...] = out[t - 1:t, :]


def _cumsum_rows(x, n_groups, tc=256):
    rows, w = x.shape
    per = rows // n_groups
    tc = min(tc, per)
    nt = per // tc
    return pl.pallas_call(
        _cumsum_kernel,
        grid=(n_groups, nt),
        in_specs=[pl.BlockSpec((tc, w), lambda b, t: (b * nt + t, 0))],
        out_specs=pl.BlockSpec((tc, w), lambda b, t: (b * nt + t, 0)),
        out_shape=jax.ShapeDtypeStruct((rows, w), F32),
        scratch_shapes=[pltpu.VMEM((1, w), F32)],
        compiler_params=_params("arbitrary", "arbitrary"),
        name="cumsum_logf",
    )(x)


def _fox_prompt_kernel(q_ref, k_ref, v_ref, fcol_ref, frow_ref, o_ref, kbf, vbf, *, tq, scale):
    h = pl.program_id(1)
    qi = pl.program_id(2)

    @pl.when(qi == 0)
    def _():
        kbf[...] = k_ref[...].astype(BF16)
        vbf[...] = v_ref[...].astype(BF16)

    q = q_ref[...]
    lane = lax.broadcasted_iota(jnp.int32, fcol_ref.shape, 1)
    ft = jnp.sum(jnp.where(lane == h, fcol_ref[...], 0.0), axis=1, keepdims=True)

    def step(j, carry, masked):
        m, l, acc = carry
        start = pl.multiple_of(j * tq, tq)
        ks = kbf[pl.ds(start, tq), :]
        vs = vbf[pl.ds(start, tq), :]
        s = lax.dot_general(q, ks, (((1,), (1,)), ((), ())),
                            preferred_element_type=F32) * scale
        s = s + (ft - frow_ref[pl.ds(j, 1), :])
        if masked:
            r = lax.broadcasted_iota(jnp.int32, (tq, tq), 0)
            c = lax.broadcasted_iota(jnp.int32, (tq, tq), 1)
            s = jnp.where(c <= r, s, NEG_INF)
        m_new = jnp.maximum(m, jnp.max(s, axis=1, keepdims=True))
        alpha = jnp.exp(m - m_new)
        p = jnp.exp(s - m_new)
        l = alpha * l + jnp.sum(p, axis=1, keepdims=True)
        acc = alpha * acc + jnp.dot(p.astype(BF16), vs, preferred_element_type=F32)
        return m_new, l, acc

    init = (jnp.full((tq, 1), NEG_INF, F32), jnp.zeros((tq, 1), F32),
            jnp.zeros((tq, HEAD_DIM), F32))
    carry = lax.fori_loop(0, qi, functools.partial(step, masked=False), init)
    m, l, acc = step(qi, carry, True)
    o_ref[...] = (acc / l).astype(o_ref.dtype)


def _fox_prompt(q, k, v, f_col, f_row, batch, seq, heads, tq=256):
    tq = min(tq, seq)
    nq = seq // tq
    return pl.pallas_call(
        functools.partial(_fox_prompt_kernel, tq=tq, scale=HEAD_DIM ** -0.5),
        grid=(batch, heads, nq),
        in_specs=[pl.BlockSpec((tq, HEAD_DIM), lambda b, h, i: (b * nq + i, h)),
                  pl.BlockSpec((seq, HEAD_DIM), lambda b, h, i: (b, h)),
                  pl.BlockSpec((seq, HEAD_DIM), lambda b, h, i: (b, h)),
                  pl.BlockSpec((tq, LANES), lambda b, h, i: (b * nq + i, 0)),
                  pl.BlockSpec((None, nq, tq), lambda b, h, i: (b * heads + h, 0, 0))],
        out_specs=pl.BlockSpec((tq, HEAD_DIM), lambda b, h, i: (b * nq + i, h)),
        out_shape=jax.ShapeDtypeStruct(q.shape, BF16),
        scratch_shapes=[pltpu.VMEM((seq, HEAD_DIM), BF16), pltpu.VMEM((seq, HEAD_DIM), BF16)],
        compiler_params=_params("arbitrary", "arbitrary", "arbitrary"),
        name="fox_prompt",
    )(q, k, v, f_col, f_row)


def _fox_sample_kernel(pt_ref, q_ref, ck_ref, cv_ref, lf_ref, kn_ref, vn_ref, fn_ref, fs_ref,
                       o_ref, m_ref, l_ref, acc_ref, tail_ref, *, n_pages, n_new, scale):
    del pt_ref
    p = pl.program_id(1)

    @pl.when(p == 0)
    def _():
        m_ref[...] = jnp.full(m_ref.shape, NEG_INF, F32)
        l_ref[...] = jnp.zeros_like(l_ref)
        acc_ref[...] = jnp.zeros_like(acc_ref)
        tail_ref[...] = jnp.zeros_like(tail_ref)

    q = q_ref[...]

    def update(s, vb):
        m_old = m_ref[...]
        m_new = jnp.maximum(m_old, jnp.max(s, axis=1, keepdims=True))
        alpha = jnp.exp(m_old - m_new)
        pr = jnp.exp(s - m_new)
        l_ref[...] = alpha * l_ref[...] + jnp.sum(pr, axis=1, keepdims=True)
        acc_ref[...] = alpha * acc_ref[...] + jnp.dot(pr.astype(BF16), vb,
                                                      preferred_element_type=F32)
        m_ref[...] = m_new

    lf = lf_ref[...]
    heads, page = lf.shape
    jj = lax.broadcasted_iota(jnp.int32, (page, page), 0)
    ss = lax.broadcasted_iota(jnp.int32, (page, page), 1)
    later = jnp.where(jj > ss, 1.0, 0.0).astype(BF16)
    y = jnp.dot(jnp.concatenate(_split3(lf), axis=0), later, preferred_element_type=F32)
    r_in = y[:heads] + y[heads:2 * heads] + y[2 * heads:] + tail_ref[...]
    tail_ref[...] = tail_ref[...] + jnp.sum(lf, axis=1, keepdims=True)
    r_rows = jnp.concatenate([r_in] * n_new, axis=0)

    kb = ck_ref[...].astype(BF16)
    s = lax.dot_general(q, kb, (((1,), (1,)), ((), ())), preferred_element_type=F32) * scale
    update(s + fn_ref[...] + r_rows, cv_ref[...].astype(BF16))

    @pl.when(p == n_pages - 1)
    def _():
        s2 = lax.dot_general(q, kn_ref[...], (((1,), (1,)), ((), ())),
                             preferred_element_type=F32) * scale
        row = lax.broadcasted_iota(jnp.int32, s2.shape, 0)
        key = lax.broadcasted_iota(jnp.int32, s2.shape, 1)
        s2 = jnp.where(key * heads <= row, s2 + (fn_ref[...] - fs_ref[...]), NEG_INF)
        update(s2, vn_ref[...])
        o_ref[...] = acc_ref[...] / l_ref[...]


def _fox_sample(page_table, q_blk, cache_k, cache_v, cache_lf_t, k_new, v_new, fn_rows, fs_rows,
                n_new):
    nb, n_pages = page_table.shape
    rows, width = q_blk.shape[1], q_blk.shape[2]
    page = cache_k.shape[1]
    heads = cache_lf_t.shape[1]
    page_idx = lambda b, p, pt: (pt[b, n_pages - 1 - p], 0, 0)
    per_b = lambda b, p, pt: (b, 0, 0)
    grid_spec = pltpu.PrefetchScalarGridSpec(
        num_scalar_prefetch=1,
        grid=(nb, n_pages),
        in_specs=[pl.BlockSpec((None, rows, width), per_b),
                  pl.BlockSpec((None, page, width), page_idx),
                  pl.BlockSpec((None, page, width), page_idx),
                  pl.BlockSpec((None, heads, page), page_idx),
                  pl.BlockSpec((None, page, width), per_b),
                  pl.BlockSpec((None, page, width), per_b),
                  pl.BlockSpec((None, rows, page), per_b),
                  pl.BlockSpec((None, rows, page), per_b)],
        out_specs=pl.BlockSpec((None, rows, width), per_b),
        scratch_shapes=[pltpu.VMEM((rows, 1), F32), pltpu.VMEM((rows, 1), F32),
                        pltpu.VMEM((rows, width), F32), pltpu.VMEM((heads, 1), F32)],
    )
    return pl.pallas_call(
        functools.partial(_fox_sample_kernel, n_pages=n_pages, n_new=n_new,
                          scale=HEAD_DIM ** -0.5),
        grid_spec=grid_spec,
        out_shape=jax.ShapeDtypeStruct((nb, rows, width), F32),
        compiler_params=_params("arbitrary", "arbitrary"),
        name="fox_sample",
    )(page_table, q_blk, cache_k, cache_v, cache_lf_t, k_new, v_new, fn_rows, fs_rows)


def _sgu_kernel(u_ref, vg_ref, nw_ref, w_ref, bt_ref, *out_refs, groups):
    o_ref = out_refs[0]
    g = vg_ref[...].astype(F32)
    v = g * lax.rsqrt(jnp.mean(g * g, axis=-1, keepdims=True) + EPS) * nw_ref[...]
    if len(out_refs) > 1:
        out_refs[1][...] = v
    vb = v.astype(BF16)
    rows = v.shape[0]
    cw = v.shape[1] // groups
    r = lax.broadcasted_iota(jnp.int32, (rows, rows), 0)
    c = lax.broadcasted_iota(jnp.int32, (rows, rows), 1)
    bt = bt_ref[...]
    for gi in range(groups):
        wm = jnp.where(c <= r, w_ref[gi], 0.0).astype(BF16)
        z = jnp.dot(wm, vb[:, gi * cw:(gi + 1) * cw], preferred_element_type=F32)
        z = z + bt[:, gi:gi + 1]
        u = u_ref[:, gi * cw:(gi + 1) * cw].astype(F32)
        o_ref[:, gi * cw:(gi + 1) * cw] = (u * z).astype(o_ref.dtype)


def _sgu(uv, norm_w, w_pos, bias_t, rows, emit_v):
    m, two_w = uv.shape
    width = two_w // 2
    groups = w_pos.shape[0]
    out_shape = [jax.ShapeDtypeStruct((m, width), BF16)]
    out_specs = [pl.BlockSpec((rows, width), lambda i: (i, 0))]
    if emit_v:
        out_shape.append(jax.ShapeDtypeStruct((m, width), F32))
        out_specs.append(pl.BlockSpec((rows, width), lambda i: (i, 0)))
    return pl.pallas_call(
        functools.partial(_sgu_kernel, groups=groups),
        grid=(m // rows,),
        in_specs=[pl.BlockSpec((rows, width), lambda i: (i, 0)),
                  pl.BlockSpec((rows, width), lambda i: (i, 1)),
                  pl.BlockSpec((1, width), lambda i: (0, 0)),
                  pl.BlockSpec((groups, rows, rows), lambda i: (0, 0, 0)),
                  pl.BlockSpec((rows, groups), lambda i: (0, 0))],
        out_specs=out_specs,
        out_shape=out_shape,
        compiler_params=_params("arbitrary"),
        name="sgu",
    )(uv, uv, norm_w, w_pos, bias_t)


def _conv_prompt_kernel(z_ref, gb_ref, w_ref, o_ref, zbuf, *, tr):
    @pl.when(pl.program_id(2) == 0)
    def _():
        zbuf[0:8, :] = jnp.zeros((8, zbuf.shape[1]), F32)

    zbuf[8:8 + tr, :] = z_ref[...]
    w = w_ref[...]
    y = (w[0:1] * zbuf[6:6 + tr, :] + w[1:2] * zbuf[7:7 + tr, :] + w[2:3] * zbuf[8:8 + tr, :])
    o_ref[...] = (gb_ref[...].astype(F32) * y).astype(o_ref.dtype)
    zbuf[0:8, :] = zbuf[tr:tr + 8, :]


def _conv_prompt(z, gb, conv_w, batch, seq, tr=256, tc=512):
    m, width = z.shape
    tr, tc = min(tr, seq), min(tc, width)
    nt = seq // tr
    return pl.pallas_call(
        functools.partial(_conv_prompt_kernel, tr=tr),
        grid=(width // tc, batch, nt),
        in_specs=[pl.BlockSpec((tr, tc), lambda c, b, t: (b * nt + t, c)),
                  pl.BlockSpec((tr, tc), lambda c, b, t: (b * nt + t, c)),
                  pl.BlockSpec((conv_w.shape[0], tc), lambda c, b, t: (0, c))],
        out_specs=pl.BlockSpec((tr, tc), lambda c, b, t: (b * nt + t, c)),
        out_shape=jax.ShapeDtypeStruct((m, width), BF16),
        scratch_shapes=[pltpu.VMEM((tr + 8, tc), F32)],
        compiler_params=_params("arbitrary", "arbitrary", "arbitrary"),
        name="conv_prompt",
    )(z, gb, conv_w)


def _conv_sample_kernel(z0_ref, z1_ref, z2_ref, gb_ref, w_ref, o_ref):
    w = w_ref[...]
    y = w[0:1] * z0_ref[...] + w[1:2] * z1_ref[...] + w[2:3] * z2_ref[...]
    o_ref[...] = (gb_ref[...].astype(F32) * y).astype(o_ref.dtype)


def _conv_sample(z0, z1, z2, gb, conv_w):
    return pl.pallas_call(
        _conv_sample_kernel,
        out_shape=jax.ShapeDtypeStruct(z0.shape, BF16),
        name="conv_sample",
    )(z0, z1, z2, gb, conv_w)


def _gla_kernel(q_ref, k_ref, v_ref, la_ref, gate_ref, nw_ref, s0_ref, o_ref, st_ref, st, *, nc):
    ci = pl.program_id(2)

    @pl.when(ci == 0)
    def _():
        st[...] = s0_ref[...]

    la = la_ref[...]
    c, dk = la.shape
    r = lax.broadcasted_iota(jnp.int32, (c, c), 0)
    cc = lax.broadcasted_iota(jnp.int32, (c, c), 1)
    tril = cc <= r
    y = jnp.dot(jnp.where(tril, 1.0, 0.0).astype(BF16),
                jnp.concatenate(_split3(la), axis=1), preferred_element_type=F32)
    bc = y[:, :dk] + y[:, dk:2 * dk] + y[:, 2 * dk:]
    q = q_ref[...].astype(F32) * dk ** -0.5
    k = k_ref[...].astype(F32)
    v = v_ref[...]
    qt = (q * jnp.exp(bc)).astype(BF16)
    kt = (k * jnp.exp(-bc)).astype(BF16)
    att = lax.dot_general(qt, kt, (((1,), (1,)), ((), ())), preferred_element_type=F32)
    att = jnp.where(tril, att, 0.0)
    s_t = st[...]
    o = (lax.dot_general(qt, s_t.astype(BF16), (((1,), (1,)), ((), ())),
                         preferred_element_type=F32)
         + jnp.dot(att.astype(BF16), v, preferred_element_type=F32))
    bl = bc[c - 1:c, :]
    kd = (k * jnp.exp(bl - bc)).astype(BF16)
    st[...] = s_t * jnp.exp(bl) + lax.dot_general(v, kd, (((0,), (0,)), ((), ())),
                                                   preferred_element_type=F32)
    on = o * lax.rsqrt(jnp.mean(o * o, axis=-1, keepdims=True) + EPS) * nw_ref[...]
    o_ref[...] = (on * gate_ref[...].astype(F32)).astype(o_ref.dtype)

    @pl.when(ci == nc - 1)
    def _():
        st_ref[...] = st[...]


def _gla(qkv, log_a, gate, norm_w, s0_t, batch, seq, heads, chunk):
    m = qkv.shape[0]
    dk = log_a.shape[1] // heads
    dv = gate.shape[1] // heads
    assert (2 * heads * dk) % dv == 0
    v0 = (2 * heads * dk) // dv
    chunk = min(chunk, seq)
    nc = seq // chunk
    row = lambda b, h, c: b * nc + c
    return pl.pallas_call(
        functools.partial(_gla_kernel, nc=nc),
        grid=(batch, heads, nc),
        in_specs=[pl.BlockSpec((chunk, dk), lambda b, h, c: (row(b, h, c), h)),
                  pl.BlockSpec((chunk, dk), lambda b, h, c: (row(b, h, c), heads + h)),
                  pl.BlockSpec((chunk, dv), lambda b, h, c: (row(b, h, c), v0 + h)),
                  pl.BlockSpec((chunk, dk), lambda b, h, c: (row(b, h, c), h)),
                  pl.BlockSpec((chunk, dv), lambda b, h, c: (row(b, h, c), h)),
                  pl.BlockSpec((1, dv), lambda b, h, c: (0, h)),
                  pl.BlockSpec((None, None, dv, dk), lambda b, h, c: (b, h, 0, 0))],
        out_specs=[pl.BlockSpec((chunk, dv), lambda b, h, c: (row(b, h, c), h)),
                   pl.BlockSpec((None, None, dv, dk), lambda b, h, c: (b, h, 0, 0))],
        out_shape=[jax.ShapeDtypeStruct((m, heads * dv), BF16),
                   jax.ShapeDtypeStruct((batch, heads, dv, dk), F32)],
        scratch_shapes=[pltpu.VMEM((dv, dk), F32)],
        compiler_params=_params("arbitrary", "arbitrary", "arbitrary"),
        name="gla",
    )(qkv, qkv, qkv, log_a, gate, norm_w, s0_t)


def _pad_cols(w, n):
    return jnp.pad(w, ((0, 0), (0, n - w.shape[1])))


def kernel(x_prompt, x_sample, cache_k, cache_v, cache_logf, state_conv, state_gla, page_table,
           c_prompt, c_sample, w_in0, f_bias, q_norm, k_norm, sgu_norm, sgu_w, sgu_b, w_out0,
           w_in1, conv_w, gla_a_w2, gla_a_b, gla_norm, w_out1, ada_w, ada_b, mlp_w1, mlp_w2):
    nbp, seq, d = x_prompt.shape
    nbs, dseq, _ = x_sample.shape
    mp, ms = nbp * seq, nbs * dseq
    fox_w = d // 2
    heads = fox_w // HEAD_DIM
    sgu_width = d // 2
    groups = sgu_w.shape[1]
    conv_width = state_conv.shape[-1]
    gla_heads, gla_dk, gla_dv = state_gla.shape[2], state_gla.shape[3], state_gla.shape[4]
    gla_key, gla_val = gla_heads * gla_dk, gla_heads * gla_dv
    gla_rank = gla_a_w2.shape[1]
    n_pool, page = cache_k.shape[1], cache_k.shape[2]
    n_pages = page_table.shape[1]

    r_c = nbp + nbs
    r_pad = -(-r_c // 16) * 16
    c_rows = jnp.pad(jnp.concatenate([c_prompt, c_sample], axis=0),
                     ((0, r_pad - r_c), (0, 0))).astype(BF16)
    mod = _ada(c_rows, ada_w, ada_b)

    def mods(layer):
        parts = jnp.split(mod[layer], 6, axis=-1)
        pp = [p[:nbp].reshape(nbp, 1, d) for p in parts]
        ps = [jnp.repeat(p[nbp:r_c], dseq, axis=0) for p in parts]
        return pp, ps

    xp = x_prompt.reshape(mp, d)
    xs = x_sample.reshape(ms, d)

    def modulate_both(xp, xs, shp, scp, shs, scs):
        hp = _modulate(xp, shp, scp, seq)
        hs = _modulate(xs, shs.reshape(1, ms, d), scs.reshape(1, ms, d), ms, tr=ms)
        return hp, hs

    def mlp(xp, xs, layer, shp, scp, gp, shs, scs, gs):
        hp, hs = modulate_both(xp, xs, shp, scp, shs, scs)
        (ap,), (as_,) = _proj([hp], [hs], [(mlp_w1[layer], 0)], mlp_w1.shape[2], _epi_relu2,
                              [BF16], name="mlp_up")
        return _down(ap, as_, mlp_w2[layer], xp, gp, xs, gs, seq)

    def residual_proj(lhs_p, lhs_s, w, xp, xs, gp, gs, name):
        (yp,), (ys,) = _proj(lhs_p, lhs_s, [(w, 0)], d, _epi_residual, [F32],
                             ext=[("res_p", "prow", xp), ("gate_p", "pgrp", gp),
                                  ("res_s", "srow", xs), ("gate_s", "srow", gs)],
                             rows_per_group=seq, name=name)
        return yp, ys

    (sh1p, sc1p, g1p, sh2p, sc2p, g2p), (sh1s, sc1s, g1s, sh2s, sc2s, g2s) = mods(0)
    hp, hs = modulate_both(xp, xs, sh1p, sc1p, sh1s, sc1s)
    w0 = w_in0[0]
    qn = jnp.tile(q_norm[0], heads).reshape(1, fox_w)
    kn = jnp.tile(k_norm[0], heads).reshape(1, fox_w)
    (q_p,), (q_s,) = _proj([hp], [hs], [(w0, 0)], fox_w, _epi_head_norm, [BF16],
                           ext=[("norm", "col", qn)], name="in0_q")
    (k_p,), (k_s,) = _proj([hp], [hs], [(w0, fox_w)], fox_w, _epi_head_norm, [F32],
                           ext=[("norm", "col", kn)], name="in0_k")
    (v_p,), (v_s,) = _proj([hp], [hs], [(w0, 2 * fox_w)], fox_w, _epi_plain, [F32], name="in0_v")
    w_fl = _pad_cols(w0[:, 3 * fox_w:3 * fox_w + heads], LANES)
    fb = jnp.pad(f_bias[0], (0, LANES - heads)).reshape(1, LANES)
    (lf_p,), (lf_s,) = _proj([hp], [hs], [(w_fl, 0)], LANES, _epi_log_forget, [F32],
                             ext=[("bias", "col", fb)], name="in0_logf")
    w_uv = w0[:, 3 * fox_w + heads:]
    (uv_p,), (uv_s,) = _proj([hp], [hs], [(w_uv, 0)], 2 * sgu_width, _epi_gelu, [BF16],
                             name="in0_uv")

    f_col = _cumsum_rows(lf_p, nbp)
    tq = min(256, seq)
    f_row = (f_col[:, :heads].reshape(nbp, seq, heads).transpose(0, 2, 1)
             .reshape(nbp * heads, seq // tq, tq))
    oa_p = _fox_prompt(q_p, k_p, v_p, f_col, f_row, nbp, seq, heads, tq)

    lf_s16 = lf_s[:, :heads].reshape(nbs, dseq, heads)
    pad_f = 8
    lf_s_pad = jnp.pad(lf_s.reshape(nbs, dseq, LANES), ((0, 0), (0, pad_f - dseq), (0, 0)))
    fn = _cumsum_rows(lf_s_pad.reshape(nbs * pad_f, LANES), nbs).reshape(
        nbs, pad_f, LANES)[:, :dseq, :heads]
    rows_s = dseq * heads
    q4 = q_s.reshape(nbs, dseq, heads, HEAD_DIM)
    eye_h = jnp.eye(heads, dtype=BF16)
    q_blk = (q4[:, :, :, None, :] * eye_h[None, None, :, :, None]).reshape(nbs, rows_s, fox_w)
    k_new = jnp.pad(k_s.reshape(nbs, dseq, fox_w), ((0, 0), (0, page - dseq), (0, 0))).astype(BF16)
    v_new = jnp.pad(v_s.reshape(nbs, dseq, fox_w), ((0, 0), (0, page - dseq), (0, 0))).astype(BF16)
    fn_rows = jnp.broadcast_to(fn.reshape(nbs, rows_s, 1), (nbs, rows_s, page))
    fs_rows = jnp.broadcast_to(fn.transpose(0, 2, 1)[:, None], (nbs, dseq, heads, dseq))
    fs_rows = jnp.pad(fs_rows.reshape(nbs, rows_s, dseq), ((0, 0), (0, 0), (0, page - dseq)))
    lf_pool_t = cache_logf[0].transpose(0, 2, 1)
    oa_full = _fox_sample(page_table, q_blk, cache_k[0].reshape(n_pool, page, fox_w),
                          cache_v[0].reshape(n_pool, page, fox_w), lf_pool_t, k_new, v_new,
                          fn_rows, fs_rows, dseq)
    oa5 = oa_full.reshape(nbs, dseq, heads, heads, HEAD_DIM)
    oa_s = jnp.diagonal(oa5, axis1=2, axis2=3).transpose(0, 1, 3, 2).reshape(ms, fox_w).astype(BF16)

    sn = sgu_norm[0].reshape(1, sgu_width)
    rows_p = min(seq, SGU_CHUNK)
    (ob_p,) = _sgu(uv_p, sn, sgu_w[0][:, :rows_p, :rows_p], sgu_b[0][:, :rows_p].T, rows_p, False)
    rows_g = min(dseq, SGU_CHUNK)
    w_small = sgu_w[0][:, :rows_g, :rows_g]
    w_big = jnp.einsum("ab,gts->gatbs", jnp.eye(nbs, dtype=F32), w_small).reshape(groups, ms, ms)
    b_big = jnp.tile(sgu_b[0][:, :rows_g].T, (nbs, 1))
    ob_s, sgu_v = _sgu(uv_s, sn, w_big, b_big, ms, True)

    xp, xs = residual_proj([oa_p, ob_p], [oa_s, ob_s], w_out0[0], xp, xs, g1p, g1s, "out0")
    xp, xs = mlp(xp, xs, 0, sh2p, sc2p, g2p, sh2s, sc2s, g2s)

    (sh1p, sc1p, g1p, sh2p, sc2p, g2p), (sh1s, sc1s, g1s, sh2s, sc2s, g2s) = mods(1)
    hp, hs = modulate_both(xp, xs, sh1p, sc1p, sh1s, sc1s)
    w1 = w_in1[0]
    cw = conv_width
    (z_p,), (z_s,) = _proj([hp], [hs], [(w1, 0), (w1, cw)], cw, _epi_product, [F32], tn=256,
                           name="in1_conv_z")
    (gb_p,), (gb_s,) = _proj([hp], [hs], [(w1, 2 * cw)], cw, _epi_plain, [BF16], name="in1_conv_b")
    (qkv_p,), (qkv_s,) = _proj([hp], [hs], [(w1, 3 * cw)], 2 * gla_key + gla_val, _epi_plain,
                               [BF16], name="in1_qkv")
    c_ga = 3 * cw + 2 * gla_key + gla_val
    w_ga = _pad_cols(w1[:, c_ga:c_ga + gla_rank], LANES)
    (ga_p,), (ga_s,) = _proj([hp], [hs], [(w_ga, 0)], LANES, _epi_plain, [BF16], name="in1_ga")
    w_a2 = jnp.pad(gla_a_w2[0], ((0, LANES - gla_rank), (0, 0)))
    (la_p,), (la_s,) = _proj([ga_p], [ga_s], [(w_a2, 0)], gla_key, _epi_log_decay, [F32],
                             ext=[("bias", "col", gla_a_b[0].reshape(1, gla_key))], name="in1_log_a")
    w_gg = w1[:, c_ga + gla_rank:]
    (gg_p,), (gg_s,) = _proj([hp], [hs], [(w_gg, 0)], gla_val, _epi_silu, [BF16], name="in1_gate")

    cwt = conv_w[0]
    oc_p = _conv_prompt(z_p, gb_p, cwt, nbp, seq)
    conv_p = z_p.reshape(nbp, seq, cw)[:, seq - (cwt.shape[0] - 1):, :]
    zp_s = jnp.concatenate([state_conv[0], z_s.reshape(nbs, dseq, cw)], axis=1)
    shifted = [zp_s[:, i:i + dseq].reshape(ms, cw) for i in range(cwt.shape[0])]
    oc_s = _conv_sample(shifted[0], shifted[1], shifted[2], gb_s, cwt)
    conv_s = zp_s[:, dseq:, :]

    gn = gla_norm[0].reshape(1, gla_val)
    s0_p = jnp.zeros((nbp, gla_heads, gla_dv, gla_dk), F32)
    od_p, st_p = _gla(qkv_p, la_p, gg_p, gn, s0_p, nbp, seq, gla_heads, GLA_CHUNK_ROWS)
    pad_t = 16

    def pad_rows(a):
        return jnp.pad(a.reshape(nbs, dseq, -1), ((0, 0), (0, pad_t - dseq), (0, 0))).reshape(
            nbs * pad_t, -1)

    od_s_pad, st_s = _gla(pad_rows(qkv_s), pad_rows(la_s), pad_rows(gg_s), gn,
                          state_gla[0].swapaxes(-1, -2), nbs, pad_t, gla_heads, pad_t)
    od_s = od_s_pad.reshape(nbs, pad_t, gla_val)[:, :dseq].reshape(ms, gla_val)

    xp, xs = residual_proj([oc_p, od_p], [oc_s, od_s], w_out1[0], xp, xs, g1p, g1s, "out1")
    xp, xs = mlp(xp, xs, 1, sh2p, sc2p, g2p, sh2s, sc2s, g2s)

    y_prompt = xp.reshape(nbp, seq, d)
    y_sample = xs.reshape(nbs, dseq, d)
    return (y_prompt, y_sample,
            k_p.reshape(1, nbp, seq, heads, HEAD_DIM), v_p.reshape(1, nbp, seq, heads, HEAD_DIM),
            lf_p[:, :heads].reshape(1, nbp, seq, heads),
            k_s.reshape(1, nbs, dseq, heads, HEAD_DIM), v_s.reshape(1, nbs, dseq, heads, HEAD_DIM),
            lf_s16[None], sgu_v.reshape(1, nbs, dseq, sgu_width),
            conv_p[None], conv_s[None],
            st_p.swapaxes(-1, -2)[None], st_s.swapaxes(-1, -2)[None])
```

```python
import functools

import jax
import jax.numpy as jnp
from jax import lax
from jax.experimental import pallas as pl
from jax.experimental.pallas import tpu as pltpu

F32 = jnp.float32
BF16 = jnp.bfloat16
EPS = 1e-6
NEG_INF = -1e30
HEAD_DIM = 128
LANES = 128
SGU_CHUNK = 128
GLA_TAU = 16.0
GLA_CHUNK_ROWS = 64
VMEM_LIMIT_BYTES = 56 * 1024 * 1024


def _params(*semantics):
    return pltpu.CompilerParams(dimension_semantics=semantics,
                                vmem_limit_bytes=VMEM_LIMIT_BYTES)


def _split3(x):
    hi = x.astype(BF16)
    r = x - hi.astype(F32)
    mid = r.astype(BF16)
    lo = (r - mid.astype(F32)).astype(BF16)
    return hi, mid, lo


def _log_sigmoid(x):
    return jnp.minimum(x, 0.0) - jnp.log1p(jnp.exp(-jnp.abs(x)))


def _gelu_tanh(x):
    c = 0.7978845608028654
    return 0.5 * x * (1.0 + jnp.tanh(c * (x + 0.044715 * (x * x * x))))


def _ada_kernel(c_ref, w_ref, b_ref, o_ref):
    o_ref[...] = jnp.dot(c_ref[...], w_ref[...].astype(BF16),
                         preferred_element_type=F32) + b_ref[...]


def _ada(c_rows, ada_w, ada_b, tn=512):
    n_layers, d, n = ada_w.shape
    r = c_rows.shape[0]
    tn = min(tn, n)
    return pl.pallas_call(
        _ada_kernel,
        grid=(n_layers, n // tn),
        in_specs=[pl.BlockSpec((r, d), lambda l, j: (0, 0)),
                  pl.BlockSpec((None, d, tn), lambda l, j: (l, 0, j)),
                  pl.BlockSpec((None, 1, tn), lambda l, j: (l, 0, j))],
        out_specs=pl.BlockSpec((None, r, tn), lambda l, j: (l, 0, j)),
        out_shape=jax.ShapeDtypeStruct((n_layers, r, n), F32),
        compiler_params=_params("arbitrary", "arbitrary"),
        name="ada",
    )(c_rows, ada_w, ada_b.reshape(n_layers, 1, n))


def _modulate_kernel(x_ref, sh_ref, sc_ref, o_ref):
    x = x_ref[...]
    y = x * lax.rsqrt(jnp.mean(x * x, axis=-1, keepdims=True) + EPS)
    o_ref[...] = (y * (1.0 + sc_ref[...]) + sh_ref[...]).astype(o_ref.dtype)


def _modulate(x, sh, sc, rows_per_group, tr=256):
    rows, d = x.shape
    tr = min(tr, rows)
    r = sh.shape[1]
    grp = lambda i: ((i * tr) // rows_per_group, 0, 0)
    return pl.pallas_call(
        _modulate_kernel,
        grid=(rows // tr,),
        in_specs=[pl.BlockSpec((tr, d), lambda i: (i, 0)),
                  pl.BlockSpec((None, r, d), grp),
                  pl.BlockSpec((None, r, d), grp)],
        out_specs=pl.BlockSpec((tr, d), lambda i: (i, 0)),
        out_shape=jax.ShapeDtypeStruct((rows, d), BF16),
        compiler_params=_params("arbitrary"),
        name="modulate",
    )(x, sh, sc)


def _proj_kernel(*refs, k_sizes, n_w, ext_names, n_out, epilogue, cast_rows):
    n_lhs = len(k_sizes)
    pos = 0
    xp = refs[pos:pos + n_lhs]; pos += n_lhs
    xs = refs[pos:pos + n_lhs]; pos += n_lhs
    w = refs[pos:pos + n_w]; pos += n_w
    ext = dict(zip(ext_names, refs[pos:pos + len(ext_names)])); pos += len(ext_names)
    out_p = refs[pos:pos + n_out]; pos += n_out
    out_s = refs[pos:pos + n_out]; pos += n_out
    wbf = refs[pos:pos + n_w]

    def accumulate(lhs):
        accs = []
        for m in range(n_w):
            acc = None
            off = 0
            for a, ka in enumerate(k_sizes):
                part = jnp.dot(lhs[a][...], wbf[m][off:off + ka, :],
                               preferred_element_type=F32)
                acc = part if acc is None else acc + part
                off += ka
            accs.append(acc)
        return accs

    @pl.when(pl.program_id(1) == 0)
    def _():
        k_total = sum(k_sizes)
        for m in range(n_w):
            for r0 in range(0, k_total, cast_rows):
                wbf[m][r0:r0 + cast_rows, :] = w[m][r0:r0 + cast_rows, :].astype(BF16)
        for o_ref, val in zip(out_s, epilogue(accumulate(xs), ext, True)):
            o_ref[...] = val.astype(o_ref.dtype)

    for o_ref, val in zip(out_p, epilogue(accumulate(xp), ext, False)):
        o_ref[...] = val.astype(o_ref.dtype)


def _proj(xp, xs, weights, n_cols, epilogue, out_dtypes, ext=(), *, tm=1024, tn=512,
          rows_per_group=None, name="proj"):
    mp, ms = xp[0].shape[0], xs[0].shape[0]
    k_sizes = tuple(int(a.shape[1]) for a in xp)
    k_total = sum(k_sizes)
    tm = min(tm, mp, rows_per_group or mp)
    tn = min(tn, n_cols)
    assert mp % tm == 0 and n_cols % tn == 0
    for warr, _, c0 in weights:
        assert warr.shape[-2] == k_total and c0 % tn == 0
    in_specs, args = [], []
    for a in xp:
        in_specs.append(pl.BlockSpec((tm, a.shape[1]), lambda j, i: (i, 0)))
        args.append(a)
    for a in xs:
        in_specs.append(pl.BlockSpec((ms, a.shape[1]), lambda j, i: (0, 0)))
        args.append(a)
    for warr, layer, c0 in weights:
        if warr.ndim == 3:
            spec = pl.BlockSpec((None, k_total, tn),
                                lambda j, i, cb=c0 // tn, l=layer: (l, 0, j + cb))
        else:
            spec = pl.BlockSpec((k_total, tn), lambda j, i, cb=c0 // tn: (0, j + cb))
        in_specs.append(spec)
        args.append(warr)
    ext_names = []
    for ename, kind, arr in ext:
        ext_names.append(ename)
        if kind == "col":
            spec = pl.BlockSpec((arr.shape[0], tn), lambda j, i: (0, j))
        elif kind == "prow":
            spec = pl.BlockSpec((tm, tn), lambda j, i: (i, j))
        elif kind == "pgrp":
            spec = pl.BlockSpec((None, 1, tn),
                                lambda j, i: ((i * tm) // rows_per_group, 0, j))
        elif kind == "srow":
            spec = pl.BlockSpec((ms, tn), lambda j, i: (0, j))
        else:
            raise ValueError(kind)
        in_specs.append(spec)
        args.append(arr)
    n_out = len(out_dtypes)
    out_specs = ([pl.BlockSpec((tm, tn), lambda j, i: (i, j))] * n_out
                 + [pl.BlockSpec((ms, tn), lambda j, i: (0, j))] * n_out)
    out_shape = ([jax.ShapeDtypeStruct((mp, n_cols), dt) for dt in out_dtypes]
                 + [jax.ShapeDtypeStruct((ms, n_cols), dt) for dt in out_dtypes])
    cast_rows = min(512, k_total)
    assert k_total % cast_rows == 0
    outs = pl.pallas_call(
        functools.partial(_proj_kernel, k_sizes=k_sizes, n_w=len(weights),
                          ext_names=tuple(ext_names), n_out=n_out, epilogue=epilogue,
                          cast_rows=cast_rows),
        grid=(n_cols // tn, mp // tm),
        in_specs=in_specs,
        out_specs=out_specs,
        out_shape=out_shape,
        scratch_shapes=[pltpu.VMEM((k_total, tn), BF16) for _ in weights],
        compiler_params=_params("arbitrary", "arbitrary"),
        name=name,
    )(*args)
    return outs[:n_out], outs[n_out:]


def _epi_plain(accs, ext, is_sample):
    return (accs[0],)


def _epi_head_norm(accs, ext, is_sample):
    acc = accs[0]
    wn = ext["norm"][...]
    parts = []
    for c0 in range(0, acc.shape[1], HEAD_DIM):
        a = acc[:, c0:c0 + HEAD_DIM]
        y = a * lax.rsqrt(jnp.mean(a * a, axis=-1, keepdims=True) + EPS)
        parts.append(y * wn[:, c0:c0 + HEAD_DIM])
    return (jnp.concatenate(parts, axis=1),)


def _epi_log_forget(accs, ext, is_sample):
    return (_log_sigmoid(accs[0] + ext["bias"][...]),)


def _epi_gelu(accs, ext, is_sample):
    return (_gelu_tanh(accs[0]),)


def _epi_silu(accs, ext, is_sample):
    a = accs[0]
    return (a * jax.nn.sigmoid(a),)


def _epi_relu2(accs, ext, is_sample):
    r = jnp.maximum(accs[0], 0.0)
    return (r * r,)


def _epi_product(accs, ext, is_sample):
    return (accs[1] * accs[0],)


def _epi_log_decay(accs, ext, is_sample):
    return (_log_sigmoid(accs[0] + ext["bias"][...]) / GLA_TAU,)


def _epi_residual(accs, ext, is_sample):
    if is_sample:
        return (ext["res_s"][...] + ext["gate_s"][...] * accs[0],)
    return (ext["res_p"][...] + ext["gate_p"][...] * accs[0],)


def _down_kernel(xp_ref, xs_ref, w_ref, resp_ref, gp_ref, ress_ref, gs_ref,
                 op_ref, os_ref, accs_ref, *, nk):
    i = pl.program_id(1)
    k = pl.program_id(2)
    wb = w_ref[...].astype(BF16)

    @pl.when(k == 0)
    def _():
        op_ref[...] = jnp.zeros_like(op_ref)

    op_ref[...] += jnp.dot(xp_ref[...], wb, preferred_element_type=F32)

    @pl.when(k == nk - 1)
    def _():
        op_ref[...] = resp_ref[...] + gp_ref[...] * op_ref[...]

    @pl.when(i == 0)
    def _():
        ps = jnp.dot(xs_ref[...], wb, preferred_element_type=F32)

        @pl.when(k == 0)
        def _():
            accs_ref[...] = ps

        @pl.when(k > 0)
        def _():
            accs_ref[...] += ps

        @pl.when(k == nk - 1)
        def _():
            os_ref[...] = ress_ref[...] + gs_ref[...] * accs_ref[...]


def _down(xp, xs, w, layer, res_p, gate_p, res_s, gate_s, rows_per_group, *,
          tm=1024, tn=1024, tk=2048):
    mp, kdim = xp.shape
    ms = xs.shape[0]
    n = w.shape[2]
    tm, tn, tk = min(tm, rows_per_group, mp), min(tn, n), min(tk, kdim)
    nk = kdim // tk
    return pl.pallas_call(
        functools.partial(_down_kernel, nk=nk),
        grid=(n // tn, mp // tm, nk),
        in_specs=[pl.BlockSpec((tm, tk), lambda j, i, k: (i, k)),
                  pl.BlockSpec((ms, tk), lambda j, i, k: (0, k)),
                  pl.BlockSpec((None, tk, tn), lambda j, i, k: (layer, k, j)),
                  pl.BlockSpec((tm, tn), lambda j, i, k: (i, j)),
                  pl.BlockSpec((None, 1, tn),
                               lambda j, i, k: ((i * tm) // rows_per_group, 0, j)),
                  pl.BlockSpec((ms, tn), lambda j, i, k: (0, j)),
                  pl.BlockSpec((ms, tn), lambda j, i, k: (0, j))],
        out_specs=[pl.BlockSpec((tm, tn), lambda j, i, k: (i, j)),
                   pl.BlockSpec((ms, tn), lambda j, i, k: (0, j))],
        out_shape=[jax.ShapeDtypeStruct((mp, n), F32),
                   jax.ShapeDtypeStruct((ms, n), F32)],
        scratch_shapes=[pltpu.VMEM((ms, tn), F32)],
        compiler_params=_params("arbitrary", "arbitrary", "arbitrary"),
        name="mlp_down",
    )(xp, xs, w, res_p, gate_p, res_s, gate_s)


def _cumsum_kernel(x_ref, o_ref, carry_ref):
    @pl.when(pl.program_id(1) == 0)
    def _():
        carry_ref[...] = jnp.zeros_like(carry_ref)

    x = x_ref[...]
    t, w = x.shape
    r = lax.broadcasted_iota(jnp.int32, (t, t), 0)
    c = lax.broadcasted_iota(jnp.int32, (t, t), 1)
    tri = jnp.where(c <= r, 1.0, 0.0).astype(BF16)
    y = jnp.dot(tri, jnp.concatenate(_split3(x), axis=1), preferred_element_type=F32)
    out = y[:, :w] + y[:, w:2 * w] + y[:, 2 * w:] + carry_ref[...]
    o_ref[...] = out
    carry_ref[...] = out[t - 1:t, :]


def _cumsum_rows(x, n_groups, tc=256):
    rows, w = x.shape
    per = rows // n_groups
    tc = min(tc, per)
    nt = per // tc
    return pl.pallas_call(
        _cumsum_kernel,
        grid=(n_groups, nt),
        in_specs=[pl.BlockSpec((tc, w), lambda b, t: (b * nt + t, 0))],
        out_specs=pl.BlockSpec((tc, w), lambda b, t: (b * nt + t, 0)),
        out_shape=jax.ShapeDtypeStruct((rows, w), F32),
        scratch_shapes=[pltpu.VMEM((1, w), F32)],
        compiler_params=_params("arbitrary", "arbitrary"),
        name="cumsum_logf",
    )(x)


def _bias_lanes(col, own_first):
    hi, mid, lo = _split3(col)
    lane = lax.broadcasted_iota(jnp.int32, (col.shape[0], LANES), 1)
    own, other = (0, 3) if own_first else (3, 0)
    x = jnp.where(lane == own, hi.astype(F32),
                  jnp.where(lane == own + 1, mid.astype(F32),
                            jnp.where(lane == own + 2, lo.astype(F32), 0.0)))
    x = jnp.where((lane >= other) & (lane < other + 3), 1.0, x)
    return x.astype(BF16)


def _fox_prompt_kernel(q_ref, k_ref, v_ref, fq_ref, fk_ref, o_ref, kaug, vbf, *, tq, scale):
    h = pl.program_id(1)
    qi = pl.program_id(2)

    def head_column(f):
        lane = lax.broadcasted_iota(jnp.int32, f.shape, 1)
        return jnp.sum(jnp.where(lane == h, f, 0.0), axis=1, keepdims=True)

    @pl.when(qi == 0)
    def _():
        kaug[:, :HEAD_DIM] = k_ref[...].astype(BF16)
        kaug[:, HEAD_DIM:] = _bias_lanes(head_column(fk_ref[...]) * (-1.0 / scale), False)
        vbf[...] = v_ref[...].astype(BF16)

    q = jnp.concatenate(
        [q_ref[...], _bias_lanes(head_column(fq_ref[...]) * (1.0 / scale), True)], axis=1)
    c2 = scale * 1.4426950408889634

    def step(j, carry, masked):
        m, l, acc = carry
        start = pl.multiple_of(j * tq, tq)
        s = lax.dot_general(q, kaug[pl.ds(start, tq), :], (((1,), (1,)), ((), ())),
                            preferred_element_type=F32) * c2
        if masked:
            r = lax.broadcasted_iota(jnp.int32, (tq, tq), 0)
            c = lax.broadcasted_iota(jnp.int32, (tq, tq), 1)
            s = jnp.where(c <= r, s, NEG_INF)
        m_new = jnp.maximum(m, jnp.max(s, axis=1, keepdims=True))
        alpha = jnp.exp2(m - m_new)
        p = jnp.exp2(s - m_new)
        l = alpha * l + jnp.sum(p, axis=1, keepdims=True)
        acc = alpha * acc + jnp.dot(p.astype(BF16), vbf[pl.ds(start, tq), :],
                                    preferred_element_type=F32)
        return m_new, l, acc

    init = (jnp.full((tq, 1), NEG_INF, F32), jnp.zeros((tq, 1), F32),
            jnp.zeros((tq, HEAD_DIM), F32))
    carry = lax.fori_loop(0, qi, functools.partial(step, masked=False), init)
    m, l, acc = step(qi, carry, True)
    o_ref[...] = (acc / l).astype(o_ref.dtype)


def _fox_prompt(q, k, v, f_col, batch, seq, heads, tq=512):
    tq = min(tq, seq)
    nq = seq // tq
    return pl.pallas_call(
        functools.partial(_fox_prompt_kernel, tq=tq, scale=HEAD_DIM ** -0.5),
        grid=(batch, heads, nq),
        in_specs=[pl.BlockSpec((tq, HEAD_DIM), lambda b, h, i: (b * nq + i, h)),
                  pl.BlockSpec((seq, HEAD_DIM), lambda b, h, i: (b, h)),
                  pl.BlockSpec((seq, HEAD_DIM), lambda b, h, i: (b, h)),
                  pl.BlockSpec((tq, LANES), lambda b, h, i: (b * nq + i, 0)),
                  pl.BlockSpec((seq, LANES), lambda b, h, i: (b, 0))],
        out_specs=pl.BlockSpec((tq, HEAD_DIM), lambda b, h, i: (b * nq + i, h)),
        out_shape=jax.ShapeDtypeStruct(q.shape, BF16),
        scratch_shapes=[pltpu.VMEM((seq, 2 * HEAD_DIM), BF16), pltpu.VMEM((seq, HEAD_DIM), BF16)],
        compiler_params=_params("arbitrary", "arbitrary", "arbitrary"),
        name="fox_prompt",
    )(q, k, v, f_col, f_col)


def _fox_sample_kernel(pt_ref, q_ref, ck_ref, cv_ref, lf_ref, kn_ref, vn_ref, fn_ref, fs_ref,
                       o_ref, m_ref, l_ref, acc_ref, tail_ref, *, n_pages, tpad, scale):
    del pt_ref
    p = pl.program_id(1)

    @pl.when(p == 0)
    def _():
        m_ref[...] = jnp.full(m_ref.shape, NEG_INF, F32)
        l_ref[...] = jnp.zeros_like(l_ref)
        acc_ref[...] = jnp.zeros_like(acc_ref)
        tail_ref[...] = jnp.zeros_like(tail_ref)

    q = q_ref[...]
    lf = lf_ref[...]
    heads, page = lf.shape

    def attend(bias, key_of, val_of):
        s = jnp.concatenate(
            [lax.dot_general(q[h * tpad:(h + 1) * tpad], key_of(h), (((1,), (1,)), ((), ())),
                             preferred_element_type=F32) for h in range(heads)], axis=0)
        s = s * scale + bias
        m_old = m_ref[...]
        m_new = jnp.maximum(m_old, jnp.max(s, axis=1, keepdims=True))
        alpha = jnp.exp(m_old - m_new)
        pr = jnp.exp(s - m_new)
        l_ref[...] = alpha * l_ref[...] + jnp.sum(pr, axis=1, keepdims=True)
        prb = pr.astype(BF16)
        pv = jnp.concatenate(
            [jnp.dot(prb[h * tpad:(h + 1) * tpad], val_of(h), preferred_element_type=F32)
             for h in range(heads)], axis=0)
        acc_ref[...] = alpha * acc_ref[...] + pv
        m_ref[...] = m_new

    jj = lax.broadcasted_iota(jnp.int32, (page, page), 0)
    ss = lax.broadcasted_iota(jnp.int32, (page, page), 1)
    later = jnp.where(jj > ss, 1.0, 0.0).astype(BF16)
    y = jnp.dot(jnp.concatenate(_split3(lf), axis=0), later, preferred_element_type=F32)
    r_in = y[:heads] + y[heads:2 * heads] + y[2 * heads:] + tail_ref[...]
    tail_ref[...] = tail_ref[...] + jnp.sum(lf, axis=1, keepdims=True)
    r_rows = jnp.broadcast_to(r_in[:, None, :], (heads, tpad, page)).reshape(heads * tpad, page)

    attend(fn_ref[...] + r_rows,
           lambda h: ck_ref[pl.ds(h, page, stride=heads), :].astype(BF16),
           lambda h: cv_ref[pl.ds(h, page, stride=heads), :].astype(BF16))

    @pl.when(p == n_pages - 1)
    def _():
        slot = lax.broadcasted_iota(jnp.int32, (heads * tpad, page), 0) & (tpad - 1)
        key = lax.broadcasted_iota(jnp.int32, (heads * tpad, page), 1)
        bias = jnp.where(key <= slot, fn_ref[...] - fs_ref[...], NEG_INF)
        attend(bias, lambda h: kn_ref[h], lambda h: vn_ref[h])
        o_ref[...] = acc_ref[...] / l_ref[...]


def _fox_sample(page_table, q_rows, cache_k, cache_v, cache_lf_t, k_new, v_new, fn_rows, fs_rows,
                tpad):
    nb, n_pages = page_table.shape
    rows, hd = q_rows.shape[1], q_rows.shape[2]
    heads, page = cache_lf_t.shape[1], cache_lf_t.shape[2]
    per_b = lambda b, p, pt: (b, 0, 0)
    page_rows = lambda b, p, pt: (pt[b, n_pages - 1 - p], 0)
    grid_spec = pltpu.PrefetchScalarGridSpec(
        num_scalar_prefetch=1,
        grid=(nb, n_pages),
        in_specs=[pl.BlockSpec((None, rows, hd), per_b),
                  pl.BlockSpec((page * heads, hd), page_rows),
                  pl.BlockSpec((page * heads, hd), page_rows),
                  pl.BlockSpec((None, heads, page), lambda b, p, pt: (pt[b, n_pages - 1 - p], 0, 0)),
                  pl.BlockSpec((None, heads, page, hd), lambda b, p, pt: (b, 0, 0, 0)),
                  pl.BlockSpec((None, heads, page, hd), lambda b, p, pt: (b, 0, 0, 0)),
                  pl.BlockSpec((None, rows, page), per_b),
                  pl.BlockSpec((None, rows, page), per_b)],
        out_specs=pl.BlockSpec((None, rows, hd), per_b),
        scratch_shapes=[pltpu.VMEM((rows, 1), F32), pltpu.VMEM((rows, 1), F32),
                        pltpu.VMEM((rows, hd), F32), pltpu.VMEM((heads, 1), F32)],
    )
    return pl.pallas_call(
        functools.partial(_fox_sample_kernel, n_pages=n_pages, tpad=tpad,
                          scale=HEAD_DIM ** -0.5),
        grid_spec=grid_spec,
        out_shape=jax.ShapeDtypeStruct((nb, rows, hd), F32),
        compiler_params=_params("arbitrary", "arbitrary"),
        name="fox_sample",
    )(page_table, q_rows, cache_k, cache_v, cache_lf_t, k_new, v_new, fn_rows, fs_rows)


def _sgu_kernel(u_ref, vg_ref, nw_ref, w_ref, bt_ref, *out_refs, groups):
    o_ref = out_refs[0]
    g = vg_ref[...].astype(F32)
    v = g * lax.rsqrt(jnp.mean(g * g, axis=-1, keepdims=True) + EPS) * nw_ref[...]
    if len(out_refs) > 1:
        out_refs[1][...] = v
    vb = v.astype(BF16)
    rows = v.shape[0]
    cw = v.shape[1] // groups
    r = lax.broadcasted_iota(jnp.int32, (rows, rows), 0)
    c = lax.broadcasted_iota(jnp.int32, (rows, rows), 1)
    bt = bt_ref[...]
    for gi in range(groups):
        wm = jnp.where(c <= r, w_ref[gi], 0.0).astype(BF16)
        z = jnp.dot(wm, vb[:, gi * cw:(gi + 1) * cw], preferred_element_type=F32)
        z = z + bt[:, gi:gi + 1]
        u = u_ref[:, gi * cw:(gi + 1) * cw].astype(F32)
        o_ref[:, gi * cw:(gi + 1) * cw] = (u * z).astype(o_ref.dtype)


def _sgu(uv, norm_w, w_pos, bias_t, rows, emit_v):
    m, two_w = uv.shape
    width = two_w // 2
    groups = w_pos.shape[0]
    out_shape = [jax.ShapeDtypeStruct((m, width), BF16)]
    out_specs = [pl.BlockSpec((rows, width), lambda i: (i, 0))]
    if emit_v:
        out_shape.append(jax.ShapeDtypeStruct((m, width), F32))
        out_specs.append(pl.BlockSpec((rows, width), lambda i: (i, 0)))
    return pl.pallas_call(
        functools.partial(_sgu_kernel, groups=groups),
        grid=(m // rows,),
        in_specs=[pl.BlockSpec((rows, width), lambda i: (i, 0)),
                  pl.BlockSpec((rows, width), lambda i: (i, 1)),
                  pl.BlockSpec((1, width), lambda i: (0, 0)),
                  pl.BlockSpec((groups, rows, rows), lambda i: (0, 0, 0)),
                  pl.BlockSpec((rows, groups), lambda i: (0, 0))],
        out_specs=out_specs,
        out_shape=out_shape,
        compiler_params=_params("arbitrary"),
        name="sgu",
    )(uv, uv, norm_w, w_pos, bias_t)


def _conv_prompt_kernel(z_ref, gb_ref, w_ref, o_ref, zbuf, *, tr):
    @pl.when(pl.program_id(2) == 0)
    def _():
        zbuf[0:8, :] = jnp.zeros((8, zbuf.shape[1]), F32)

    zbuf[8:8 + tr, :] = z_ref[...]
    w = w_ref[...]
    y = (w[0:1] * zbuf[6:6 + tr, :] + w[1:2] * zbuf[7:7 + tr, :] + w[2:3] * zbuf[8:8 + tr, :])
    o_ref[...] = (gb_ref[...].astype(F32) * y).astype(o_ref.dtype)
    zbuf[0:8, :] = zbuf[tr:tr + 8, :]


def _conv_prompt(z, gb, conv_w, batch, seq, tr=256, tc=512):
    m, width = z.shape
    tr, tc = min(tr, seq), min(tc, width)
    nt = seq // tr
    return pl.pallas_call(
        functools.partial(_conv_prompt_kernel, tr=tr),
        grid=(width // tc, batch, nt),
        in_specs=[pl.BlockSpec((tr, tc), lambda c, b, t: (b * nt + t, c)),
                  pl.BlockSpec((tr, tc), lambda c, b, t: (b * nt + t, c)),
                  pl.BlockSpec((conv_w.shape[0], tc), lambda c, b, t: (0, c))],
        out_specs=pl.BlockSpec((tr, tc), lambda c, b, t: (b * nt + t, c)),
        out_shape=jax.ShapeDtypeStruct((m, width), BF16),
        scratch_shapes=[pltpu.VMEM((tr + 8, tc), F32)],
        compiler_params=_params("arbitrary", "arbitrary", "arbitrary"),
        name="conv_prompt",
    )(z, gb, conv_w)


def _conv_sample_kernel(z0_ref, z1_ref, z2_ref, gb_ref, w_ref, o_ref):
    w = w_ref[...]
    y = w[0:1] * z0_ref[...] + w[1:2] * z1_ref[...] + w[2:3] * z2_ref[...]
    o_ref[...] = (gb_ref[...].astype(F32) * y).astype(o_ref.dtype)


def _conv_sample(z0, z1, z2, gb, conv_w):
    return pl.pallas_call(
        _conv_sample_kernel,
        out_shape=jax.ShapeDtypeStruct(z0.shape, BF16),
        name="conv_sample",
    )(z0, z1, z2, gb, conv_w)


def _gla_kernel(q_ref, k_ref, v_ref, la_ref, gate_ref, nw_ref, s0_ref, o_ref, st_ref, st, *, nc):
    ci = pl.program_id(2)

    @pl.when(ci == 0)
    def _():
        st[...] = s0_ref[...]

    la = la_ref[...]
    c, dk = la.shape
    r = lax.broadcasted_iota(jnp.int32, (c, c), 0)
    cc = lax.broadcasted_iota(jnp.int32, (c, c), 1)
    tril = cc <= r
    y = jnp.dot(jnp.where(tril, 1.0, 0.0).astype(BF16),
                jnp.concatenate(_split3(la), axis=1), preferred_element_type=F32)
    bc = y[:, :dk] + y[:, dk:2 * dk] + y[:, 2 * dk:]
    q = q_ref[...].astype(F32) * dk ** -0.5
    k = k_ref[...].astype(F32)
    v = v_ref[...]
    qt = (q * jnp.exp(bc)).astype(BF16)
    kt = (k * jnp.exp(-bc)).astype(BF16)
    att = lax.dot_general(qt, kt, (((1,), (1,)), ((), ())), preferred_element_type=F32)
    att = jnp.where(tril, att, 0.0)
    s_t = st[...]
    o = (lax.dot_general(qt, s_t.astype(BF16), (((1,), (1,)), ((), ())),
                         preferred_element_type=F32)
         + jnp.dot(att.astype(BF16), v, preferred_element_type=F32))
    bl = bc[c - 1:c, :]
    kd = (k * jnp.exp(bl - bc)).astype(BF16)
    st[...] = s_t * jnp.exp(bl) + lax.dot_general(v, kd, (((0,), (0,)), ((), ())),
                                                   preferred_element_type=F32)
    on = o * lax.rsqrt(jnp.mean(o * o, axis=-1, keepdims=True) + EPS) * nw_ref[...]
    o_ref[...] = (on * gate_ref[...].astype(F32)).astype(o_ref.dtype)

    @pl.when(ci == nc - 1)
    def _():
        st_ref[...] = st[...]


def _gla(qkv, log_a, gate, norm_w, s0_t, batch, seq, heads, chunk):
    m = qkv.shape[0]
    dk = log_a.shape[1] // heads
    dv = gate.shape[1] // heads
    assert (2 * heads * dk) % dv == 0
    v0 = (2 * heads * dk) // dv
    chunk = min(chunk, seq)
    nc = seq // chunk
    row = lambda b, h, c: b * nc + c
    return pl.pallas_call(
        functools.partial(_gla_kernel, nc=nc),
        grid=(batch, heads, nc),
        in_specs=[pl.BlockSpec((chunk, dk), lambda b, h, c: (row(b, h, c), h)),
                  pl.BlockSpec((chunk, dk), lambda b, h, c: (row(b, h, c), heads + h)),
                  pl.BlockSpec((chunk, dv), lambda b, h, c: (row(b, h, c), v0 + h)),
                  pl.BlockSpec((chunk, dk), lambda b, h, c: (row(b, h, c), h)),
                  pl.BlockSpec((chunk, dv), lambda b, h, c: (row(b, h, c), h)),
                  pl.BlockSpec((1, dv), lambda b, h, c: (0, h)),
                  pl.BlockSpec((None, None, dv, dk), lambda b, h, c: (b, h, 0, 0))],
        out_specs=[pl.BlockSpec((chunk, dv), lambda b, h, c: (row(b, h, c), h)),
                   pl.BlockSpec((None, None, dv, dk), lambda b, h, c: (b, h, 0, 0))],
        out_shape=[jax.ShapeDtypeStruct((m, heads * dv), BF16),
                   jax.ShapeDtypeStruct((batch, heads, dv, dk), F32)],
        scratch_shapes=[pltpu.VMEM((dv, dk), F32)],
        compiler_params=_params("arbitrary", "arbitrary", "arbitrary"),
        name="gla",
    )(qkv, qkv, qkv, log_a, gate, norm_w, s0_t)


def _pad_cols(w, n):
    return jnp.pad(w, ((0, 0), (0, n - w.shape[1])))


def kernel(x_prompt, x_sample, cache_k, cache_v, cache_logf, state_conv, state_gla, page_table,
           c_prompt, c_sample, w_in0, f_bias, q_norm, k_norm, sgu_norm, sgu_w, sgu_b, w_out0,
           w_in1, conv_w, gla_a_w2, gla_a_b, gla_norm, w_out1, ada_w, ada_b, mlp_w1, mlp_w2):
    nbp, seq, d = x_prompt.shape
    nbs, dseq, _ = x_sample.shape
    mp, ms = nbp * seq, nbs * dseq
    fox_w = d // 2
    heads = fox_w // HEAD_DIM
    sgu_width = d // 2
    groups = sgu_w.shape[1]
    conv_width = state_conv.shape[-1]
    gla_heads, gla_dk, gla_dv = state_gla.shape[2], state_gla.shape[3], state_gla.shape[4]
    gla_key, gla_val = gla_heads * gla_dk, gla_heads * gla_dv
    gla_rank = gla_a_w2.shape[1]
    n_pool, page = cache_k.shape[1], cache_k.shape[2]
    n_pages = page_table.shape[1]

    r_c = nbp + nbs
    r_pad = -(-r_c // 16) * 16
    c_rows = jnp.pad(jnp.concatenate([c_prompt, c_sample], axis=0),
                     ((0, r_pad - r_c), (0, 0))).astype(BF16)
    mod = _ada(c_rows, ada_w, ada_b)

    def mods(layer):
        parts = jnp.split(mod[layer], 6, axis=-1)
        pp = [p[:nbp].reshape(nbp, 1, d) for p in parts]
        ps = [jnp.repeat(p[nbp:r_c], dseq, axis=0) for p in parts]
        return pp, ps

    xp = x_prompt.reshape(mp, d)
    xs = x_sample.reshape(ms, d)

    def modulate_both(xp, xs, shp, scp, shs, scs):
        hp = _modulate(xp, shp, scp, seq)
        hs = _modulate(xs, shs.reshape(1, ms, d), scs.reshape(1, ms, d), ms, tr=ms)
        return hp, hs

    def mlp(xp, xs, layer, shp, scp, gp, shs, scs, gs):
        hp, hs = modulate_both(xp, xs, shp, scp, shs, scs)
        (ap,), (as_,) = _proj([hp], [hs], [(mlp_w1, layer, 0)], mlp_w1.shape[2], _epi_relu2,
                              [BF16], name="mlp_up")
        return _down(ap, as_, mlp_w2, layer, xp, gp, xs, gs, seq)

    def residual_proj(lhs_p, lhs_s, w, xp, xs, gp, gs, name):
        (yp,), (ys,) = _proj(lhs_p, lhs_s, [(w, 0, 0)], d, _epi_residual, [F32],
                             ext=[("res_p", "prow", xp), ("gate_p", "pgrp", gp),
                                  ("res_s", "srow", xs), ("gate_s", "srow", gs)],
                             rows_per_group=seq, name=name)
        return yp, ys

    (sh1p, sc1p, g1p, sh2p, sc2p, g2p), (sh1s, sc1s, g1s, sh2s, sc2s, g2s) = mods(0)
    hp, hs = modulate_both(xp, xs, sh1p, sc1p, sh1s, sc1s)
    w0 = w_in0[0]
    qn = jnp.tile(q_norm[0], heads).reshape(1, fox_w)
    kn = jnp.tile(k_norm[0], heads).reshape(1, fox_w)
    (q_p,), (q_s,) = _proj([hp], [hs], [(w_in0, 0, 0)], fox_w, _epi_head_norm, [BF16],
                           ext=[("norm", "col", qn)], name="in0_q")
    (k_p,), (k_s,) = _proj([hp], [hs], [(w_in0, 0, fox_w)], fox_w, _epi_head_norm, [F32],
                           ext=[("norm", "col", kn)], name="in0_k")
    (v_p,), (v_s,) = _proj([hp], [hs], [(w_in0, 0, 2 * fox_w)], fox_w, _epi_plain, [F32],
                           name="in0_v")
    w_fl = _pad_cols(w0[:, 3 * fox_w:3 * fox_w + heads], LANES)
    fb = jnp.pad(f_bias[0], (0, LANES - heads)).reshape(1, LANES)
    (lf_p,), (lf_s,) = _proj([hp], [hs], [(w_fl, 0, 0)], LANES, _epi_log_forget, [F32],
                             ext=[("bias", "col", fb)], name="in0_logf")
    w_uv = w0[:, 3 * fox_w + heads:]
    (uv_p,), (uv_s,) = _proj([hp], [hs], [(w_uv, 0, 0)], 2 * sgu_width, _epi_gelu, [BF16],
                             name="in0_uv")

    f_col = _cumsum_rows(lf_p, nbp)
    oa_p = _fox_prompt(q_p, k_p, v_p, f_col, nbp, seq, heads)

    lf_s16 = lf_s[:, :heads].reshape(nbs, dseq, heads)
    tpad = 8
    lf_s_pad = jnp.pad(lf_s.reshape(nbs, dseq, LANES), ((0, 0), (0, tpad - dseq), (0, 0)))
    fn = _cumsum_rows(lf_s_pad.reshape(nbs * tpad, LANES), nbs).reshape(
        nbs, tpad, LANES)[:, :dseq, :heads]
    rows_s = heads * tpad
    fn_t = fn.transpose(0, 2, 1)

    def head_major(a, t_to):
        a = a.reshape(nbs, dseq, heads, HEAD_DIM).transpose(0, 2, 1, 3)
        return jnp.pad(a, ((0, 0), (0, 0), (0, t_to - dseq), (0, 0)))

    q_rows = head_major(q_s, tpad).reshape(nbs, rows_s, HEAD_DIM)
    k_new = head_major(k_s, page).astype(BF16)
    v_new = head_major(v_s, page).astype(BF16)
    fn_rows = jnp.broadcast_to(
        jnp.pad(fn_t, ((0, 0), (0, 0), (0, tpad - dseq))).reshape(nbs, rows_s, 1),
        (nbs, rows_s, page))
    fs_rows = jnp.broadcast_to(fn_t[:, :, None, :], (nbs, heads, tpad, dseq))
    fs_rows = jnp.pad(fs_rows.reshape(nbs, rows_s, dseq), ((0, 0), (0, 0), (0, page - dseq)))
    lf_pool_t = cache_logf[0].transpose(0, 2, 1)
    oa_rows = _fox_sample(page_table, q_rows, cache_k[0].reshape(n_pool * page * heads, HEAD_DIM),
                          cache_v[0].reshape(n_pool * page * heads, HEAD_DIM), lf_pool_t,
                          k_new, v_new, fn_rows, fs_rows, tpad)
    oa_s = (oa_rows.reshape(nbs, heads, tpad, HEAD_DIM)[:, :, :dseq].transpose(0, 2, 1, 3)
            .reshape(ms, fox_w).astype(BF16))

    sn = sgu_norm[0].reshape(1, sgu_width)
    rows_p = min(seq, SGU_CHUNK)
    (ob_p,) = _sgu(uv_p, sn, sgu_w[0][:, :rows_p, :rows_p], sgu_b[0][:, :rows_p].T, rows_p, False)
    rows_g = min(dseq, SGU_CHUNK)
    w_small = sgu_w[0][:, :rows_g, :rows_g]
    w_big = jnp.einsum("ab,gts->gatbs", jnp.eye(nbs, dtype=F32), w_small).reshape(groups, ms, ms)
    b_big = jnp.tile(sgu_b[0][:, :rows_g].T, (nbs, 1))
    ob_s, sgu_v = _sgu(uv_s, sn, w_big, b_big, ms, True)

    xp, xs = residual_proj([oa_p, ob_p], [oa_s, ob_s], w_out0, xp, xs, g1p, g1s, "out0")
    xp, xs = mlp(xp, xs, 0, sh2p, sc2p, g2p, sh2s, sc2s, g2s)

    (sh1p, sc1p, g1p, sh2p, sc2p, g2p), (sh1s, sc1s, g1s, sh2s, sc2s, g2s) = mods(1)
    hp, hs = modulate_both(xp, xs, sh1p, sc1p, sh1s, sc1s)
    w1 = w_in1[0]
    cw = conv_width
    (z_p,), (z_s,) = _proj([hp], [hs], [(w_in1, 0, 0), (w_in1, 0, cw)], cw, _epi_product, [F32], tn=256,
                           name="in1_conv_z")
    (gb_p,), (gb_s,) = _proj([hp], [hs], [(w_in1, 0, 2 * cw)], cw, _epi_plain, [BF16], name="in1_conv_b")
    (qkv_p,), (qkv_s,) = _proj([hp], [hs], [(w_in1, 0, 3 * cw)], 2 * gla_key + gla_val, _epi_plain,
                               [BF16], name="in1_qkv")
    c_ga = 3 * cw + 2 * gla_key + gla_val
    w_ga = _pad_cols(w1[:, c_ga:c_ga + gla_rank], LANES)
    (ga_p,), (ga_s,) = _proj([hp], [hs], [(w_ga, 0, 0)], LANES, _epi_plain, [BF16], name="in1_ga")
    w_a2 = jnp.pad(gla_a_w2[0], ((0, LANES - gla_rank), (0, 0)))
    (la_p,), (la_s,) = _proj([ga_p], [ga_s], [(w_a2, 0, 0)], gla_key, _epi_log_decay, [F32],
                             ext=[("bias", "col", gla_a_b[0].reshape(1, gla_key))], name="in1_log_a")
    w_gg = w1[:, c_ga + gla_rank:]
    (gg_p,), (gg_s,) = _proj([hp], [hs], [(w_gg, 0, 0)], gla_val, _epi_silu, [BF16], name="in1_gate")

    cwt = conv_w[0]
    oc_p = _conv_prompt(z_p, gb_p, cwt, nbp, seq)
    conv_p = z_p.reshape(nbp, seq, cw)[:, seq - (cwt.shape[0] - 1):, :]
    zp_s = jnp.concatenate([state_conv[0], z_s.reshape(nbs, dseq, cw)], axis=1)
    shifted = [zp_s[:, i:i + dseq].reshape(ms, cw) for i in range(cwt.shape[0])]
    oc_s = _conv_sample(shifted[0], shifted[1], shifted[2], gb_s, cwt)
    conv_s = zp_s[:, dseq:, :]

    gn = gla_norm[0].reshape(1, gla_val)
    s0_p = jnp.zeros((nbp, gla_heads, gla_dv, gla_dk), F32)
    od_p, st_p = _gla(qkv_p, la_p, gg_p, gn, s0_p, nbp, seq, gla_heads, GLA_CHUNK_ROWS)
    pad_t = 16

    def pad_rows(a):
        return jnp.pad(a.reshape(nbs, dseq, -1), ((0, 0), (0, pad_t - dseq), (0, 0))).reshape(
            nbs * pad_t, -1)

    od_s_pad, st_s = _gla(pad_rows(qkv_s), pad_rows(la_s), pad_rows(gg_s), gn,
                          state_gla[0].swapaxes(-1, -2), nbs, pad_t, gla_heads, pad_t)
    od_s = od_s_pad.reshape(nbs, pad_t, gla_val)[:, :dseq].reshape(ms, gla_val)

    xp, xs = residual_proj([oc_p, od_p], [oc_s, od_s], w_out1, xp, xs, g1p, g1s, "out1")
    xp, xs = mlp(xp, xs, 1, sh2p, sc2p, g2p, sh2s, sc2s, g2s)

    y_prompt = xp.reshape(nbp, seq, d)
    y_sample = xs.reshape(nbs, dseq, d)
    return (y_prompt, y_sample,
            k_p.reshape(1, nbp, seq, heads, HEAD_DIM), v_p.reshape(1, nbp, seq, heads, HEAD_DIM),
            lf_p[:, :heads].reshape(1, nbp, seq, heads),
            k_s.reshape(1, nbs, dseq, heads, HEAD_DIM), v_s.reshape(1, nbs, dseq, heads, HEAD_DIM),
            lf_s16[None], sgu_v.reshape(1, nbs, dseq, sgu_width),
            conv_p[None], conv_s[None],
            st_p.swapaxes(-1, -2)[None], st_s.swapaxes(-1, -2)[None])
```

```python
import functools
import math

import jax
import jax.numpy as jnp
from jax import lax
from jax.experimental import pallas as pl
from jax.experimental.pallas import tpu as pltpu

F32 = jnp.float32
BF16 = jnp.bfloat16
EPS = 1e-6
NEG_INF = -1e30
HEAD_DIM = 128
LANES = 128
SGU_CHUNK = 128
GLA_TAU = 16.0
GLA_CHUNK_ROWS = 128
FOX_PAGES_PER_STEP = 4
VMEM_LIMIT_BYTES = 56 * 1024 * 1024


def _params(*semantics):
    return pltpu.CompilerParams(dimension_semantics=semantics,
                                vmem_limit_bytes=VMEM_LIMIT_BYTES)


def _split3(x):
    hi = x.astype(BF16)
    r = x - hi.astype(F32)
    mid = r.astype(BF16)
    lo = (r - mid.astype(F32)).astype(BF16)
    return hi, mid, lo


def _log_sigmoid(x):
    return jnp.minimum(x, 0.0) - jnp.log1p(jnp.exp(-jnp.abs(x)))


def _gelu_tanh(x):
    c = 0.7978845608028654
    return 0.5 * x * (1.0 + jnp.tanh(c * (x + 0.044715 * (x * x * x))))


def _ada_kernel(c_ref, w_ref, b_ref, o_ref):
    o_ref[...] = jnp.dot(c_ref[...], w_ref[...].astype(BF16),
                         preferred_element_type=F32) + b_ref[...]


def _ada(c_rows, ada_w, ada_b, tn=512):
    n_layers, d, n = ada_w.shape
    r = c_rows.shape[0]
    tn = min(tn, n)
    return pl.pallas_call(
        _ada_kernel,
        grid=(n_layers, n // tn),
        in_specs=[pl.BlockSpec((r, d), lambda l, j: (0, 0)),
                  pl.BlockSpec((None, d, tn), lambda l, j: (l, 0, j)),
                  pl.BlockSpec((None, 1, tn), lambda l, j: (l, 0, j))],
        out_specs=pl.BlockSpec((None, r, tn), lambda l, j: (l, 0, j)),
        out_shape=jax.ShapeDtypeStruct((n_layers, r, n), F32),
        compiler_params=_params("arbitrary", "arbitrary"),
        name="ada",
    )(c_rows, ada_w, ada_b.reshape(n_layers, 1, n))


def _modulate_kernel(x_ref, sh_ref, sc_ref, o_ref):
    x = x_ref[...]
    y = x * lax.rsqrt(jnp.mean(x * x, axis=-1, keepdims=True) + EPS)
    o_ref[...] = (y * (1.0 + sc_ref[...]) + sh_ref[...]).astype(o_ref.dtype)


def _modulate(x, sh, sc, rows_per_group, tr=256):
    rows, d = x.shape
    tr = min(tr, rows)
    r = sh.shape[1]
    grp = lambda i: ((i * tr) // rows_per_group, 0, 0)
    return pl.pallas_call(
        _modulate_kernel,
        grid=(rows // tr,),
        in_specs=[pl.BlockSpec((tr, d), lambda i: (i, 0)),
                  pl.BlockSpec((None, r, d), grp),
                  pl.BlockSpec((None, r, d), grp)],
        out_specs=pl.BlockSpec((tr, d), lambda i: (i, 0)),
        out_shape=jax.ShapeDtypeStruct((rows, d), BF16),
        compiler_params=_params("arbitrary"),
        name="modulate",
    )(x, sh, sc)


def _proj_kernel(*refs, k_sizes, n_w, ext_names, n_out, epilogue, cast_rows):
    n_lhs = len(k_sizes)
    pos = 0
    xp = refs[pos:pos + n_lhs]; pos += n_lhs
    xs = refs[pos:pos + n_lhs]; pos += n_lhs
    w = refs[pos:pos + n_w]; pos += n_w
    ext = dict(zip(ext_names, refs[pos:pos + len(ext_names)])); pos += len(ext_names)
    out_p = refs[pos:pos + n_out]; pos += n_out
    out_s = refs[pos:pos + n_out]; pos += n_out
    wbf = refs[pos:pos + n_w]

    def accumulate(lhs):
        accs = []
        for m in range(n_w):
            acc = None
            off = 0
            for a, ka in enumerate(k_sizes):
                part = jnp.dot(lhs[a][...], wbf[m][off:off + ka, :],
                               preferred_element_type=F32)
                acc = part if acc is None else acc + part
                off += ka
            accs.append(acc)
        return accs

    @pl.when(pl.program_id(1) == 0)
    def _():
        k_total = sum(k_sizes)
        for m in range(n_w):
            for r0 in range(0, k_total, cast_rows):
                wbf[m][r0:r0 + cast_rows, :] = w[m][r0:r0 + cast_rows, :].astype(BF16)
        for o_ref, val in zip(out_s, epilogue(accumulate(xs), ext, True)):
            o_ref[...] = val.astype(o_ref.dtype)

    for o_ref, val in zip(out_p, epilogue(accumulate(xp), ext, False)):
        o_ref[...] = val.astype(o_ref.dtype)


def _proj(xp, xs, weights, n_cols, epilogue, out_dtypes, ext=(), *, tm=1024, tn=512,
          rows_per_group=None, name="proj"):
    mp, ms = xp[0].shape[0], xs[0].shape[0]
    k_sizes = tuple(int(a.shape[1]) for a in xp)
    k_total = sum(k_sizes)
    tm = min(tm, mp, rows_per_group or mp)
    tn = min(tn, n_cols)
    assert mp % tm == 0 and n_cols % tn == 0
    for warr, _, c0 in weights:
        assert warr.shape[-2] == k_total and c0 % tn == 0
    in_specs, args = [], []
    for a in xp:
        in_specs.append(pl.BlockSpec((tm, a.shape[1]), lambda j, i: (i, 0)))
        args.append(a)
    for a in xs:
        in_specs.append(pl.BlockSpec((ms, a.shape[1]), lambda j, i: (0, 0)))
        args.append(a)
    for warr, layer, c0 in weights:
        if warr.ndim == 3:
            spec = pl.BlockSpec((None, k_total, tn),
                                lambda j, i, cb=c0 // tn, l=layer: (l, 0, j + cb))
        else:
            spec = pl.BlockSpec((k_total, tn), lambda j, i, cb=c0 // tn: (0, j + cb))
        in_specs.append(spec)
        args.append(warr)
    ext_names = []
    for ename, kind, arr in ext:
        ext_names.append(ename)
        if kind == "col":
            spec = pl.BlockSpec((arr.shape[0], tn), lambda j, i: (0, j))
        elif kind == "prow":
            spec = pl.BlockSpec((tm, tn), lambda j, i: (i, j))
        elif kind == "pgrp":
            spec = pl.BlockSpec((None, 1, tn),
                                lambda j, i: ((i * tm) // rows_per_group, 0, j))
        elif kind == "srow":
            spec = pl.BlockSpec((ms, tn), lambda j, i: (0, j))
        else:
            raise ValueError(kind)
        in_specs.append(spec)
        args.append(arr)
    n_out = len(out_dtypes)
    out_specs = ([pl.BlockSpec((tm, tn), lambda j, i: (i, j))] * n_out
                 + [pl.BlockSpec((ms, tn), lambda j, i: (0, j))] * n_out)
    out_shape = ([jax.ShapeDtypeStruct((mp, n_cols), dt) for dt in out_dtypes]
                 + [jax.ShapeDtypeStruct((ms, n_cols), dt) for dt in out_dtypes])
    cast_rows = min(512, k_total)
    assert k_total % cast_rows == 0
    outs = pl.pallas_call(
        functools.partial(_proj_kernel, k_sizes=k_sizes, n_w=len(weights),
                          ext_names=tuple(ext_names), n_out=n_out, epilogue=epilogue,
                          cast_rows=cast_rows),
        grid=(n_cols // tn, mp // tm),
        in_specs=in_specs,
        out_specs=out_specs,
        out_shape=out_shape,
        scratch_shapes=[pltpu.VMEM((k_total, tn), BF16) for _ in weights],
        compiler_params=_params("arbitrary", "arbitrary"),
        name=name,
    )(*args)
    return outs[:n_out], outs[n_out:]


def _epi_plain(accs, ext, is_sample):
    return (accs[0],)


def _epi_head_norm(accs, ext, is_sample):
    acc = accs[0]
    wn = ext["norm"][...]
    parts = []
    for c0 in range(0, acc.shape[1], HEAD_DIM):
        a = acc[:, c0:c0 + HEAD_DIM]
        y = a * lax.rsqrt(jnp.mean(a * a, axis=-1, keepdims=True) + EPS)
        parts.append(y * wn[:, c0:c0 + HEAD_DIM])
    return (jnp.concatenate(parts, axis=1),)


def _epi_log_forget(accs, ext, is_sample):
    return (_log_sigmoid(accs[0] + ext["bias"][...]),)


def _epi_gelu(accs, ext, is_sample):
    return (_gelu_tanh(accs[0]),)


def _epi_silu(accs, ext, is_sample):
    a = accs[0]
    return (a * jax.nn.sigmoid(a),)


def _epi_relu2(accs, ext, is_sample):
    r = jnp.maximum(accs[0], 0.0)
    return (r * r,)


def _epi_product(accs, ext, is_sample):
    return (accs[1] * accs[0],)


def _epi_log_decay(accs, ext, is_sample):
    return (_log_sigmoid(accs[0] + ext["bias"][...]) / GLA_TAU,)


def _epi_residual(accs, ext, is_sample):
    if is_sample:
        return (ext["res_s"][...] + ext["gate_s"][...] * accs[0],)
    return (ext["res_p"][...] + ext["gate_p"][...] * accs[0],)


def _down_kernel(xp_ref, xs_ref, w_ref, resp_ref, gp_ref, ress_ref, gs_ref,
                 op_ref, os_ref, accs_ref, *, nk):
    i = pl.program_id(1)
    k = pl.program_id(2)
    wb = w_ref[...].astype(BF16)

    @pl.when(k == 0)
    def _():
        op_ref[...] = jnp.zeros_like(op_ref)

    op_ref[...] += jnp.dot(xp_ref[...], wb, preferred_element_type=F32)

    @pl.when(k == nk - 1)
    def _():
        op_ref[...] = resp_ref[...] + gp_ref[...] * op_ref[...]

    @pl.when(i == 0)
    def _():
        ps = jnp.dot(xs_ref[...], wb, preferred_element_type=F32)

        @pl.when(k == 0)
        def _():
            accs_ref[...] = ps

        @pl.when(k > 0)
        def _():
            accs_ref[...] += ps

        @pl.when(k == nk - 1)
        def _():
            os_ref[...] = ress_ref[...] + gs_ref[...] * accs_ref[...]


def _down(xp, xs, w, layer, res_p, gate_p, res_s, gate_s, rows_per_group, *,
          tm=1024, tn=1024, tk=2048):
    mp, kdim = xp.shape
    ms = xs.shape[0]
    n = w.shape[2]
    tm, tn, tk = min(tm, rows_per_group, mp), min(tn, n), min(tk, kdim)
    nk = kdim // tk
    return pl.pallas_call(
        functools.partial(_down_kernel, nk=nk),
        grid=(n // tn, mp // tm, nk),
        in_specs=[pl.BlockSpec((tm, tk), lambda j, i, k: (i, k)),
                  pl.BlockSpec((ms, tk), lambda j, i, k: (0, k)),
                  pl.BlockSpec((None, tk, tn), lambda j, i, k: (layer, k, j)),
                  pl.BlockSpec((tm, tn), lambda j, i, k: (i, j)),
                  pl.BlockSpec((None, 1, tn),
                               lambda j, i, k: ((i * tm) // rows_per_group, 0, j)),
                  pl.BlockSpec((ms, tn), lambda j, i, k: (0, j)),
                  pl.BlockSpec((ms, tn), lambda j, i, k: (0, j))],
        out_specs=[pl.BlockSpec((tm, tn), lambda j, i, k: (i, j)),
                   pl.BlockSpec((ms, tn), lambda j, i, k: (0, j))],
        out_shape=[jax.ShapeDtypeStruct((mp, n), F32),
                   jax.ShapeDtypeStruct((ms, n), F32)],
        scratch_shapes=[pltpu.VMEM((ms, tn), F32)],
        compiler_params=_params("arbitrary", "arbitrary", "arbitrary"),
        name="mlp_down",
    )(xp, xs, w, res_p, gate_p, res_s, gate_s)


def _to_heads_kernel(x_ref, o_ref):
    for h in range(o_ref.shape[1]):
        o_ref[:, h, :] = x_ref[:, h * HEAD_DIM:(h + 1) * HEAD_DIM]


def _to_heads(x, heads, tr=256):
    m, w = x.shape
    tr = min(tr, m)
    return pl.pallas_call(
        _to_heads_kernel,
        grid=(m // tr,),
        in_specs=[pl.BlockSpec((tr, w), lambda i: (i, 0))],
        out_specs=pl.BlockSpec((tr, heads, HEAD_DIM), lambda i: (i, 0, 0)),
        out_shape=jax.ShapeDtypeStruct((m, heads, HEAD_DIM), x.dtype),
        compiler_params=_params("arbitrary"),
        name="to_heads",
    )(x)


def _cumsum_kernel(x_ref, o_ref, carry_ref):
    @pl.when(pl.program_id(1) == 0)
    def _():
        carry_ref[...] = jnp.zeros_like(carry_ref)

    x = x_ref[...]
    t, w = x.shape
    r = lax.broadcasted_iota(jnp.int32, (t, t), 0)
    c = lax.broadcasted_iota(jnp.int32, (t, t), 1)
    tri = jnp.where(c <= r, 1.0, 0.0).astype(BF16)
    y = jnp.dot(tri, jnp.concatenate(_split3(x), axis=1), preferred_element_type=F32)
    out = y[:, :w] + y[:, w:2 * w] + y[:, 2 * w:] + carry_ref[...]
    o_ref[...] = out
    carry_ref[...] = out[t - 1:t, :]


def _cumsum_rows(x, n_groups, tc=256):
    rows, w = x.shape
    per = rows // n_groups
    tc = min(tc, per)
    nt = per // tc
    return pl.pallas_call(
        _cumsum_kernel,
        grid=(n_groups, nt),
        in_specs=[pl.BlockSpec((tc, w), lambda b, t: (b * nt + t, 0))],
        out_specs=pl.BlockSpec((tc, w), lambda b, t: (b * nt + t, 0)),
        out_shape=jax.ShapeDtypeStruct((rows, w), F32),
        scratch_shapes=[pltpu.VMEM((1, w), F32)],
        compiler_params=_params("arbitrary", "arbitrary"),
        name="cumsum_logf",
    )(x)


def _bias_lanes(col, own_first):
    hi, mid, lo = _split3(col)
    lane = lax.broadcasted_iota(jnp.int32, (col.shape[0], LANES), 1)
    own, other = (0, 3) if own_first else (3, 0)
    x = jnp.where(lane == own, hi.astype(F32),
                  jnp.where(lane == own + 1, mid.astype(F32),
                            jnp.where(lane == own + 2, lo.astype(F32), 0.0)))
    x = jnp.where((lane >= other) & (lane < other + 3), 1.0, x)
    return x.astype(BF16)


def _fox_prompt_kernel(q_ref, k_ref, v_ref, fq_ref, fk_ref, o_ref, kaug, vbf, *, tq, scale):
    h = pl.program_id(1)
    qi = pl.program_id(2)

    def head_column(f):
        lane = lax.broadcasted_iota(jnp.int32, f.shape, 1)
        return jnp.sum(jnp.where(lane == h, f, 0.0), axis=1, keepdims=True)

    @pl.when(qi == 0)
    def _():
        kaug[:, :HEAD_DIM] = k_ref[...].astype(BF16)
        kaug[:, HEAD_DIM:] = _bias_lanes(head_column(fk_ref[...]) * (-1.0 / scale), False)
        vbf[...] = v_ref[...].astype(BF16)

    q = jnp.concatenate(
        [q_ref[...], _bias_lanes(head_column(fq_ref[...]) * (1.0 / scale), True)], axis=1)
    c2 = scale * 1.4426950408889634

    def step(j, carry, masked):
        m, l, acc = carry
        start = pl.multiple_of(j * tq, tq)
        s = lax.dot_general(q, kaug[pl.ds(start, tq), :], (((1,), (1,)), ((), ())),
                            preferred_element_type=F32) * c2
        if masked:
            r = lax.broadcasted_iota(jnp.int32, (tq, tq), 0)
            c = lax.broadcasted_iota(jnp.int32, (tq, tq), 1)
            s = jnp.where(c <= r, s, NEG_INF)
        m_new = jnp.maximum(m, jnp.max(s, axis=1, keepdims=True))
        alpha = jnp.exp2(m - m_new)
        p = jnp.exp2(s - m_new)
        l = alpha * l + jnp.sum(p, axis=1, keepdims=True)
        acc = alpha * acc + jnp.dot(p.astype(BF16), vbf[pl.ds(start, tq), :],
                                    preferred_element_type=F32)
        return m_new, l, acc

    init = (jnp.full((tq, 1), NEG_INF, F32), jnp.zeros((tq, 1), F32),
            jnp.zeros((tq, HEAD_DIM), F32))
    carry = lax.fori_loop(0, qi, functools.partial(step, masked=False), init)
    m, l, acc = step(qi, carry, True)
    o_ref[...] = (acc / l).astype(o_ref.dtype)


def _fox_prompt(q, k, v, f_col, batch, seq, heads, tq=512):
    tq = min(tq, seq)
    nq = seq // tq
    return pl.pallas_call(
        functools.partial(_fox_prompt_kernel, tq=tq, scale=HEAD_DIM ** -0.5),
        grid=(batch, heads, nq),
        in_specs=[pl.BlockSpec((tq, HEAD_DIM), lambda b, h, i: (b * nq + i, h)),
                  pl.BlockSpec((seq, HEAD_DIM), lambda b, h, i: (b, h)),
                  pl.BlockSpec((seq, HEAD_DIM), lambda b, h, i: (b, h)),
                  pl.BlockSpec((tq, LANES), lambda b, h, i: (b * nq + i, 0)),
                  pl.BlockSpec((seq, LANES), lambda b, h, i: (b, 0))],
        out_specs=pl.BlockSpec((tq, HEAD_DIM), lambda b, h, i: (b * nq + i, h)),
        out_shape=jax.ShapeDtypeStruct(q.shape, BF16),
        scratch_shapes=[pltpu.VMEM((seq, 2 * HEAD_DIM), BF16), pltpu.VMEM((seq, HEAD_DIM), BF16)],
        compiler_params=_params("arbitrary", "arbitrary", "arbitrary"),
        name="fox_prompt",
    )(q, k, v, f_col, f_col)


def _fox_sample_kernel(pt_ref, q_ref, *refs, n_steps, pps, heads, scale):
    del pt_ref
    ck_refs, cv_refs, lf_refs = refs[:pps], refs[pps:2 * pps], refs[2 * pps:3 * pps]
    kn_ref, vn_ref, fn_ref, fs_ref, o_ref, m_ref, l_ref, acc_ref, tail_ref = refs[3 * pps:]
    p = pl.program_id(1)

    @pl.when(p == 0)
    def _():
        m_ref[...] = jnp.full(m_ref.shape, NEG_INF, F32)
        l_ref[...] = jnp.zeros_like(l_ref)
        acc_ref[...] = jnp.zeros_like(acc_ref)
        tail_ref[...] = jnp.zeros_like(tail_ref)

    q = q_ref[...]
    rows = q.shape[0]
    fn = fn_ref[...]
    row = lax.broadcasted_iota(jnp.int32, (rows, LANES), 0)
    lane = lax.broadcasted_iota(jnp.int32, (rows, LANES), 1)
    own_head = (lane & (heads - 1)) == (row & (heads - 1))

    def attend(keys, vals, lane_bias, visible):
        s = lax.dot_general(q, keys, (((1,), (1,)), ((), ())), preferred_element_type=F32)
        parts = []
        for g in range(keys.shape[0] // LANES):
            sg = s[:, g * LANES:(g + 1) * LANES] * scale + (fn + lane_bias[g:g + 1, :])
            parts.append(jnp.where(visible, sg, NEG_INF))
        s = jnp.concatenate(parts, axis=1)
        m_old = m_ref[...]
        m_new = jnp.maximum(m_old, jnp.max(s, axis=1, keepdims=True))
        alpha = jnp.exp(m_old - m_new)
        pr = jnp.exp(s - m_new)
        l_ref[...] = alpha * l_ref[...] + jnp.sum(pr, axis=1, keepdims=True)
        acc_ref[...] = alpha * acc_ref[...] + jnp.dot(pr.astype(BF16), vals,
                                                      preferred_element_type=F32)
        m_ref[...] = m_new

    a = lax.broadcasted_iota(jnp.int32, (LANES, LANES), 0)
    b = lax.broadcasted_iota(jnp.int32, (LANES, LANES), 1)
    same_head = (a & (heads - 1)) == (b & (heads - 1))
    later_in_row = jnp.where(same_head & (a > b), 1.0, 0.0).astype(BF16)
    whole_row = jnp.where(same_head, 1.0, 0.0).astype(BF16)
    sums = jnp.concatenate([later_in_row, whole_row], axis=1)
    n_rows = lf_refs[0].shape[0]
    ra = lax.broadcasted_iota(jnp.int32, (n_rows, n_rows), 0)
    rb = lax.broadcasted_iota(jnp.int32, (n_rows, n_rows), 1)
    later_rows = jnp.where(rb > ra, 1.0, 0.0).astype(BF16)

    tail = tail_ref[...]
    biases = []
    for c in range(pps):
        lf = lf_refs[c][...]
        y = jnp.dot(jnp.concatenate(_split3(lf), axis=0), sums, preferred_element_type=F32)
        y = y[:n_rows] + y[n_rows:2 * n_rows] + y[2 * n_rows:]
        within, row_tot = y[:, :LANES], y[:, LANES:]
        z = jnp.dot(later_rows, jnp.concatenate(_split3(row_tot), axis=1),
                    preferred_element_type=F32)
        biases.append(within + z[:, :LANES] + z[:, LANES:2 * LANES] + z[:, 2 * LANES:] + tail)
        tail = tail + jnp.sum(row_tot, axis=0, keepdims=True)
    tail_ref[...] = tail

    attend(jnp.concatenate([r[...].astype(BF16) for r in ck_refs], axis=0),
           jnp.concatenate([r[...].astype(BF16) for r in cv_refs], axis=0),
           jnp.concatenate(biases, axis=0), own_head)

    @pl.when(p == n_steps - 1)
    def _():
        attend(kn_ref[...], vn_ref[...], -fs_ref[...],
               own_head & (lane - (lane & (heads - 1)) <= row - (row & (heads - 1))))
        o_ref[...] = acc_ref[...] / l_ref[...]


def _fox_sample(page_table, q_rows, cache_k, cache_v, cache_lf, k_new, v_new, fn_col, fs_lane,
                heads):
    nb, n_pages = page_table.shape
    rows, hd = q_rows.shape[1], q_rows.shape[2]
    lf_rows = cache_lf.shape[1]
    page_rows = lf_rows * LANES
    assert heads & (heads - 1) == 0 and LANES % heads == 0
    pps = math.gcd(n_pages, FOX_PAGES_PER_STEP)
    n_steps = n_pages // pps
    per_b = lambda b, p, pt: (b, 0, 0)

    def page_of(c):
        return lambda b, p, pt: pt[b, n_pages - 1 - (p * pps + c)]

    kv_specs = [pl.BlockSpec((page_rows, hd), lambda b, p, pt, f=page_of(c): (f(b, p, pt), 0))
                for c in range(pps)]
    lf_specs = [pl.BlockSpec((None, lf_rows, LANES),
                             lambda b, p, pt, f=page_of(c): (f(b, p, pt), 0, 0))
                for c in range(pps)]
    grid_spec = pltpu.PrefetchScalarGridSpec(
        num_scalar_prefetch=1,
        grid=(nb, n_steps),
        in_specs=([pl.BlockSpec((None, rows, hd), per_b)] + kv_specs + kv_specs + lf_specs
                  + [pl.BlockSpec((None, LANES, hd), per_b),
                     pl.BlockSpec((None, LANES, hd), per_b),
                     pl.BlockSpec((None, rows, 1), per_b),
                     pl.BlockSpec((None, 1, LANES), per_b)]),
        out_specs=pl.BlockSpec((None, rows, hd), per_b),
        scratch_shapes=[pltpu.VMEM((rows, 1), F32), pltpu.VMEM((rows, 1), F32),
                        pltpu.VMEM((rows, hd), F32), pltpu.VMEM((1, LANES), F32)],
    )
    return pl.pallas_call(
        functools.partial(_fox_sample_kernel, n_steps=n_steps, pps=pps, heads=heads,
                          scale=HEAD_DIM ** -0.5),
        grid_spec=grid_spec,
        out_shape=jax.ShapeDtypeStruct((nb, rows, hd), F32),
        compiler_params=_params("arbitrary", "arbitrary"),
        name="fox_sample",
    )(page_table, q_rows, *([cache_k] * pps), *([cache_v] * pps), *([cache_lf] * pps),
      k_new, v_new, fn_col, fs_lane)


def _sgu_kernel(u_ref, vg_ref, nw_ref, w_ref, bt_ref, *out_refs, groups):
    o_ref = out_refs[0]
    g = vg_ref[...].astype(F32)
    v = g * lax.rsqrt(jnp.mean(g * g, axis=-1, keepdims=True) + EPS) * nw_ref[...]
    if len(out_refs) > 1:
        out_refs[1][...] = v
    vb = v.astype(BF16)
    rows = v.shape[0]
    cw = v.shape[1] // groups
    r = lax.broadcasted_iota(jnp.int32, (rows, rows), 0)
    c = lax.broadcasted_iota(jnp.int32, (rows, rows), 1)
    bt = bt_ref[...]
    for gi in range(groups):
        wm = jnp.where(c <= r, w_ref[gi], 0.0).astype(BF16)
        z = jnp.dot(wm, vb[:, gi * cw:(gi + 1) * cw], preferred_element_type=F32)
        z = z + bt[:, gi:gi + 1]
        u = u_ref[:, gi * cw:(gi + 1) * cw].astype(F32)
        o_ref[:, gi * cw:(gi + 1) * cw] = (u * z).astype(o_ref.dtype)


def _sgu(uv, norm_w, w_pos, bias_t, rows, emit_v):
    m, two_w = uv.shape
    width = two_w // 2
    groups = w_pos.shape[0]
    out_shape = [jax.ShapeDtypeStruct((m, width), BF16)]
    out_specs = [pl.BlockSpec((rows, width), lambda i: (i, 0))]
    if emit_v:
        out_shape.append(jax.ShapeDtypeStruct((m, width), F32))
        out_specs.append(pl.BlockSpec((rows, width), lambda i: (i, 0)))
    return pl.pallas_call(
        functools.partial(_sgu_kernel, groups=groups),
        grid=(m // rows,),
        in_specs=[pl.BlockSpec((rows, width), lambda i: (i, 0)),
                  pl.BlockSpec((rows, width), lambda i: (i, 1)),
                  pl.BlockSpec((1, width), lambda i: (0, 0)),
                  pl.BlockSpec((groups, rows, rows), lambda i: (0, 0, 0)),
                  pl.BlockSpec((rows, groups), lambda i: (0, 0))],
        out_specs=out_specs,
        out_shape=out_shape,
        compiler_params=_params("arbitrary"),
        name="sgu",
    )(uv, uv, norm_w, w_pos, bias_t)


def _conv_prompt_kernel(z_ref, gb_ref, w_ref, o_ref, zbuf, *, tr):
    @pl.when(pl.program_id(2) == 0)
    def _():
        zbuf[0:8, :] = jnp.zeros((8, zbuf.shape[1]), F32)

    zbuf[8:8 + tr, :] = z_ref[...]
    w = w_ref[...]
    y = (w[0:1] * zbuf[6:6 + tr, :] + w[1:2] * zbuf[7:7 + tr, :] + w[2:3] * zbuf[8:8 + tr, :])
    o_ref[...] = (gb_ref[...].astype(F32) * y).astype(o_ref.dtype)
    zbuf[0:8, :] = zbuf[tr:tr + 8, :]


def _conv_prompt(z, gb, conv_w, batch, seq, tr=256, tc=512):
    m, width = z.shape
    tr, tc = min(tr, seq), min(tc, width)
    nt = seq // tr
    return pl.pallas_call(
        functools.partial(_conv_prompt_kernel, tr=tr),
        grid=(width // tc, batch, nt),
        in_specs=[pl.BlockSpec((tr, tc), lambda c, b, t: (b * nt + t, c)),
                  pl.BlockSpec((tr, tc), lambda c, b, t: (b * nt + t, c)),
                  pl.BlockSpec((conv_w.shape[0], tc), lambda c, b, t: (0, c))],
        out_specs=pl.BlockSpec((tr, tc), lambda c, b, t: (b * nt + t, c)),
        out_shape=jax.ShapeDtypeStruct((m, width), BF16),
        scratch_shapes=[pltpu.VMEM((tr + 8, tc), F32)],
        compiler_params=_params("arbitrary", "arbitrary", "arbitrary"),
        name="conv_prompt",
    )(z, gb, conv_w)


def _conv_sample_kernel(z0_ref, z1_ref, z2_ref, gb_ref, w_ref, o_ref):
    w = w_ref[...]
    y = w[0:1] * z0_ref[...] + w[1:2] * z1_ref[...] + w[2:3] * z2_ref[...]
    o_ref[...] = (gb_ref[...].astype(F32) * y).astype(o_ref.dtype)


def _conv_sample(z0, z1, z2, gb, conv_w):
    return pl.pallas_call(
        _conv_sample_kernel,
        out_shape=jax.ShapeDtypeStruct(z0.shape, BF16),
        name="conv_sample",
    )(z0, z1, z2, gb, conv_w)


def _gla_kernel(q_ref, k_ref, v_ref, la_ref, gate_ref, nw_ref, s0_ref, o_ref, st_ref, st, *, nc):
    ci = pl.program_id(2)

    @pl.when(ci == 0)
    def _():
        st[...] = s0_ref[...]

    la = la_ref[...]
    c, dk = la.shape
    r = lax.broadcasted_iota(jnp.int32, (c, c), 0)
    cc = lax.broadcasted_iota(jnp.int32, (c, c), 1)
    tril = cc <= r
    y = jnp.dot(jnp.where(tril, 1.0, 0.0).astype(BF16),
                jnp.concatenate(_split3(la), axis=1), preferred_element_type=F32)
    bc = y[:, :dk] + y[:, dk:2 * dk] + y[:, 2 * dk:]
    q = q_ref[...].astype(F32) * dk ** -0.5
    k = k_ref[...].astype(F32)
    v = v_ref[...]
    qt = (q * jnp.exp(bc)).astype(BF16)
    mid = bc[c // 2 - 1:c // 2, :]
    qa = (q * jnp.exp(bc - mid)).astype(BF16)
    ka = (k * jnp.exp(mid - bc)).astype(BF16)
    att = lax.dot_general(qa, ka, (((1,), (1,)), ((), ())), preferred_element_type=F32)
    att = jnp.where(tril, att, 0.0)
    s_t = st[...]
    o = (lax.dot_general(qt, s_t.astype(BF16), (((1,), (1,)), ((), ())),
                         preferred_element_type=F32)
         + jnp.dot(att.astype(BF16), v, preferred_element_type=F32))
    bl = bc[c - 1:c, :]
    kd = (k * jnp.exp(bl - bc)).astype(BF16)
    st[...] = s_t * jnp.exp(bl) + lax.dot_general(v, kd, (((0,), (0,)), ((), ())),
                                                   preferred_element_type=F32)
    on = o * lax.rsqrt(jnp.mean(o * o, axis=-1, keepdims=True) + EPS) * nw_ref[...]
    o_ref[...] = (on * gate_ref[...].astype(F32)).astype(o_ref.dtype)

    @pl.when(ci == nc - 1)
    def _():
        st_ref[...] = st[...]


def _gla(qkv, log_a, gate, norm_w, s0_t, batch, seq, heads, chunk):
    m = qkv.shape[0]
    dk = log_a.shape[1] // heads
    dv = gate.shape[1] // heads
    assert (2 * heads * dk) % dv == 0
    v0 = (2 * heads * dk) // dv
    chunk = min(chunk, seq)
    nc = seq // chunk
    row = lambda b, h, c: b * nc + c
    return pl.pallas_call(
        functools.partial(_gla_kernel, nc=nc),
        grid=(batch, heads, nc),
        in_specs=[pl.BlockSpec((chunk, dk), lambda b, h, c: (row(b, h, c), h)),
                  pl.BlockSpec((chunk, dk), lambda b, h, c: (row(b, h, c), heads + h)),
                  pl.BlockSpec((chunk, dv), lambda b, h, c: (row(b, h, c), v0 + h)),
                  pl.BlockSpec((chunk, dk), lambda b, h, c: (row(b, h, c), h)),
                  pl.BlockSpec((chunk, dv), lambda b, h, c: (row(b, h, c), h)),
                  pl.BlockSpec((1, dv), lambda b, h, c: (0, h)),
                  pl.BlockSpec((None, None, dv, dk), lambda b, h, c: (b, h, 0, 0))],
        out_specs=[pl.BlockSpec((chunk, dv), lambda b, h, c: (row(b, h, c), h)),
                   pl.BlockSpec((None, None, dv, dk), lambda b, h, c: (b, h, 0, 0))],
        out_shape=[jax.ShapeDtypeStruct((m, heads * dv), BF16),
                   jax.ShapeDtypeStruct((batch, heads, dv, dk), F32)],
        scratch_shapes=[pltpu.VMEM((dv, dk), F32)],
        compiler_params=_params("arbitrary", "arbitrary", "arbitrary"),
        name="gla",
    )(qkv, qkv, qkv, log_a, gate, norm_w, s0_t)


def _pad_cols(w, n):
    return jnp.pad(w, ((0, 0), (0, n - w.shape[1])))


def kernel(x_prompt, x_sample, cache_k, cache_v, cache_logf, state_conv, state_gla, page_table,
           c_prompt, c_sample, w_in0, f_bias, q_norm, k_norm, sgu_norm, sgu_w, sgu_b, w_out0,
           w_in1, conv_w, gla_a_w2, gla_a_b, gla_norm, w_out1, ada_w, ada_b, mlp_w1, mlp_w2):
    nbp, seq, d = x_prompt.shape
    nbs, dseq, _ = x_sample.shape
    mp, ms = nbp * seq, nbs * dseq
    fox_w = d // 2
    heads = fox_w // HEAD_DIM
    sgu_width = d // 2
    groups = sgu_w.shape[1]
    conv_width = state_conv.shape[-1]
    gla_heads, gla_dk, gla_dv = state_gla.shape[2], state_gla.shape[3], state_gla.shape[4]
    gla_key, gla_val = gla_heads * gla_dk, gla_heads * gla_dv
    gla_rank = gla_a_w2.shape[1]
    n_pool, page = cache_k.shape[1], cache_k.shape[2]
    n_pages = page_table.shape[1]

    r_c = nbp + nbs
    r_pad = -(-r_c // 16) * 16
    c_rows = jnp.pad(jnp.concatenate([c_prompt, c_sample], axis=0),
                     ((0, r_pad - r_c), (0, 0))).astype(BF16)
    mod = _ada(c_rows, ada_w, ada_b)

    def mods(layer):
        parts = jnp.split(mod[layer], 6, axis=-1)
        pp = [p[:nbp].reshape(nbp, 1, d) for p in parts]
        ps = [jnp.repeat(p[nbp:r_c], dseq, axis=0) for p in parts]
        return pp, ps

    xp = x_prompt.reshape(mp, d)
    xs = x_sample.reshape(ms, d)

    def modulate_both(xp, xs, shp, scp, shs, scs):
        hp = _modulate(xp, shp, scp, seq)
        hs = _modulate(xs, shs.reshape(1, ms, d), scs.reshape(1, ms, d), ms, tr=ms)
        return hp, hs

    def mlp(xp, xs, layer, shp, scp, gp, shs, scs, gs):
        hp, hs = modulate_both(xp, xs, shp, scp, shs, scs)
        (ap,), (as_,) = _proj([hp], [hs], [(mlp_w1, layer, 0)], mlp_w1.shape[2], _epi_relu2,
                              [BF16], name="mlp_up")
        return _down(ap, as_, mlp_w2, layer, xp, gp, xs, gs, seq)

    def residual_proj(lhs_p, lhs_s, w, xp, xs, gp, gs, name):
        (yp,), (ys,) = _proj(lhs_p, lhs_s, [(w, 0, 0)], d, _epi_residual, [F32],
                             ext=[("res_p", "prow", xp), ("gate_p", "pgrp", gp),
                                  ("res_s", "srow", xs), ("gate_s", "srow", gs)],
                             rows_per_group=seq, name=name)
        return yp, ys

    (sh1p, sc1p, g1p, sh2p, sc2p, g2p), (sh1s, sc1s, g1s, sh2s, sc2s, g2s) = mods(0)
    hp, hs = modulate_both(xp, xs, sh1p, sc1p, sh1s, sc1s)
    w0 = w_in0[0]
    qn = jnp.tile(q_norm[0], heads).reshape(1, fox_w)
    kn = jnp.tile(k_norm[0], heads).reshape(1, fox_w)
    (q_p,), (q_s,) = _proj([hp], [hs], [(w_in0, 0, 0)], fox_w, _epi_head_norm, [BF16],
                           ext=[("norm", "col", qn)], name="in0_q")
    (k_p,), (k_s,) = _proj([hp], [hs], [(w_in0, 0, fox_w)], fox_w, _epi_head_norm, [F32],
                           ext=[("norm", "col", kn)], name="in0_k")
    (v_p,), (v_s,) = _proj([hp], [hs], [(w_in0, 0, 2 * fox_w)], fox_w, _epi_plain, [F32],
                           name="in0_v")
    w_fl = _pad_cols(w0[:, 3 * fox_w:3 * fox_w + heads], LANES)
    fb = jnp.pad(f_bias[0], (0, LANES - heads)).reshape(1, LANES)
    (lf_p,), (lf_s,) = _proj([hp], [hs], [(w_fl, 0, 0)], LANES, _epi_log_forget, [F32],
                             ext=[("bias", "col", fb)], name="in0_logf")
    w_uv = w0[:, 3 * fox_w + heads:]
    (uv_p,), (uv_s,) = _proj([hp], [hs], [(w_uv, 0, 0)], 2 * sgu_width, _epi_gelu, [BF16],
                             name="in0_uv")

    f_col = _cumsum_rows(lf_p, nbp)
    oa_p = _fox_prompt(q_p, k_p, v_p, f_col, nbp, seq, heads)

    lf_s16 = lf_s[:, :heads].reshape(nbs, dseq, heads)
    tpad = 8
    lf_s_pad = jnp.pad(lf_s.reshape(nbs, dseq, LANES), ((0, 0), (0, tpad - dseq), (0, 0)))
    fn = _cumsum_rows(lf_s_pad.reshape(nbs * tpad, LANES), nbs).reshape(
        nbs, tpad, LANES)[:, :dseq, :heads]
    rows_s = dseq * heads
    assert rows_s <= LANES

    def new_rows(a):
        a = a.reshape(nbs, rows_s, HEAD_DIM)
        return jnp.pad(a, ((0, 0), (0, LANES - rows_s), (0, 0))).astype(BF16)

    fs_lane = jnp.pad(fn.reshape(nbs, 1, rows_s), ((0, 0), (0, 0), (0, LANES - rows_s)))
    oa_rows = _fox_sample(page_table, q_s.reshape(nbs, rows_s, HEAD_DIM),
                          cache_k[0].reshape(n_pool * page * heads, HEAD_DIM),
                          cache_v[0].reshape(n_pool * page * heads, HEAD_DIM),
                          cache_logf[0].reshape(n_pool, page * heads // LANES, LANES),
                          new_rows(k_s), new_rows(v_s), fn.reshape(nbs, rows_s, 1), fs_lane, heads)
    oa_s = oa_rows.reshape(ms, fox_w).astype(BF16)

    sn = sgu_norm[0].reshape(1, sgu_width)
    rows_p = min(seq, SGU_CHUNK)
    (ob_p,) = _sgu(uv_p, sn, sgu_w[0][:, :rows_p, :rows_p], sgu_b[0][:, :rows_p].T, rows_p, False)
    rows_g = min(dseq, SGU_CHUNK)
    w_small = sgu_w[0][:, :rows_g, :rows_g]
    w_big = jnp.einsum("ab,gts->gatbs", jnp.eye(nbs, dtype=F32), w_small).reshape(groups, ms, ms)
    b_big = jnp.tile(sgu_b[0][:, :rows_g].T, (nbs, 1))
    ob_s, sgu_v = _sgu(uv_s, sn, w_big, b_big, ms, True)

    xp, xs = residual_proj([oa_p, ob_p], [oa_s, ob_s], w_out0, xp, xs, g1p, g1s, "out0")
    xp, xs = mlp(xp, xs, 0, sh2p, sc2p, g2p, sh2s, sc2s, g2s)

    (sh1p, sc1p, g1p, sh2p, sc2p, g2p), (sh1s, sc1s, g1s, sh2s, sc2s, g2s) = mods(1)
    hp, hs = modulate_both(xp, xs, sh1p, sc1p, sh1s, sc1s)
    w1 = w_in1[0]
    cw = conv_width
    (z_p,), (z_s,) = _proj([hp], [hs], [(w_in1, 0, 0), (w_in1, 0, cw)], cw, _epi_product, [F32], tn=256,
                           name="in1_conv_z")
    (gb_p,), (gb_s,) = _proj([hp], [hs], [(w_in1, 0, 2 * cw)], cw, _epi_plain, [BF16], name="in1_conv_b")
    (qkv_p,), (qkv_s,) = _proj([hp], [hs], [(w_in1, 0, 3 * cw)], 2 * gla_key + gla_val, _epi_plain,
                               [BF16], name="in1_qkv")
    c_ga = 3 * cw + 2 * gla_key + gla_val
    w_ga = _pad_cols(w1[:, c_ga:c_ga + gla_rank], LANES)
    (ga_p,), (ga_s,) = _proj([hp], [hs], [(w_ga, 0, 0)], LANES, _epi_plain, [BF16], name="in1_ga")
    w_a2 = jnp.pad(gla_a_w2[0], ((0, LANES - gla_rank), (0, 0)))
    (la_p,), (la_s,) = _proj([ga_p], [ga_s], [(w_a2, 0, 0)], gla_key, _epi_log_decay, [F32],
                             ext=[("bias", "col", gla_a_b[0].reshape(1, gla_key))], name="in1_log_a")
    w_gg = w1[:, c_ga + gla_rank:]
    (gg_p,), (gg_s,) = _proj([hp], [hs], [(w_gg, 0, 0)], gla_val, _epi_silu, [BF16], name="in1_gate")

    cwt = conv_w[0]
    oc_p = _conv_prompt(z_p, gb_p, cwt, nbp, seq)
    conv_p = z_p.reshape(nbp, seq, cw)[:, seq - (cwt.shape[0] - 1):, :]
    zp_s = jnp.concatenate([state_conv[0], z_s.reshape(nbs, dseq, cw)], axis=1)
    shifted = [zp_s[:, i:i + dseq].reshape(ms, cw) for i in range(cwt.shape[0])]
    oc_s = _conv_sample(shifted[0], shifted[1], shifted[2], gb_s, cwt)
    conv_s = zp_s[:, dseq:, :]

    gn = gla_norm[0].reshape(1, gla_val)
    s0_p = jnp.zeros((nbp, gla_heads, gla_dv, gla_dk), F32)
    od_p, st_p = _gla(qkv_p, la_p, gg_p, gn, s0_p, nbp, seq, gla_heads, GLA_CHUNK_ROWS)
    pad_t = 16

    def pad_rows(a):
        return jnp.pad(a.reshape(nbs, dseq, -1), ((0, 0), (0, pad_t - dseq), (0, 0))).reshape(
            nbs * pad_t, -1)

    od_s_pad, st_s = _gla(pad_rows(qkv_s), pad_rows(la_s), pad_rows(gg_s), gn,
                          state_gla[0].swapaxes(-1, -2), nbs, pad_t, gla_heads, pad_t)
    od_s = od_s_pad.reshape(nbs, pad_t, gla_val)[:, :dseq].reshape(ms, gla_val)

    xp, xs = residual_proj([oc_p, od_p], [oc_s, od_s], w_out1, xp, xs, g1p, g1s, "out1")
    xp, xs = mlp(xp, xs, 1, sh2p, sc2p, g2p, sh2s, sc2s, g2s)

    y_prompt = xp.reshape(nbp, seq, d)
    y_sample = xs.reshape(nbs, dseq, d)
    return (y_prompt, y_sample,
            _to_heads(k_p, heads).reshape(1, nbp, seq, heads, HEAD_DIM),
            _to_heads(v_p, heads).reshape(1, nbp, seq, heads, HEAD_DIM),
            lf_p[:, :heads].reshape(1, nbp, seq, heads),
            k_s.reshape(1, nbs, dseq, heads, HEAD_DIM), v_s.reshape(1, nbs, dseq, heads, HEAD_DIM),
            lf_s16[None], sgu_v.reshape(1, nbs, dseq, sgu_width),
            conv_p[None], conv_s[None],
            st_p.swapaxes(-1, -2)[None], st_s.swapaxes(-1, -2)[None])
```

```python
import functools
import math

import jax
import jax.numpy as jnp
from jax import lax
from jax.experimental import pallas as pl
from jax.experimental.pallas import tpu as pltpu

F32 = jnp.float32
BF16 = jnp.bfloat16
EPS = 1e-6
NEG_INF = -1e30
HEAD_DIM = 128
LANES = 128
SGU_CHUNK = 128
GLA_TAU = 16.0
GLA_CHUNK_ROWS = 128
FOX_PAGES_PER_STEP = 4
VMEM_LIMIT_BYTES = 56 * 1024 * 1024


def _params(*semantics):
    return pltpu.CompilerParams(dimension_semantics=semantics,
                                vmem_limit_bytes=VMEM_LIMIT_BYTES)


def _split3(x):
    hi = x.astype(BF16)
    r = x - hi.astype(F32)
    mid = r.astype(BF16)
    lo = (r - mid.astype(F32)).astype(BF16)
    return hi, mid, lo


def _log_sigmoid(x):
    return jnp.minimum(x, 0.0) - jnp.log1p(jnp.exp(-jnp.abs(x)))


def _gelu_tanh(x):
    c = 0.7978845608028654
    return 0.5 * x * (1.0 + jnp.tanh(c * (x + 0.044715 * (x * x * x))))


def _ada_kernel(c_ref, w_ref, b_ref, o_ref):
    o_ref[...] = jnp.dot(c_ref[...], w_ref[...].astype(BF16),
                         preferred_element_type=F32) + b_ref[...]


def _ada(c_rows, ada_w, ada_b, tn=512):
    n_layers, d, n = ada_w.shape
    r = c_rows.shape[0]
    tn = min(tn, n)
    return pl.pallas_call(
        _ada_kernel,
        grid=(n_layers, n // tn),
        in_specs=[pl.BlockSpec((r, d), lambda l, j: (0, 0)),
                  pl.BlockSpec((None, d, tn), lambda l, j: (l, 0, j)),
                  pl.BlockSpec((None, 1, tn), lambda l, j: (l, 0, j))],
        out_specs=pl.BlockSpec((None, r, tn), lambda l, j: (l, 0, j)),
        out_shape=jax.ShapeDtypeStruct((n_layers, r, n), F32),
        compiler_params=_params("arbitrary", "arbitrary"),
        name="ada",
    )(c_rows, ada_w, ada_b.reshape(n_layers, 1, n))


def _modulate_kernel(x_ref, sh_ref, sc_ref, o_ref):
    x = x_ref[...]
    y = x * lax.rsqrt(jnp.mean(x * x, axis=-1, keepdims=True) + EPS)
    o_ref[...] = (y * (1.0 + sc_ref[...]) + sh_ref[...]).astype(o_ref.dtype)


def _modulate(x, sh, sc, rows_per_group, tr=256):
    rows, d = x.shape
    tr = min(tr, rows)
    r = sh.shape[1]
    grp = lambda i: ((i * tr) // rows_per_group, 0, 0)
    return pl.pallas_call(
        _modulate_kernel,
        grid=(rows // tr,),
        in_specs=[pl.BlockSpec((tr, d), lambda i: (i, 0)),
                  pl.BlockSpec((None, r, d), grp),
                  pl.BlockSpec((None, r, d), grp)],
        out_specs=pl.BlockSpec((tr, d), lambda i: (i, 0)),
        out_shape=jax.ShapeDtypeStruct((rows, d), BF16),
        compiler_params=_params("arbitrary"),
        name="modulate",
    )(x, sh, sc)


def _proj_kernel(*refs, k_sizes, n_w, ext_names, n_out, epilogue, cast_rows, w_rows, n_scratch):
    n_lhs = len(k_sizes)
    pos = 0
    xp = refs[pos:pos + n_lhs]; pos += n_lhs
    xs = refs[pos:pos + n_lhs]; pos += n_lhs
    w = refs[pos:pos + n_w]; pos += n_w
    ext = dict(zip(ext_names, refs[pos:pos + len(ext_names)])); pos += len(ext_names)
    out_p = refs[pos:pos + n_out]; pos += n_out
    out_s = refs[pos:pos + n_out]; pos += n_out
    wbf = refs[pos:pos + n_w]; pos += n_w
    ext["scratch"] = refs[pos:pos + n_scratch]

    def accumulate(lhs):
        accs = []
        for m in range(n_w):
            acc = None
            off = 0
            for a, ka in enumerate(k_sizes):
                part = jnp.dot(lhs[a][...], wbf[m][off:off + ka, :],
                               preferred_element_type=F32)
                acc = part if acc is None else acc + part
                off += ka
            accs.append(acc)
        return accs

    @pl.when(pl.program_id(1) == 0)
    def _():
        k_total = sum(k_sizes)
        for m in range(n_w):
            for r0 in range(0, k_total, cast_rows):
                if w_rows:
                    wbf[m][r0:r0 + cast_rows, :] = w[m][:, r0:r0 + cast_rows].T.astype(BF16)
                else:
                    wbf[m][r0:r0 + cast_rows, :] = w[m][r0:r0 + cast_rows, :].astype(BF16)
        for o_ref, val in zip(out_s, epilogue(accumulate(xs), ext, True)):
            o_ref[...] = val.astype(o_ref.dtype)

    for o_ref, val in zip(out_p, epilogue(accumulate(xp), ext, False)):
        o_ref[...] = val.astype(o_ref.dtype)


def _proj(xp, xs, weights, n_cols, epilogue, out_dtypes, ext=(), *, tm=1024, tn=512,
          rows_per_group=None, w_rows=False, tail_rows=(), scratch=None, name="proj"):
    mp, ms = xp[0].shape[0], xs[0].shape[0]
    k_sizes = tuple(int(a.shape[1]) for a in xp)
    k_total = sum(k_sizes)
    tm = min(tm, mp, rows_per_group or mp)
    tn = min(tn, n_cols)
    assert mp % tm == 0 and n_cols % tn == 0
    for warr, _, c0 in weights:
        assert warr.ndim == 3
        if w_rows:
            assert warr.shape[2] == k_total and c0 % 8 == 0 and len(k_sizes) == 1
        else:
            assert warr.shape[1] == k_total and c0 % tn == 0
    in_specs, args = [], []
    for a in xp:
        in_specs.append(pl.BlockSpec((tm, a.shape[1]), lambda j, i: (i, 0)))
        args.append(a)
    for a in xs:
        in_specs.append(pl.BlockSpec((ms, a.shape[1]), lambda j, i: (0, 0)))
        args.append(a)
    for warr, layer, c0 in weights:
        if w_rows:
            spec = pl.BlockSpec((None, pl.Element(tn), pl.Element(k_total)),
                                lambda j, i, c=c0, l=layer: (l, pl.multiple_of(c + j * tn, 8), 0))
        else:
            spec = pl.BlockSpec((None, k_total, tn),
                                lambda j, i, cb=c0 // tn, l=layer: (l, 0, j + cb))
        in_specs.append(spec)
        args.append(warr)
    ext_names = []
    for ename, kind, arr in ext:
        ext_names.append(ename)
        if kind == "col":
            spec = pl.BlockSpec((arr.shape[0], tn), lambda j, i: (0, j))
        elif kind == "prow":
            spec = pl.BlockSpec((tm, tn), lambda j, i: (i, j))
        elif kind == "pgrp":
            spec = pl.BlockSpec((None, 1, tn),
                                lambda j, i: ((i * tm) // rows_per_group, 0, j))
        elif kind == "srow":
            spec = pl.BlockSpec((ms, tn), lambda j, i: (0, j))
        else:
            raise ValueError(kind)
        in_specs.append(spec)
        args.append(arr)
    n_out = len(out_dtypes)
    rows_p = [(8 * (mp // tm), 8) if o in tail_rows else (mp, tm) for o in range(n_out)]
    out_specs = ([pl.BlockSpec((br, tn), lambda j, i: (i, j)) for _, br in rows_p]
                 + [pl.BlockSpec((ms, tn), lambda j, i: (0, j))] * n_out)
    out_shape = ([jax.ShapeDtypeStruct((r, n_cols), dt) for (r, _), dt in zip(rows_p, out_dtypes)]
                 + [jax.ShapeDtypeStruct((ms, n_cols), dt) for dt in out_dtypes])
    cast_rows = min(512, k_total)
    assert k_total % cast_rows == 0
    extra_scratch = scratch(tm, tn) if scratch else []
    outs = pl.pallas_call(
        functools.partial(_proj_kernel, k_sizes=k_sizes, n_w=len(weights),
                          ext_names=tuple(ext_names), n_out=n_out, epilogue=epilogue,
                          cast_rows=cast_rows, w_rows=w_rows, n_scratch=len(extra_scratch)),
        grid=(n_cols // tn, mp // tm),
        in_specs=in_specs,
        out_specs=out_specs,
        out_shape=out_shape,
        scratch_shapes=[pltpu.VMEM((k_total, tn), BF16) for _ in weights] + extra_scratch,
        compiler_params=_params("arbitrary", "arbitrary"),
        name=name,
    )(*args)
    return outs[:n_out], outs[n_out:]


def _epi_plain(accs, ext, is_sample):
    return (accs[0],)


def _epi_head_norm(accs, ext, is_sample):
    acc = accs[0]
    wn = ext["norm"][...]
    parts = []
    for c0 in range(0, acc.shape[1], HEAD_DIM):
        a = acc[:, c0:c0 + HEAD_DIM]
        y = a * lax.rsqrt(jnp.mean(a * a, axis=-1, keepdims=True) + EPS)
        parts.append(y * wn[:, c0:c0 + HEAD_DIM])
    return (jnp.concatenate(parts, axis=1),)


def _epi_log_forget(accs, ext, is_sample):
    return (_log_sigmoid(accs[0] + ext["bias"][...]),)


def _epi_gelu(accs, ext, is_sample):
    return (_gelu_tanh(accs[0]),)


def _epi_silu(accs, ext, is_sample):
    a = accs[0]
    return (a * jax.nn.sigmoid(a),)


def _epi_relu2(accs, ext, is_sample):
    r = jnp.maximum(accs[0], 0.0)
    return (r * r,)


def _make_epi_conv(seq):
    def epilogue(accs, ext, is_sample):
        z = accs[1] * accs[0]
        if is_sample:
            return accs[2], z
        (zbuf,) = ext["scratch"]
        tm = z.shape[0]

        @pl.when(pl.program_id(1) % (seq // tm) == 0)
        def _():
            zbuf[0:8, :] = jnp.zeros((8, zbuf.shape[1]), F32)

        zbuf[8:8 + tm, :] = z
        w = ext["taps"][...]
        y = w[0:1] * zbuf[6:6 + tm, :] + w[1:2] * zbuf[7:7 + tm, :] + w[2:3] * z
        tail = z[tm - 8:tm, :]
        zbuf[0:8, :] = tail
        return accs[2] * y, tail

    return epilogue


def _epi_log_decay(accs, ext, is_sample):
    return (_log_sigmoid(accs[0] + ext["bias"][...]) / GLA_TAU,)


def _epi_residual(accs, ext, is_sample):
    if is_sample:
        return (ext["res_s"][...] + ext["gate_s"][...] * accs[0],)
    return (ext["res_p"][...] + ext["gate_p"][...] * accs[0],)


def _down_kernel(xp_ref, xs_ref, w_ref, resp_ref, gp_ref, ress_ref, gs_ref,
                 op_ref, os_ref, accs_ref, *, nk):
    i = pl.program_id(1)
    k = pl.program_id(2)
    wb = w_ref[...].astype(BF16)

    @pl.when(k == 0)
    def _():
        op_ref[...] = jnp.zeros_like(op_ref)

    op_ref[...] += jnp.dot(xp_ref[...], wb, preferred_element_type=F32)

    @pl.when(k == nk - 1)
    def _():
        op_ref[...] = resp_ref[...] + gp_ref[...] * op_ref[...]

    @pl.when(i == 0)
    def _():
        ps = jnp.dot(xs_ref[...], wb, preferred_element_type=F32)

        @pl.when(k == 0)
        def _():
            accs_ref[...] = ps

        @pl.when(k > 0)
        def _():
            accs_ref[...] += ps

        @pl.when(k == nk - 1)
        def _():
            os_ref[...] = ress_ref[...] + gs_ref[...] * accs_ref[...]


def _down(xp, xs, w, layer, res_p, gate_p, res_s, gate_s, rows_per_group, *,
          tm=1024, tn=1024, tk=2048):
    mp, kdim = xp.shape
    ms = xs.shape[0]
    n = w.shape[2]
    tm, tn, tk = min(tm, rows_per_group, mp), min(tn, n), min(tk, kdim)
    nk = kdim // tk
    return pl.pallas_call(
        functools.partial(_down_kernel, nk=nk),
        grid=(n // tn, mp // tm, nk),
        in_specs=[pl.BlockSpec((tm, tk), lambda j, i, k: (i, k)),
                  pl.BlockSpec((ms, tk), lambda j, i, k: (0, k)),
                  pl.BlockSpec((None, tk, tn), lambda j, i, k: (layer, k, j)),
                  pl.BlockSpec((tm, tn), lambda j, i, k: (i, j)),
                  pl.BlockSpec((None, 1, tn),
                               lambda j, i, k: ((i * tm) // rows_per_group, 0, j)),
                  pl.BlockSpec((ms, tn), lambda j, i, k: (0, j)),
                  pl.BlockSpec((ms, tn), lambda j, i, k: (0, j))],
        out_specs=[pl.BlockSpec((tm, tn), lambda j, i, k: (i, j)),
                   pl.BlockSpec((ms, tn), lambda j, i, k: (0, j))],
        out_shape=[jax.ShapeDtypeStruct((mp, n), F32),
                   jax.ShapeDtypeStruct((ms, n), F32)],
        scratch_shapes=[pltpu.VMEM((ms, tn), F32)],
        compiler_params=_params("arbitrary", "arbitrary", "arbitrary"),
        name="mlp_down",
    )(xp, xs, w, res_p, gate_p, res_s, gate_s)


def _to_heads_kernel(x_ref, o_ref):
    for h in range(o_ref.shape[1]):
        o_ref[:, h, :] = x_ref[:, h * HEAD_DIM:(h + 1) * HEAD_DIM]


def _to_heads(x, heads, tr=256):
    m, w = x.shape
    tr = min(tr, m)
    return pl.pallas_call(
        _to_heads_kernel,
        grid=(m // tr,),
        in_specs=[pl.BlockSpec((tr, w), lambda i: (i, 0))],
        out_specs=pl.BlockSpec((tr, heads, HEAD_DIM), lambda i: (i, 0, 0)),
        out_shape=jax.ShapeDtypeStruct((m, heads, HEAD_DIM), x.dtype),
        compiler_params=_params("arbitrary"),
        name="to_heads",
    )(x)


def _cumsum_kernel(x_ref, o_ref, carry_ref):
    @pl.when(pl.program_id(1) == 0)
    def _():
        carry_ref[...] = jnp.zeros_like(carry_ref)

    x = x_ref[...]
    t, w = x.shape
    r = lax.broadcasted_iota(jnp.int32, (t, t), 0)
    c = lax.broadcasted_iota(jnp.int32, (t, t), 1)
    tri = jnp.where(c <= r, 1.0, 0.0).astype(BF16)
    y = jnp.dot(tri, jnp.concatenate(_split3(x), axis=1), preferred_element_type=F32)
    out = y[:, :w] + y[:, w:2 * w] + y[:, 2 * w:] + carry_ref[...]
    o_ref[...] = out
    carry_ref[...] = out[t - 1:t, :]


def _cumsum_rows(x, n_groups, tc=256):
    rows, w = x.shape
    per = rows // n_groups
    tc = min(tc, per)
    nt = per // tc
    return pl.pallas_call(
        _cumsum_kernel,
        grid=(n_groups, nt),
        in_specs=[pl.BlockSpec((tc, w), lambda b, t: (b * nt + t, 0))],
        out_specs=pl.BlockSpec((tc, w), lambda b, t: (b * nt + t, 0)),
        out_shape=jax.ShapeDtypeStruct((rows, w), F32),
        scratch_shapes=[pltpu.VMEM((1, w), F32)],
        compiler_params=_params("arbitrary", "arbitrary"),
        name="cumsum_logf",
    )(x)


def _bias_lanes(col, own_first):
    hi, mid, lo = _split3(col)
    lane = lax.broadcasted_iota(jnp.int32, (col.shape[0], LANES), 1)
    own, other = (0, 3) if own_first else (3, 0)
    x = jnp.where(lane == own, hi.astype(F32),
                  jnp.where(lane == own + 1, mid.astype(F32),
                            jnp.where(lane == own + 2, lo.astype(F32), 0.0)))
    x = jnp.where((lane >= other) & (lane < other + 3), 1.0, x)
    return x.astype(BF16)


def _fox_prompt_kernel(q_ref, k_ref, v_ref, fq_ref, fk_ref, o_ref, kaug, vbf, *, tq, scale):
    h = pl.program_id(1)
    qi = pl.program_id(2)

    def head_column(f):
        lane = lax.broadcasted_iota(jnp.int32, f.shape, 1)
        return jnp.sum(jnp.where(lane == h, f, 0.0), axis=1, keepdims=True)

    @pl.when(qi == 0)
    def _():
        kaug[:, :HEAD_DIM] = k_ref[...].astype(BF16)
        kaug[:, HEAD_DIM:] = _bias_lanes(head_column(fk_ref[...]) * (-1.0 / scale), False)
        vbf[...] = v_ref[...].astype(BF16)

    q = jnp.concatenate(
        [q_ref[...], _bias_lanes(head_column(fq_ref[...]) * (1.0 / scale), True)], axis=1)
    c2 = scale * 1.4426950408889634

    def step(j, carry, masked):
        m, l, acc = carry
        start = pl.multiple_of(j * tq, tq)
        s = lax.dot_general(q, kaug[pl.ds(start, tq), :], (((1,), (1,)), ((), ())),
                            preferred_element_type=F32) * c2
        if masked:
            r = lax.broadcasted_iota(jnp.int32, (tq, tq), 0)
            c = lax.broadcasted_iota(jnp.int32, (tq, tq), 1)
            s = jnp.where(c <= r, s, NEG_INF)
        m_new = jnp.maximum(m, jnp.max(s, axis=1, keepdims=True))
        alpha = jnp.exp2(m - m_new)
        p = jnp.exp2(s - m_new)
        l = alpha * l + jnp.sum(p, axis=1, keepdims=True)
        acc = alpha * acc + jnp.dot(p.astype(BF16), vbf[pl.ds(start, tq), :],
                                    preferred_element_type=F32)
        return m_new, l, acc

    init = (jnp.full((tq, 1), NEG_INF, F32), jnp.zeros((tq, 1), F32),
            jnp.zeros((tq, HEAD_DIM), F32))
    carry = lax.fori_loop(0, qi, functools.partial(step, masked=False), init)
    m, l, acc = step(qi, carry, True)
    o_ref[...] = (acc / l).astype(o_ref.dtype)


def _fox_prompt(q, k, v, f_col, batch, seq, heads, tq=512):
    tq = min(tq, seq)
    nq = seq // tq
    return pl.pallas_call(
        functools.partial(_fox_prompt_kernel, tq=tq, scale=HEAD_DIM ** -0.5),
        grid=(batch, heads, nq),
        in_specs=[pl.BlockSpec((tq, HEAD_DIM), lambda b, h, i: (b * nq + i, h)),
                  pl.BlockSpec((seq, HEAD_DIM), lambda b, h, i: (b, h)),
                  pl.BlockSpec((seq, HEAD_DIM), lambda b, h, i: (b, h)),
                  pl.BlockSpec((tq, LANES), lambda b, h, i: (b * nq + i, 0)),
                  pl.BlockSpec((seq, LANES), lambda b, h, i: (b, 0))],
        out_specs=pl.BlockSpec((tq, HEAD_DIM), lambda b, h, i: (b * nq + i, h)),
        out_shape=jax.ShapeDtypeStruct(q.shape, BF16),
        scratch_shapes=[pltpu.VMEM((seq, 2 * HEAD_DIM), BF16), pltpu.VMEM((seq, HEAD_DIM), BF16)],
        compiler_params=_params("arbitrary", "arbitrary", "arbitrary"),
        name="fox_prompt",
    )(q, k, v, f_col, f_col)


def _fox_sample_kernel(pt_ref, q_ref, *refs, n_steps, pps, heads, scale):
    del pt_ref
    ck_refs, cv_refs, lf_refs = refs[:pps], refs[pps:2 * pps], refs[2 * pps:3 * pps]
    kn_ref, vn_ref, fn_ref, fs_ref, o_ref, m_ref, l_ref, acc_ref, tail_ref = refs[3 * pps:]
    p = pl.program_id(1)

    @pl.when(p == 0)
    def _():
        m_ref[...] = jnp.full(m_ref.shape, NEG_INF, F32)
        l_ref[...] = jnp.zeros_like(l_ref)
        acc_ref[...] = jnp.zeros_like(acc_ref)
        tail_ref[...] = jnp.zeros_like(tail_ref)

    q = q_ref[...]
    rows = q.shape[0]
    fn = fn_ref[...]
    row = lax.broadcasted_iota(jnp.int32, (rows, LANES), 0)
    lane = lax.broadcasted_iota(jnp.int32, (rows, LANES), 1)
    own_head = (lane & (heads - 1)) == (row & (heads - 1))

    def attend(keys, vals, lane_bias, visible):
        s = lax.dot_general(q, keys, (((1,), (1,)), ((), ())), preferred_element_type=F32)
        parts = []
        for g in range(keys.shape[0] // LANES):
            sg = s[:, g * LANES:(g + 1) * LANES] * scale + (fn + lane_bias[g:g + 1, :])
            parts.append(jnp.where(visible, sg, NEG_INF))
        s = jnp.concatenate(parts, axis=1)
        m_old = m_ref[...]
        m_new = jnp.maximum(m_old, jnp.max(s, axis=1, keepdims=True))
        alpha = jnp.exp(m_old - m_new)
        pr = jnp.exp(s - m_new)
        l_ref[...] = alpha * l_ref[...] + jnp.sum(pr, axis=1, keepdims=True)
        acc_ref[...] = alpha * acc_ref[...] + jnp.dot(pr.astype(BF16), vals,
                                                      preferred_element_type=F32)
        m_ref[...] = m_new

    a = lax.broadcasted_iota(jnp.int32, (LANES, LANES), 0)
    b = lax.broadcasted_iota(jnp.int32, (LANES, LANES), 1)
    same_head = (a & (heads - 1)) == (b & (heads - 1))
    later_in_row = jnp.where(same_head & (a > b), 1.0, 0.0).astype(BF16)
    whole_row = jnp.where(same_head, 1.0, 0.0).astype(BF16)
    sums = jnp.concatenate([later_in_row, whole_row], axis=1)
    n_rows = lf_refs[0].shape[0]
    ra = lax.broadcasted_iota(jnp.int32, (n_rows, n_rows), 0)
    rb = lax.broadcasted_iota(jnp.int32, (n_rows, n_rows), 1)
    later_rows = jnp.where(rb > ra, 1.0, 0.0).astype(BF16)

    tail = tail_ref[...]
    biases = []
    for c in range(pps):
        lf = lf_refs[c][...]
        y = jnp.dot(jnp.concatenate(_split3(lf), axis=0), sums, preferred_element_type=F32)
        y = y[:n_rows] + y[n_rows:2 * n_rows] + y[2 * n_rows:]
        within, row_tot = y[:, :LANES], y[:, LANES:]
        z = jnp.dot(later_rows, jnp.concatenate(_split3(row_tot), axis=1),
                    preferred_element_type=F32)
        biases.append(within + z[:, :LANES] + z[:, LANES:2 * LANES] + z[:, 2 * LANES:] + tail)
        tail = tail + jnp.sum(row_tot, axis=0, keepdims=True)
    tail_ref[...] = tail

    attend(jnp.concatenate([r[...].astype(BF16) for r in ck_refs], axis=0),
           jnp.concatenate([r[...].astype(BF16) for r in cv_refs], axis=0),
           jnp.concatenate(biases, axis=0), own_head)

    @pl.when(p == n_steps - 1)
    def _():
        attend(kn_ref[...], vn_ref[...], -fs_ref[...],
               own_head & (lane - (lane & (heads - 1)) <= row - (row & (heads - 1))))
        o_ref[...] = acc_ref[...] / l_ref[...]


def _fox_sample(page_table, q_rows, cache_k, cache_v, cache_lf, k_new, v_new, fn_col, fs_lane,
                heads):
    nb, n_pages = page_table.shape
    rows, hd = q_rows.shape[1], q_rows.shape[2]
    lf_rows = cache_lf.shape[1]
    page_rows = lf_rows * LANES
    assert heads & (heads - 1) == 0 and LANES % heads == 0
    pps = math.gcd(n_pages, FOX_PAGES_PER_STEP)
    n_steps = n_pages // pps
    per_b = lambda b, p, pt: (b, 0, 0)

    def page_of(c):
        return lambda b, p, pt: pt[b, n_pages - 1 - (p * pps + c)]

    kv_specs = [pl.BlockSpec((page_rows, hd), lambda b, p, pt, f=page_of(c): (f(b, p, pt), 0))
                for c in range(pps)]
    lf_specs = [pl.BlockSpec((None, lf_rows, LANES),
                             lambda b, p, pt, f=page_of(c): (f(b, p, pt), 0, 0))
                for c in range(pps)]
    grid_spec = pltpu.PrefetchScalarGridSpec(
        num_scalar_prefetch=1,
        grid=(nb, n_steps),
        in_specs=([pl.BlockSpec((None, rows, hd), per_b)] + kv_specs + kv_specs + lf_specs
                  + [pl.BlockSpec((None, LANES, hd), per_b),
                     pl.BlockSpec((None, LANES, hd), per_b),
                     pl.BlockSpec((None, rows, 1), per_b),
                     pl.BlockSpec((None, 1, LANES), per_b)]),
        out_specs=pl.BlockSpec((None, rows, hd), per_b),
        scratch_shapes=[pltpu.VMEM((rows, 1), F32), pltpu.VMEM((rows, 1), F32),
                        pltpu.VMEM((rows, hd), F32), pltpu.VMEM((1, LANES), F32)],
    )
    return pl.pallas_call(
        functools.partial(_fox_sample_kernel, n_steps=n_steps, pps=pps, heads=heads,
                          scale=HEAD_DIM ** -0.5),
        grid_spec=grid_spec,
        out_shape=jax.ShapeDtypeStruct((nb, rows, hd), F32),
        compiler_params=_params("arbitrary", "arbitrary"),
        name="fox_sample",
    )(page_table, q_rows, *([cache_k] * pps), *([cache_v] * pps), *([cache_lf] * pps),
      k_new, v_new, fn_col, fs_lane)


def _sgu_kernel(u_ref, vg_ref, nw_ref, w_ref, bt_ref, *out_refs, groups):
    o_ref = out_refs[0]
    g = vg_ref[...].astype(F32)
    v = g * lax.rsqrt(jnp.mean(g * g, axis=-1, keepdims=True) + EPS) * nw_ref[...]
    if len(out_refs) > 1:
        out_refs[1][...] = v
    vb = v.astype(BF16)
    rows = v.shape[0]
    cw = v.shape[1] // groups
    r = lax.broadcasted_iota(jnp.int32, (rows, rows), 0)
    c = lax.broadcasted_iota(jnp.int32, (rows, rows), 1)
    bt = bt_ref[...]
    for gi in range(groups):
        wm = jnp.where(c <= r, w_ref[gi], 0.0).astype(BF16)
        z = jnp.dot(wm, vb[:, gi * cw:(gi + 1) * cw], preferred_element_type=F32)
        z = z + bt[:, gi:gi + 1]
        u = u_ref[:, gi * cw:(gi + 1) * cw].astype(F32)
        o_ref[:, gi * cw:(gi + 1) * cw] = (u * z).astype(o_ref.dtype)


def _sgu(uv, norm_w, w_pos, bias_t, rows, emit_v):
    m, two_w = uv.shape
    width = two_w // 2
    groups = w_pos.shape[0]
    out_shape = [jax.ShapeDtypeStruct((m, width), BF16)]
    out_specs = [pl.BlockSpec((rows, width), lambda i: (i, 0))]
    if emit_v:
        out_shape.append(jax.ShapeDtypeStruct((m, width), F32))
        out_specs.append(pl.BlockSpec((rows, width), lambda i: (i, 0)))
    return pl.pallas_call(
        functools.partial(_sgu_kernel, groups=groups),
        grid=(m // rows,),
        in_specs=[pl.BlockSpec((rows, width), lambda i: (i, 0)),
                  pl.BlockSpec((rows, width), lambda i: (i, 1)),
                  pl.BlockSpec((1, width), lambda i: (0, 0)),
                  pl.BlockSpec((groups, rows, rows), lambda i: (0, 0, 0)),
                  pl.BlockSpec((rows, groups), lambda i: (0, 0))],
        out_specs=out_specs,
        out_shape=out_shape,
        compiler_params=_params("arbitrary"),
        name="sgu",
    )(uv, uv, norm_w, w_pos, bias_t)


def _conv_sample_kernel(z0_ref, z1_ref, z2_ref, gb_ref, w_ref, o_ref):
    w = w_ref[...]
    y = w[0:1] * z0_ref[...] + w[1:2] * z1_ref[...] + w[2:3] * z2_ref[...]
    o_ref[...] = (gb_ref[...].astype(F32) * y).astype(o_ref.dtype)


def _conv_sample(z0, z1, z2, gb, conv_w):
    return pl.pallas_call(
        _conv_sample_kernel,
        out_shape=jax.ShapeDtypeStruct(z0.shape, BF16),
        name="conv_sample",
    )(z0, z1, z2, gb, conv_w)


def _gla_kernel(q_ref, k_ref, v_ref, la_ref, gate_ref, nw_ref, s0_ref, o_ref, st_ref, st, *, nc):
    ci = pl.program_id(2)

    @pl.when(ci == 0)
    def _():
        st[...] = s0_ref[...]

    la = la_ref[...]
    c, dk = la.shape
    r = lax.broadcasted_iota(jnp.int32, (c, c), 0)
    cc = lax.broadcasted_iota(jnp.int32, (c, c), 1)
    tril = cc <= r
    y = jnp.dot(jnp.where(tril, 1.0, 0.0).astype(BF16),
                jnp.concatenate(_split3(la), axis=1), preferred_element_type=F32)
    bc = y[:, :dk] + y[:, dk:2 * dk] + y[:, 2 * dk:]
    q = q_ref[...].astype(F32) * dk ** -0.5
    k = k_ref[...].astype(F32)
    v = v_ref[...]
    qt = (q * jnp.exp(bc)).astype(BF16)
    mid = bc[c // 2 - 1:c // 2, :]
    qa = (q * jnp.exp(bc - mid)).astype(BF16)
    ka = (k * jnp.exp(mid - bc)).astype(BF16)
    att = lax.dot_general(qa, ka, (((1,), (1,)), ((), ())), preferred_element_type=F32)
    att = jnp.where(tril, att, 0.0)
    s_t = st[...]
    o = (lax.dot_general(qt, s_t.astype(BF16), (((1,), (1,)), ((), ())),
                         preferred_element_type=F32)
         + jnp.dot(att.astype(BF16), v, preferred_element_type=F32))
    bl = bc[c - 1:c, :]
    kd = (k * jnp.exp(bl - bc)).astype(BF16)
    st[...] = s_t * jnp.exp(bl) + lax.dot_general(v, kd, (((0,), (0,)), ((), ())),
                                                   preferred_element_type=F32)
    on = o * lax.rsqrt(jnp.mean(o * o, axis=-1, keepdims=True) + EPS) * nw_ref[...]
    o_ref[...] = (on * gate_ref[...].astype(F32)).astype(o_ref.dtype)

    @pl.when(ci == nc - 1)
    def _():
        st_ref[...] = st[...]


def _gla(qkv, log_a, gate, norm_w, s0_t, batch, seq, heads, chunk):
    m = qkv.shape[0]
    dk = log_a.shape[1] // heads
    dv = gate.shape[1] // heads
    assert (2 * heads * dk) % dv == 0
    v0 = (2 * heads * dk) // dv
    chunk = min(chunk, seq)
    nc = seq // chunk
    row = lambda b, h, c: b * nc + c
    return pl.pallas_call(
        functools.partial(_gla_kernel, nc=nc),
        grid=(batch, heads, nc),
        in_specs=[pl.BlockSpec((chunk, dk), lambda b, h, c: (row(b, h, c), h)),
                  pl.BlockSpec((chunk, dk), lambda b, h, c: (row(b, h, c), heads + h)),
                  pl.BlockSpec((chunk, dv), lambda b, h, c: (row(b, h, c), v0 + h)),
                  pl.BlockSpec((chunk, dk), lambda b, h, c: (row(b, h, c), h)),
                  pl.BlockSpec((chunk, dv), lambda b, h, c: (row(b, h, c), h)),
                  pl.BlockSpec((1, dv), lambda b, h, c: (0, h)),
                  pl.BlockSpec((None, None, dv, dk), lambda b, h, c: (b, h, 0, 0))],
        out_specs=[pl.BlockSpec((chunk, dv), lambda b, h, c: (row(b, h, c), h)),
                   pl.BlockSpec((None, None, dv, dk), lambda b, h, c: (b, h, 0, 0))],
        out_shape=[jax.ShapeDtypeStruct((m, heads * dv), BF16),
                   jax.ShapeDtypeStruct((batch, heads, dv, dk), F32)],
        scratch_shapes=[pltpu.VMEM((dv, dk), F32)],
        compiler_params=_params("arbitrary", "arbitrary", "arbitrary"),
        name="gla",
    )(qkv, qkv, qkv, log_a, gate, norm_w, s0_t)


def kernel(x_prompt, x_sample, cache_k, cache_v, cache_logf, state_conv, state_gla, page_table,
           c_prompt, c_sample, w_in0, f_bias, q_norm, k_norm, sgu_norm, sgu_w, sgu_b, w_out0,
           w_in1, conv_w, gla_a_w2, gla_a_b, gla_norm, w_out1, ada_w, ada_b, mlp_w1, mlp_w2):
    nbp, seq, d = x_prompt.shape
    nbs, dseq, _ = x_sample.shape
    mp, ms = nbp * seq, nbs * dseq
    fox_w = d // 2
    heads = fox_w // HEAD_DIM
    sgu_width = d // 2
    groups = sgu_w.shape[1]
    conv_width = state_conv.shape[-1]
    gla_heads, gla_dk, gla_dv = state_gla.shape[2], state_gla.shape[3], state_gla.shape[4]
    gla_key, gla_val = gla_heads * gla_dk, gla_heads * gla_dv
    gla_rank = gla_a_w2.shape[1]
    n_pool, page = cache_k.shape[1], cache_k.shape[2]
    n_pages = page_table.shape[1]

    r_c = nbp + nbs
    r_pad = -(-r_c // 16) * 16
    c_rows = jnp.pad(jnp.concatenate([c_prompt, c_sample], axis=0),
                     ((0, r_pad - r_c), (0, 0))).astype(BF16)
    mod = _ada(c_rows, ada_w, ada_b)

    def mods(layer):
        parts = jnp.split(mod[layer], 6, axis=-1)
        pp = [p[:nbp].reshape(nbp, 1, d) for p in parts]
        ps = [jnp.repeat(p[nbp:r_c], dseq, axis=0) for p in parts]
        return pp, ps

    xp = x_prompt.reshape(mp, d)
    xs = x_sample.reshape(ms, d)

    def modulate_both(xp, xs, shp, scp, shs, scs):
        hp = _modulate(xp, shp, scp, seq)
        hs = _modulate(xs, shs.reshape(1, ms, d), scs.reshape(1, ms, d), ms, tr=ms)
        return hp, hs

    def mlp(xp, xs, layer, shp, scp, gp, shs, scs, gs):
        hp, hs = modulate_both(xp, xs, shp, scp, shs, scs)
        (ap,), (as_,) = _proj([hp], [hs], [(mlp_w1, layer, 0)], mlp_w1.shape[2], _epi_relu2,
                              [BF16], name="mlp_up")
        return _down(ap, as_, mlp_w2, layer, xp, gp, xs, gs, seq)

    def residual_proj(lhs_p, lhs_s, w, xp, xs, gp, gs, name):
        (yp,), (ys,) = _proj(lhs_p, lhs_s, [(w, 0, 0)], d, _epi_residual, [F32],
                             ext=[("res_p", "prow", xp), ("gate_p", "pgrp", gp),
                                  ("res_s", "srow", xs), ("gate_s", "srow", gs)],
                             rows_per_group=seq, name=name)
        return yp, ys

    (sh1p, sc1p, g1p, sh2p, sc2p, g2p), (sh1s, sc1s, g1s, sh2s, sc2s, g2s) = mods(0)
    hp, hs = modulate_both(xp, xs, sh1p, sc1p, sh1s, sc1s)
    w0 = jnp.swapaxes(w_in0, 1, 2)
    qn = jnp.tile(q_norm[0], heads).reshape(1, fox_w)
    kn = jnp.tile(k_norm[0], heads).reshape(1, fox_w)
    (q_p,), (q_s,) = _proj([hp], [hs], [(w0, 0, 0)], fox_w, _epi_head_norm, [BF16],
                           ext=[("norm", "col", qn)], w_rows=True, name="in0_q")
    (k_p,), (k_s,) = _proj([hp], [hs], [(w0, 0, fox_w)], fox_w, _epi_head_norm, [F32],
                           ext=[("norm", "col", kn)], w_rows=True, name="in0_k")
    (v_p,), (v_s,) = _proj([hp], [hs], [(w0, 0, 2 * fox_w)], fox_w, _epi_plain, [F32],
                           w_rows=True, name="in0_v")
    fb = jnp.pad(f_bias[0], (0, LANES - heads)).reshape(1, LANES)
    (lf_p,), (lf_s,) = _proj([hp], [hs], [(w0, 0, 3 * fox_w)], LANES, _epi_log_forget, [F32],
                             ext=[("bias", "col", fb)], w_rows=True, name="in0_logf")
    (uv_p,), (uv_s,) = _proj([hp], [hs], [(w0, 0, 3 * fox_w + heads)], 2 * sgu_width, _epi_gelu,
                             [BF16], w_rows=True, name="in0_uv")

    f_col = _cumsum_rows(lf_p, nbp)
    oa_p = _fox_prompt(q_p, k_p, v_p, f_col, nbp, seq, heads)

    lf_s16 = lf_s[:, :heads].reshape(nbs, dseq, heads)
    tpad = 8
    lf_s_pad = jnp.pad(lf_s.reshape(nbs, dseq, LANES), ((0, 0), (0, tpad - dseq), (0, 0)))
    fn = _cumsum_rows(lf_s_pad.reshape(nbs * tpad, LANES), nbs).reshape(
        nbs, tpad, LANES)[:, :dseq, :heads]
    rows_s = dseq * heads
    assert rows_s <= LANES

    def new_rows(a):
        a = a.reshape(nbs, rows_s, HEAD_DIM)
        return jnp.pad(a, ((0, 0), (0, LANES - rows_s), (0, 0))).astype(BF16)

    fs_lane = jnp.pad(fn.reshape(nbs, 1, rows_s), ((0, 0), (0, 0), (0, LANES - rows_s)))
    oa_rows = _fox_sample(page_table, q_s.reshape(nbs, rows_s, HEAD_DIM),
                          cache_k[0].reshape(n_pool * page * heads, HEAD_DIM),
                          cache_v[0].reshape(n_pool * page * heads, HEAD_DIM),
                          cache_logf[0].reshape(n_pool, page * heads // LANES, LANES),
                          new_rows(k_s), new_rows(v_s), fn.reshape(nbs, rows_s, 1), fs_lane, heads)
    oa_s = oa_rows.reshape(ms, fox_w).astype(BF16)

    sn = sgu_norm[0].reshape(1, sgu_width)
    rows_p = min(seq, SGU_CHUNK)
    (ob_p,) = _sgu(uv_p, sn, sgu_w[0][:, :rows_p, :rows_p], sgu_b[0][:, :rows_p].T, rows_p, False)
    rows_g = min(dseq, SGU_CHUNK)
    w_small = sgu_w[0][:, :rows_g, :rows_g]
    w_big = jnp.einsum("ab,gts->gatbs", jnp.eye(nbs, dtype=F32), w_small).reshape(groups, ms, ms)
    b_big = jnp.tile(sgu_b[0][:, :rows_g].T, (nbs, 1))
    ob_s, sgu_v = _sgu(uv_s, sn, w_big, b_big, ms, True)

    xp, xs = residual_proj([oa_p, ob_p], [oa_s, ob_s], w_out0, xp, xs, g1p, g1s, "out0")
    xp, xs = mlp(xp, xs, 0, sh2p, sc2p, g2p, sh2s, sc2s, g2s)

    (sh1p, sc1p, g1p, sh2p, sc2p, g2p), (sh1s, sc1s, g1s, sh2s, sc2s, g2s) = mods(1)
    hp, hs = modulate_both(xp, xs, sh1p, sc1p, sh1s, sc1s)
    w1 = jnp.swapaxes(w_in1, 1, 2)
    cw = conv_width
    cwt = conv_w[0]
    n_tap = cwt.shape[0]
    (oc_p, ztail_p), (gb_s, z_s) = _proj(
        [hp], [hs], [(w1, 0, 0), (w1, 0, cw), (w1, 0, 2 * cw)], cw, _make_epi_conv(seq),
        [BF16, F32], ext=[("taps", "col", cwt)], tm=512, tn=256, rows_per_group=seq, w_rows=True,
        tail_rows=(1,), scratch=lambda tm, tn: [pltpu.VMEM((tm + 8, tn), F32)], name="in1_conv")
    conv_p = ztail_p.reshape(nbp, -1, 8, cw)[:, -1, 8 - (n_tap - 1):, :]
    (qkv_p,), (qkv_s,) = _proj([hp], [hs], [(w1, 0, 3 * cw)], 2 * gla_key + gla_val, _epi_plain,
                               [BF16], w_rows=True, name="in1_qkv")
    c_ga = 3 * cw + 2 * gla_key + gla_val
    (ga_p,), (ga_s,) = _proj([hp], [hs], [(w1, 0, c_ga)], LANES, _epi_plain, [BF16],
                             w_rows=True, name="in1_ga")
    w_a2 = jnp.pad(gla_a_w2, ((0, 0), (0, LANES - gla_rank), (0, 0)))
    (la_p,), (la_s,) = _proj([ga_p], [ga_s], [(w_a2, 0, 0)], gla_key, _epi_log_decay, [F32],
                             ext=[("bias", "col", gla_a_b[0].reshape(1, gla_key))], name="in1_log_a")
    (gg_p,), (gg_s,) = _proj([hp], [hs], [(w1, 0, c_ga + gla_rank)], gla_val, _epi_silu, [BF16],
                             w_rows=True, name="in1_gate")

    zp_s = jnp.concatenate([state_conv[0], z_s.reshape(nbs, dseq, cw)], axis=1)
    shifted = [zp_s[:, i:i + dseq].reshape(ms, cw) for i in range(cwt.shape[0])]
    oc_s = _conv_sample(shifted[0], shifted[1], shifted[2], gb_s, cwt)
    conv_s = zp_s[:, dseq:, :]

    gn = gla_norm[0].reshape(1, gla_val)
    s0_p = jnp.zeros((nbp, gla_heads, gla_dv, gla_dk), F32)
    od_p, st_p = _gla(qkv_p, la_p, gg_p, gn, s0_p, nbp, seq, gla_heads, GLA_CHUNK_ROWS)
    pad_t = 16

    def pad_rows(a):
        return jnp.pad(a.reshape(nbs, dseq, -1), ((0, 0), (0, pad_t - dseq), (0, 0))).reshape(
            nbs * pad_t, -1)

    od_s_pad, st_s = _gla(pad_rows(qkv_s), pad_rows(la_s), pad_rows(gg_s), gn,
                          state_gla[0].swapaxes(-1, -2), nbs, pad_t, gla_heads, pad_t)
    od_s = od_s_pad.reshape(nbs, pad_t, gla_val)[:, :dseq].reshape(ms, gla_val)

    xp, xs = residual_proj([oc_p, od_p], [oc_s, od_s], w_out1, xp, xs, g1p, g1s, "out1")
    xp, xs = mlp(xp, xs, 1, sh2p, sc2p, g2p, sh2s, sc2s, g2s)

    y_prompt = xp.reshape(nbp, seq, d)
    y_sample = xs.reshape(nbs, dseq, d)
    return (y_prompt, y_sample,
            _to_heads(k_p, heads).reshape(1, nbp, seq, heads, HEAD_DIM),
            _to_heads(v_p, heads).reshape(1, nbp, seq, heads, HEAD_DIM),
            lf_p[:, :heads].reshape(1, nbp, seq, heads),
            k_s.reshape(1, nbs, dseq, heads, HEAD_DIM), v_s.reshape(1, nbs, dseq, heads, HEAD_DIM),
            lf_s16[None], sgu_v.reshape(1, nbs, dseq, sgu_width),
            conv_p[None], conv_s[None],
            st_p.swapaxes(-1, -2)[None], st_s.swapaxes(-1, -2)[None])
```

```python
import functools
import math

import jax
import jax.numpy as jnp
from jax import lax
from jax.experimental import pallas as pl
from jax.experimental.pallas import tpu as pltpu

F32 = jnp.float32
BF16 = jnp.bfloat16
EPS = 1e-6
NEG_INF = -1e30
HEAD_DIM = 128
LANES = 128
SGU_CHUNK = 128
GLA_TAU = 16.0
GLA_CHUNK_ROWS = 128
FOX_PAGES_PER_STEP = 4
VMEM_LIMIT_BYTES = 56 * 1024 * 1024


def _params(*semantics):
    return pltpu.CompilerParams(dimension_semantics=semantics,
                                vmem_limit_bytes=VMEM_LIMIT_BYTES)


def _split3(x):
    hi = x.astype(BF16)
    r = x - hi.astype(F32)
    mid = r.astype(BF16)
    lo = (r - mid.astype(F32)).astype(BF16)
    return hi, mid, lo


def _log_sigmoid(x):
    return jnp.minimum(x, 0.0) - jnp.log1p(jnp.exp(-jnp.abs(x)))


def _gelu_tanh(x):
    c = 0.7978845608028654
    return 0.5 * x * (1.0 + jnp.tanh(c * (x + 0.044715 * (x * x * x))))


def _ada_kernel(c_ref, w_ref, b_ref, o_ref):
    o_ref[...] = jnp.dot(c_ref[...], w_ref[...].astype(BF16),
                         preferred_element_type=F32) + b_ref[...]


def _ada(c_rows, ada_w, ada_b, tn=512):
    n_layers, d, n = ada_w.shape
    r = c_rows.shape[0]
    tn = min(tn, n)
    return pl.pallas_call(
        _ada_kernel,
        grid=(n_layers, n // tn),
        in_specs=[pl.BlockSpec((r, d), lambda l, j: (0, 0)),
                  pl.BlockSpec((None, d, tn), lambda l, j: (l, 0, j)),
                  pl.BlockSpec((None, 1, tn), lambda l, j: (l, 0, j))],
        out_specs=pl.BlockSpec((None, r, tn), lambda l, j: (l, 0, j)),
        out_shape=jax.ShapeDtypeStruct((n_layers, r, n), F32),
        compiler_params=_params("arbitrary", "arbitrary"),
        name="ada",
    )(c_rows, ada_w, ada_b.reshape(n_layers, 1, n))


def _modulate_kernel(x_ref, sh_ref, sc_ref, o_ref):
    x = x_ref[...]
    y = x * lax.rsqrt(jnp.mean(x * x, axis=-1, keepdims=True) + EPS)
    o_ref[...] = (y * (1.0 + sc_ref[...]) + sh_ref[...]).astype(o_ref.dtype)


def _modulate(x, sh, sc, rows_per_group, tr=256):
    rows, d = x.shape
    tr = min(tr, rows)
    r = sh.shape[1]
    grp = lambda i: ((i * tr) // rows_per_group, 0, 0)
    return pl.pallas_call(
        _modulate_kernel,
        grid=(rows // tr,),
        in_specs=[pl.BlockSpec((tr, d), lambda i: (i, 0)),
                  pl.BlockSpec((None, r, d), grp),
                  pl.BlockSpec((None, r, d), grp)],
        out_specs=pl.BlockSpec((tr, d), lambda i: (i, 0)),
        out_shape=jax.ShapeDtypeStruct((rows, d), BF16),
        compiler_params=_params("arbitrary"),
        name="modulate",
    )(x, sh, sc)


def _proj_kernel(*refs, k_sizes, n_w, ext_names, n_out, epilogue, cast_rows, w_rows, n_scratch):
    n_lhs = len(k_sizes)
    pos = 0
    xp = refs[pos:pos + n_lhs]; pos += n_lhs
    xs = refs[pos:pos + n_lhs]; pos += n_lhs
    w = refs[pos:pos + n_w]; pos += n_w
    ext = dict(zip(ext_names, refs[pos:pos + len(ext_names)])); pos += len(ext_names)
    out_p = refs[pos:pos + n_out]; pos += n_out
    out_s = refs[pos:pos + n_out]; pos += n_out
    wbf = refs[pos:pos + n_w]; pos += n_w
    ext["scratch"] = refs[pos:pos + n_scratch]

    def accumulate(lhs):
        accs = []
        for m in range(n_w):
            acc = None
            off = 0
            for a, ka in enumerate(k_sizes):
                part = jnp.dot(lhs[a][...], wbf[m][off:off + ka, :],
                               preferred_element_type=F32)
                acc = part if acc is None else acc + part
                off += ka
            accs.append(acc)
        return accs

    @pl.when(pl.program_id(1) == 0)
    def _():
        k_total = sum(k_sizes)
        for m in range(n_w):
            for r0 in range(0, k_total, cast_rows):
                if w_rows:
                    wbf[m][r0:r0 + cast_rows, :] = w[m][:, r0:r0 + cast_rows].T.astype(BF16)
                else:
                    wbf[m][r0:r0 + cast_rows, :] = w[m][r0:r0 + cast_rows, :].astype(BF16)
        for o_ref, val in zip(out_s, epilogue(accumulate(xs), ext, True)):
            o_ref[...] = val.astype(o_ref.dtype)

    for o_ref, val in zip(out_p, epilogue(accumulate(xp), ext, False)):
        o_ref[...] = val.astype(o_ref.dtype)


def _proj(xp, xs, weights, n_cols, epilogue, out_dtypes, ext=(), *, tm=1024, tn=512,
          rows_per_group=None, w_rows=False, tail_rows=(), scratch=None, name="proj"):
    mp, ms = xp[0].shape[0], xs[0].shape[0]
    k_sizes = tuple(int(a.shape[1]) for a in xp)
    k_total = sum(k_sizes)
    tm = min(tm, mp, rows_per_group or mp)
    tn = min(tn, n_cols)
    assert mp % tm == 0 and n_cols % tn == 0
    for warr, _, c0 in weights:
        assert warr.ndim == 3
        if w_rows:
            assert warr.shape[2] == k_total and c0 % 8 == 0 and len(k_sizes) == 1
        else:
            assert warr.shape[1] == k_total and c0 % tn == 0
    in_specs, args = [], []
    for a in xp:
        in_specs.append(pl.BlockSpec((tm, a.shape[1]), lambda j, i: (i, 0)))
        args.append(a)
    for a in xs:
        in_specs.append(pl.BlockSpec((ms, a.shape[1]), lambda j, i: (0, 0)))
        args.append(a)
    for warr, layer, c0 in weights:
        if w_rows:
            spec = pl.BlockSpec((None, pl.Element(tn), pl.Element(k_total)),
                                lambda j, i, c=c0, l=layer: (l, pl.multiple_of(c + j * tn, 8), 0))
        else:
            spec = pl.BlockSpec((None, k_total, tn),
                                lambda j, i, cb=c0 // tn, l=layer: (l, 0, j + cb))
        in_specs.append(spec)
        args.append(warr)
    ext_names = []
    for ename, kind, arr in ext:
        ext_names.append(ename)
        if kind == "col":
            spec = pl.BlockSpec((arr.shape[0], tn), lambda j, i: (0, j))
        elif kind == "prow":
            spec = pl.BlockSpec((tm, tn), lambda j, i: (i, j))
        elif kind == "pgrp":
            spec = pl.BlockSpec((None, 1, tn),
                                lambda j, i: ((i * tm) // rows_per_group, 0, j))
        elif kind == "srow":
            spec = pl.BlockSpec((ms, tn), lambda j, i: (0, j))
        else:
            raise ValueError(kind)
        in_specs.append(spec)
        args.append(arr)
    n_out = len(out_dtypes)
    rows_p = [(8 * (mp // tm), 8) if o in tail_rows else (mp, tm) for o in range(n_out)]
    out_specs = ([pl.BlockSpec((br, tn), lambda j, i: (i, j)) for _, br in rows_p]
                 + [pl.BlockSpec((ms, tn), lambda j, i: (0, j))] * n_out)
    out_shape = ([jax.ShapeDtypeStruct((r, n_cols), dt) for (r, _), dt in zip(rows_p, out_dtypes)]
                 + [jax.ShapeDtypeStruct((ms, n_cols), dt) for dt in out_dtypes])
    cast_rows = min(512, k_total)
    assert k_total % cast_rows == 0
    extra_scratch = scratch(tm, tn) if scratch else []
    outs = pl.pallas_call(
        functools.partial(_proj_kernel, k_sizes=k_sizes, n_w=len(weights),
                          ext_names=tuple(ext_names), n_out=n_out, epilogue=epilogue,
                          cast_rows=cast_rows, w_rows=w_rows, n_scratch=len(extra_scratch)),
        grid=(n_cols // tn, mp // tm),
        in_specs=in_specs,
        out_specs=out_specs,
        out_shape=out_shape,
        scratch_shapes=[pltpu.VMEM((k_total, tn), BF16) for _ in weights] + extra_scratch,
        compiler_params=_params("arbitrary", "arbitrary"),
        name=name,
    )(*args)
    return outs[:n_out], outs[n_out:]


def _epi_plain(accs, ext, is_sample):
    return (accs[0],)


def _epi_head_norm(accs, ext, is_sample):
    acc = accs[0]
    wn = ext["norm"][...]
    parts = []
    for c0 in range(0, acc.shape[1], HEAD_DIM):
        a = acc[:, c0:c0 + HEAD_DIM]
        y = a * lax.rsqrt(jnp.mean(a * a, axis=-1, keepdims=True) + EPS)
        parts.append(y * wn[:, c0:c0 + HEAD_DIM])
    return (jnp.concatenate(parts, axis=1),)


def _epi_log_forget(accs, ext, is_sample):
    return (_log_sigmoid(accs[0] + ext["bias"][...]),)


def _epi_gelu(accs, ext, is_sample):
    return (_gelu_tanh(accs[0]),)


def _epi_silu(accs, ext, is_sample):
    a = accs[0]
    return (a * jax.nn.sigmoid(a),)


def _epi_relu2(accs, ext, is_sample):
    r = jnp.maximum(accs[0], 0.0)
    return (r * r,)


def _make_epi_conv(seq):
    def epilogue(accs, ext, is_sample):
        z = accs[1] * accs[0]
        if is_sample:
            return accs[2], z
        (zbuf,) = ext["scratch"]
        tm = z.shape[0]

        @pl.when(pl.program_id(1) % (seq // tm) == 0)
        def _():
            zbuf[0:8, :] = jnp.zeros((8, zbuf.shape[1]), F32)

        zbuf[8:8 + tm, :] = z
        w = ext["taps"][...]
        y = w[0:1] * zbuf[6:6 + tm, :] + w[1:2] * zbuf[7:7 + tm, :] + w[2:3] * z
        tail = z[tm - 8:tm, :]
        zbuf[0:8, :] = tail
        return accs[2] * y, tail

    return epilogue


def _epi_log_decay(accs, ext, is_sample):
    return (_log_sigmoid(accs[0] + ext["bias"][...]) / GLA_TAU,)


def _epi_residual(accs, ext, is_sample):
    if is_sample:
        return (ext["res_s"][...] + ext["gate_s"][...] * accs[0],)
    return (ext["res_p"][...] + ext["gate_p"][...] * accs[0],)


def _down_kernel(xp_ref, xs_ref, w_ref, resp_ref, gp_ref, ress_ref, gs_ref,
                 op_ref, os_ref, accs_ref, wbf, *, nk, cast_rows):
    i = pl.program_id(1)
    k = pl.program_id(2)

    @pl.when(k == 0)
    def _():
        op_ref[...] = jnp.zeros_like(op_ref)

    total = None
    for r0 in range(0, w_ref.shape[0], cast_rows):
        wbf[r0:r0 + cast_rows, :] = w_ref[r0:r0 + cast_rows, :].astype(BF16)
        part = jnp.dot(xp_ref[:, r0:r0 + cast_rows], wbf[r0:r0 + cast_rows, :],
                       preferred_element_type=F32)
        total = part if total is None else total + part
    op_ref[...] += total

    @pl.when(k == nk - 1)
    def _():
        op_ref[...] = resp_ref[...] + gp_ref[...] * op_ref[...]

    @pl.when(i == 0)
    def _():
        ps = jnp.dot(xs_ref[...], wbf[...], preferred_element_type=F32)

        @pl.when(k == 0)
        def _():
            accs_ref[...] = ps

        @pl.when(k > 0)
        def _():
            accs_ref[...] += ps

        @pl.when(k == nk - 1)
        def _():
            os_ref[...] = ress_ref[...] + gs_ref[...] * accs_ref[...]


def _down(xp, xs, w, layer, res_p, gate_p, res_s, gate_s, rows_per_group, *,
          tm=1024, tn=1024, tk=2048):
    mp, kdim = xp.shape
    ms = xs.shape[0]
    n = w.shape[2]
    tm, tn, tk = min(tm, rows_per_group, mp), min(tn, n), min(tk, kdim)
    nk = kdim // tk
    cast_rows = min(512, tk)
    assert tk % cast_rows == 0
    return pl.pallas_call(
        functools.partial(_down_kernel, nk=nk, cast_rows=cast_rows),
        grid=(n // tn, mp // tm, nk),
        in_specs=[pl.BlockSpec((tm, tk), lambda j, i, k: (i, k)),
                  pl.BlockSpec((ms, tk), lambda j, i, k: (0, k)),
                  pl.BlockSpec((None, tk, tn), lambda j, i, k: (layer, k, j)),
                  pl.BlockSpec((tm, tn), lambda j, i, k: (i, j)),
                  pl.BlockSpec((None, 1, tn),
                               lambda j, i, k: ((i * tm) // rows_per_group, 0, j)),
                  pl.BlockSpec((ms, tn), lambda j, i, k: (0, j)),
                  pl.BlockSpec((ms, tn), lambda j, i, k: (0, j))],
        out_specs=[pl.BlockSpec((tm, tn), lambda j, i, k: (i, j)),
                   pl.BlockSpec((ms, tn), lambda j, i, k: (0, j))],
        out_shape=[jax.ShapeDtypeStruct((mp, n), F32),
                   jax.ShapeDtypeStruct((ms, n), F32)],
        scratch_shapes=[pltpu.VMEM((ms, tn), F32), pltpu.VMEM((tk, tn), BF16)],
        compiler_params=_params("arbitrary", "arbitrary", "arbitrary"),
        name="mlp_down",
    )(xp, xs, w, res_p, gate_p, res_s, gate_s)


def _to_heads_kernel(x_ref, o_ref):
    for h in range(o_ref.shape[1]):
        o_ref[:, h, :] = x_ref[:, h * HEAD_DIM:(h + 1) * HEAD_DIM]


def _to_heads(x, heads, tr=256):
    m, w = x.shape
    tr = min(tr, m)
    return pl.pallas_call(
        _to_heads_kernel,
        grid=(m // tr,),
        in_specs=[pl.BlockSpec((tr, w), lambda i: (i, 0))],
        out_specs=pl.BlockSpec((tr, heads, HEAD_DIM), lambda i: (i, 0, 0)),
        out_shape=jax.ShapeDtypeStruct((m, heads, HEAD_DIM), x.dtype),
        compiler_params=_params("arbitrary"),
        name="to_heads",
    )(x)


def _cumsum_kernel(x_ref, o_ref, carry_ref):
    @pl.when(pl.program_id(1) == 0)
    def _():
        carry_ref[...] = jnp.zeros_like(carry_ref)

    x = x_ref[...]
    t, w = x.shape
    r = lax.broadcasted_iota(jnp.int32, (t, t), 0)
    c = lax.broadcasted_iota(jnp.int32, (t, t), 1)
    tri = jnp.where(c <= r, 1.0, 0.0).astype(BF16)
    y = jnp.dot(tri, jnp.concatenate(_split3(x), axis=1), preferred_element_type=F32)
    out = y[:, :w] + y[:, w:2 * w] + y[:, 2 * w:] + carry_ref[...]
    o_ref[...] = out
    carry_ref[...] = out[t - 1:t, :]


def _cumsum_rows(x, n_groups, tc=256):
    rows, w = x.shape
    per = rows // n_groups
    tc = min(tc, per)
    nt = per // tc
    return pl.pallas_call(
        _cumsum_kernel,
        grid=(n_groups, nt),
        in_specs=[pl.BlockSpec((tc, w), lambda b, t: (b * nt + t, 0))],
        out_specs=pl.BlockSpec((tc, w), lambda b, t: (b * nt + t, 0)),
        out_shape=jax.ShapeDtypeStruct((rows, w), F32),
        scratch_shapes=[pltpu.VMEM((1, w), F32)],
        compiler_params=_params("arbitrary", "arbitrary"),
        name="cumsum_logf",
    )(x)


def _bias_lanes(col, own_first):
    hi, mid, lo = _split3(col)
    lane = lax.broadcasted_iota(jnp.int32, (col.shape[0], LANES), 1)
    own, other = (0, 3) if own_first else (3, 0)
    x = jnp.where(lane == own, hi.astype(F32),
                  jnp.where(lane == own + 1, mid.astype(F32),
                            jnp.where(lane == own + 2, lo.astype(F32), 0.0)))
    x = jnp.where((lane >= other) & (lane < other + 3), 1.0, x)
    return x.astype(BF16)


def _fox_prompt_kernel(q_ref, k_ref, v_ref, fq_ref, fk_ref, o_ref, kaug, vbf, *, tq, nq, scale):
    h = pl.program_id(1)
    qi = pl.program_id(2)

    def head_column(f):
        lane = lax.broadcasted_iota(jnp.int32, f.shape, 1)
        return jnp.sum(jnp.where(lane == h, f, 0.0), axis=1, keepdims=True)

    @pl.when(qi == 0)
    def _():
        kaug[:, :HEAD_DIM] = k_ref[...].astype(BF16)
        kaug[:, HEAD_DIM:] = _bias_lanes(head_column(fk_ref[...]) * (-1.0 / scale), False)
        vbf[...] = v_ref[...].astype(BF16)

    q = jnp.concatenate(
        [q_ref[...], _bias_lanes(head_column(fq_ref[...]) * (1.0 / scale), True)], axis=1)
    c2 = scale * 1.4426950408889634

    def step(j, carry, masked):
        m, l, acc = carry
        start = j * tq
        s = lax.dot_general(q, kaug[pl.ds(start, tq), :], (((1,), (1,)), ((), ())),
                            preferred_element_type=F32) * c2
        if masked:
            r = lax.broadcasted_iota(jnp.int32, (tq, tq), 0)
            c = lax.broadcasted_iota(jnp.int32, (tq, tq), 1)
            s = jnp.where(c <= r, s, NEG_INF)
        m_new = jnp.maximum(m, jnp.max(s, axis=1, keepdims=True))
        alpha = jnp.exp2(m - m_new)
        p = jnp.exp2(s - m_new)
        l = alpha * l + jnp.sum(p, axis=1, keepdims=True)
        acc = alpha * acc + jnp.dot(p.astype(BF16), vbf[pl.ds(start, tq), :],
                                    preferred_element_type=F32)
        return m_new, l, acc

    init = (jnp.full((tq, 1), NEG_INF, F32), jnp.zeros((tq, 1), F32),
            jnp.zeros((tq, HEAD_DIM), F32))
    for n_full in range(nq):
        @pl.when(qi == n_full)
        def _(n_full=n_full):
            carry = init
            for j in range(n_full):
                carry = step(j, carry, False)
            _, l, acc = step(n_full, carry, True)
            o_ref[...] = (acc / l).astype(o_ref.dtype)


def _fox_prompt(q, k, v, f_col, batch, seq, heads, tq=512):
    tq = min(tq, seq)
    nq = seq // tq
    return pl.pallas_call(
        functools.partial(_fox_prompt_kernel, tq=tq, nq=nq, scale=HEAD_DIM ** -0.5),
        grid=(batch, heads, nq),
        in_specs=[pl.BlockSpec((tq, HEAD_DIM), lambda b, h, i: (b * nq + i, h)),
                  pl.BlockSpec((seq, HEAD_DIM), lambda b, h, i: (b, h)),
                  pl.BlockSpec((seq, HEAD_DIM), lambda b, h, i: (b, h)),
                  pl.BlockSpec((tq, LANES), lambda b, h, i: (b * nq + i, 0)),
                  pl.BlockSpec((seq, LANES), lambda b, h, i: (b, 0))],
        out_specs=pl.BlockSpec((tq, HEAD_DIM), lambda b, h, i: (b * nq + i, h)),
        out_shape=jax.ShapeDtypeStruct(q.shape, BF16),
        scratch_shapes=[pltpu.VMEM((seq, 2 * HEAD_DIM), BF16), pltpu.VMEM((seq, HEAD_DIM), BF16)],
        compiler_params=_params("arbitrary", "arbitrary", "arbitrary"),
        name="fox_prompt",
    )(q, k, v, f_col, f_col)


def _fox_sample_kernel(pt_ref, q_ref, *refs, n_steps, pps, heads, scale):
    del pt_ref
    ck_refs, cv_refs, lf_refs = refs[:pps], refs[pps:2 * pps], refs[2 * pps:3 * pps]
    kn_ref, vn_ref, fn_ref, fs_ref, o_ref, m_ref, l_ref, acc_ref, tail_ref = refs[3 * pps:]
    p = pl.program_id(1)

    @pl.when(p == 0)
    def _():
        m_ref[...] = jnp.full(m_ref.shape, NEG_INF, F32)
        l_ref[...] = jnp.zeros_like(l_ref)
        acc_ref[...] = jnp.zeros_like(acc_ref)
        tail_ref[...] = jnp.zeros_like(tail_ref)

    q = q_ref[...]
    rows = q.shape[0]
    fn = fn_ref[...]
    row = lax.broadcasted_iota(jnp.int32, (rows, LANES), 0)
    lane = lax.broadcasted_iota(jnp.int32, (rows, LANES), 1)
    own_head = (lane & (heads - 1)) == (row & (heads - 1))

    log2e = 1.4426950408889634

    def attend(keys, vals, lane_bias, visible):
        s = lax.dot_general(q, keys, (((1,), (1,)), ((), ())), preferred_element_type=F32)
        bias2 = lane_bias * log2e
        fn2 = fn * log2e
        parts = []
        for g in range(keys.shape[0] // LANES):
            sg = s[:, g * LANES:(g + 1) * LANES] * (scale * log2e) + (fn2 + bias2[g:g + 1, :])
            parts.append(jnp.where(visible, sg, NEG_INF))
        s = jnp.concatenate(parts, axis=1)
        m_old = m_ref[...]
        m_new = jnp.maximum(m_old, jnp.max(s, axis=1, keepdims=True))
        alpha = jnp.exp2(m_old - m_new)
        pr = jnp.exp2(s - m_new)
        l_ref[...] = alpha * l_ref[...] + jnp.sum(pr, axis=1, keepdims=True)
        acc_ref[...] = alpha * acc_ref[...] + jnp.dot(pr.astype(BF16), vals,
                                                      preferred_element_type=F32)
        m_ref[...] = m_new

    a = lax.broadcasted_iota(jnp.int32, (LANES, LANES), 0)
    b = lax.broadcasted_iota(jnp.int32, (LANES, LANES), 1)
    same_head = (a & (heads - 1)) == (b & (heads - 1))
    later_in_row = jnp.where(same_head & (a > b), 1.0, 0.0).astype(BF16)
    whole_row = jnp.where(same_head, 1.0, 0.0).astype(BF16)
    sums = jnp.concatenate([later_in_row, whole_row], axis=1)
    n_rows = lf_refs[0].shape[0]
    ra = lax.broadcasted_iota(jnp.int32, (n_rows, n_rows), 0)
    rb = lax.broadcasted_iota(jnp.int32, (n_rows, n_rows), 1)
    later_rows = jnp.where(rb > ra, 1.0, 0.0).astype(BF16)

    tail = tail_ref[...]
    biases = []
    for c in range(pps):
        lf = lf_refs[c][...]
        y = jnp.dot(jnp.concatenate(_split3(lf), axis=0), sums, preferred_element_type=F32)
        y = y[:n_rows] + y[n_rows:2 * n_rows] + y[2 * n_rows:]
        within, row_tot = y[:, :LANES], y[:, LANES:]
        z = jnp.dot(later_rows, jnp.concatenate(_split3(row_tot), axis=1),
                    preferred_element_type=F32)
        biases.append(within + z[:, :LANES] + z[:, LANES:2 * LANES] + z[:, 2 * LANES:] + tail)
        tail = tail + jnp.sum(row_tot, axis=0, keepdims=True)
    tail_ref[...] = tail

    attend(jnp.concatenate([r[...].astype(BF16) for r in ck_refs], axis=0),
           jnp.concatenate([r[...].astype(BF16) for r in cv_refs], axis=0),
           jnp.concatenate(biases, axis=0), own_head)

    @pl.when(p == n_steps - 1)
    def _():
        attend(kn_ref[...], vn_ref[...], -fs_ref[...],
               own_head & (lane - (lane & (heads - 1)) <= row - (row & (heads - 1))))
        o_ref[...] = acc_ref[...] / l_ref[...]


def _fox_sample(page_table, q_rows, cache_k, cache_v, cache_lf, k_new, v_new, fn_col, fs_lane,
                heads):
    nb, n_pages = page_table.shape
    rows, hd = q_rows.shape[1], q_rows.shape[2]
    lf_rows = cache_lf.shape[1]
    page_rows = lf_rows * LANES
    assert heads & (heads - 1) == 0 and LANES % heads == 0
    pps = math.gcd(n_pages, FOX_PAGES_PER_STEP)
    n_steps = n_pages // pps
    per_b = lambda b, p, pt: (b, 0, 0)

    def page_of(c):
        return lambda b, p, pt: pt[b, n_pages - 1 - (p * pps + c)]

    kv_specs = [pl.BlockSpec((page_rows, hd), lambda b, p, pt, f=page_of(c): (f(b, p, pt), 0))
                for c in range(pps)]
    lf_specs = [pl.BlockSpec((None, lf_rows, LANES),
                             lambda b, p, pt, f=page_of(c): (f(b, p, pt), 0, 0))
                for c in range(pps)]
    grid_spec = pltpu.PrefetchScalarGridSpec(
        num_scalar_prefetch=1,
        grid=(nb, n_steps),
        in_specs=([pl.BlockSpec((None, rows, hd), per_b)] + kv_specs + kv_specs + lf_specs
                  + [pl.BlockSpec((None, LANES, hd), per_b),
                     pl.BlockSpec((None, LANES, hd), per_b),
                     pl.BlockSpec((None, rows, 1), per_b),
                     pl.BlockSpec((None, 1, LANES), per_b)]),
        out_specs=pl.BlockSpec((None, rows, hd), per_b),
        scratch_shapes=[pltpu.VMEM((rows, 1), F32), pltpu.VMEM((rows, 1), F32),
                        pltpu.VMEM((rows, hd), F32), pltpu.VMEM((1, LANES), F32)],
    )
    return pl.pallas_call(
        functools.partial(_fox_sample_kernel, n_steps=n_steps, pps=pps, heads=heads,
                          scale=HEAD_DIM ** -0.5),
        grid_spec=grid_spec,
        out_shape=jax.ShapeDtypeStruct((nb, rows, hd), F32),
        compiler_params=_params("arbitrary", "arbitrary"),
        name="fox_sample",
    )(page_table, q_rows, *([cache_k] * pps), *([cache_v] * pps), *([cache_lf] * pps),
      k_new, v_new, fn_col, fs_lane)


def _sgu_kernel(u_ref, vg_ref, nw_ref, w_ref, bt_ref, *out_refs, groups):
    o_ref = out_refs[0]
    g = vg_ref[...].astype(F32)
    v = g * lax.rsqrt(jnp.mean(g * g, axis=-1, keepdims=True) + EPS) * nw_ref[...]
    if len(out_refs) > 1:
        out_refs[1][...] = v
    vb = v.astype(BF16)
    rows = v.shape[0]
    cw = v.shape[1] // groups
    r = lax.broadcasted_iota(jnp.int32, (rows, rows), 0)
    c = lax.broadcasted_iota(jnp.int32, (rows, rows), 1)
    bt = bt_ref[...]
    for gi in range(groups):
        wm = jnp.where(c <= r, w_ref[gi], 0.0).astype(BF16)
        z = jnp.dot(wm, vb[:, gi * cw:(gi + 1) * cw], preferred_element_type=F32)
        z = z + bt[:, gi:gi + 1]
        u = u_ref[:, gi * cw:(gi + 1) * cw].astype(F32)
        o_ref[:, gi * cw:(gi + 1) * cw] = (u * z).astype(o_ref.dtype)


def _sgu(uv, norm_w, w_pos, bias_t, rows, emit_v):
    m, two_w = uv.shape
    width = two_w // 2
    groups = w_pos.shape[0]
    out_shape = [jax.ShapeDtypeStruct((m, width), BF16)]
    out_specs = [pl.BlockSpec((rows, width), lambda i: (i, 0))]
    if emit_v:
        out_shape.append(jax.ShapeDtypeStruct((m, width), F32))
        out_specs.append(pl.BlockSpec((rows, width), lambda i: (i, 0)))
    return pl.pallas_call(
        functools.partial(_sgu_kernel, groups=groups),
        grid=(m // rows,),
        in_specs=[pl.BlockSpec((rows, width), lambda i: (i, 0)),
                  pl.BlockSpec((rows, width), lambda i: (i, 1)),
                  pl.BlockSpec((1, width), lambda i: (0, 0)),
                  pl.BlockSpec((groups, rows, rows), lambda i: (0, 0, 0)),
                  pl.BlockSpec((rows, groups), lambda i: (0, 0))],
        out_specs=out_specs,
        out_shape=out_shape,
        compiler_params=_params("arbitrary"),
        name="sgu",
    )(uv, uv, norm_w, w_pos, bias_t)


def _conv_sample_kernel(z0_ref, z1_ref, z2_ref, gb_ref, w_ref, o_ref):
    w = w_ref[...]
    y = w[0:1] * z0_ref[...] + w[1:2] * z1_ref[...] + w[2:3] * z2_ref[...]
    o_ref[...] = (gb_ref[...].astype(F32) * y).astype(o_ref.dtype)


def _conv_sample(z0, z1, z2, gb, conv_w):
    return pl.pallas_call(
        _conv_sample_kernel,
        out_shape=jax.ShapeDtypeStruct(z0.shape, BF16),
        name="conv_sample",
    )(z0, z1, z2, gb, conv_w)


def _gla_kernel(q_ref, k_ref, v_ref, la_ref, gate_ref, nw_ref, s0_ref, o_ref, st_ref, st, *, nc):
    ci = pl.program_id(2)

    @pl.when(ci == 0)
    def _():
        st[...] = s0_ref[...]

    la = la_ref[...]
    c, dk = la.shape
    r = lax.broadcasted_iota(jnp.int32, (c, c), 0)
    cc = lax.broadcasted_iota(jnp.int32, (c, c), 1)
    tril = cc <= r
    y = jnp.dot(jnp.where(tril, 1.0, 0.0).astype(BF16),
                jnp.concatenate(_split3(la), axis=1), preferred_element_type=F32)
    bc = y[:, :dk] + y[:, dk:2 * dk] + y[:, 2 * dk:]
    q = q_ref[...].astype(F32) * dk ** -0.5
    k = k_ref[...].astype(F32)
    v = v_ref[...]
    qt = (q * jnp.exp(bc)).astype(BF16)
    mid = bc[c // 2 - 1:c // 2, :]
    qa = (q * jnp.exp(bc - mid)).astype(BF16)
    ka = (k * jnp.exp(mid - bc)).astype(BF16)
    att = lax.dot_general(qa, ka, (((1,), (1,)), ((), ())), preferred_element_type=F32)
    att = jnp.where(tril, att, 0.0)
    s_t = st[...]
    o = (lax.dot_general(qt, s_t.astype(BF16), (((1,), (1,)), ((), ())),
                         preferred_element_type=F32)
         + jnp.dot(att.astype(BF16), v, preferred_element_type=F32))
    bl = bc[c - 1:c, :]
    kd = (k * jnp.exp(bl - bc)).astype(BF16)
    st[...] = s_t * jnp.exp(bl) + lax.dot_general(v, kd, (((0,), (0,)), ((), ())),
                                                   preferred_element_type=F32)
    on = o * lax.rsqrt(jnp.mean(o * o, axis=-1, keepdims=True) + EPS) * nw_ref[...]
    o_ref[...] = (on * gate_ref[...].astype(F32)).astype(o_ref.dtype)

    @pl.when(ci == nc - 1)
    def _():
        st_ref[...] = st[...]


def _gla(qkv, log_a, gate, norm_w, s0_t, batch, seq, heads, chunk):
    m = qkv.shape[0]
    dk = log_a.shape[1] // heads
    dv = gate.shape[1] // heads
    assert (2 * heads * dk) % dv == 0
    v0 = (2 * heads * dk) // dv
    chunk = min(chunk, seq)
    nc = seq // chunk
    row = lambda b, h, c: b * nc + c
    return pl.pallas_call(
        functools.partial(_gla_kernel, nc=nc),
        grid=(batch, heads, nc),
        in_specs=[pl.BlockSpec((chunk, dk), lambda b, h, c: (row(b, h, c), h)),
                  pl.BlockSpec((chunk, dk), lambda b, h, c: (row(b, h, c), heads + h)),
                  pl.BlockSpec((chunk, dv), lambda b, h, c: (row(b, h, c), v0 + h)),
                  pl.BlockSpec((chunk, dk), lambda b, h, c: (row(b, h, c), h)),
                  pl.BlockSpec((chunk, dv), lambda b, h, c: (row(b, h, c), h)),
                  pl.BlockSpec((1, dv), lambda b, h, c: (0, h)),
                  pl.BlockSpec((None, None, dv, dk), lambda b, h, c: (b, h, 0, 0))],
        out_specs=[pl.BlockSpec((chunk, dv), lambda b, h, c: (row(b, h, c), h)),
                   pl.BlockSpec((None, None, dv, dk), lambda b, h, c: (b, h, 0, 0))],
        out_shape=[jax.ShapeDtypeStruct((m, heads * dv), BF16),
                   jax.ShapeDtypeStruct((batch, heads, dv, dk), F32)],
        scratch_shapes=[pltpu.VMEM((dv, dk), F32)],
        compiler_params=_params("arbitrary", "arbitrary", "arbitrary"),
        name="gla",
    )(qkv, qkv, qkv, log_a, gate, norm_w, s0_t)


def kernel(x_prompt, x_sample, cache_k, cache_v, cache_logf, state_conv, state_gla, page_table,
           c_prompt, c_sample, w_in0, f_bias, q_norm, k_norm, sgu_norm, sgu_w, sgu_b, w_out0,
           w_in1, conv_w, gla_a_w2, gla_a_b, gla_norm, w_out1, ada_w, ada_b, mlp_w1, mlp_w2):
    nbp, seq, d = x_prompt.shape
    nbs, dseq, _ = x_sample.shape
    mp, ms = nbp * seq, nbs * dseq
    fox_w = d // 2
    heads = fox_w // HEAD_DIM
    sgu_width = d // 2
    groups = sgu_w.shape[1]
    conv_width = state_conv.shape[-1]
    gla_heads, gla_dk, gla_dv = state_gla.shape[2], state_gla.shape[3], state_gla.shape[4]
    gla_key, gla_val = gla_heads * gla_dk, gla_heads * gla_dv
    gla_rank = gla_a_w2.shape[1]
    n_pool, page = cache_k.shape[1], cache_k.shape[2]
    n_pages = page_table.shape[1]

    r_c = nbp + nbs
    r_pad = -(-r_c // 16) * 16
    c_rows = jnp.pad(jnp.concatenate([c_prompt, c_sample], axis=0),
                     ((0, r_pad - r_c), (0, 0))).astype(BF16)
    mod = _ada(c_rows, ada_w, ada_b)

    def mods(layer):
        parts = jnp.split(mod[layer], 6, axis=-1)
        pp = [p[:nbp].reshape(nbp, 1, d) for p in parts]
        ps = [jnp.repeat(p[nbp:r_c], dseq, axis=0) for p in parts]
        return pp, ps

    xp = x_prompt.reshape(mp, d)
    xs = x_sample.reshape(ms, d)

    def modulate_both(xp, xs, shp, scp, shs, scs):
        hp = _modulate(xp, shp, scp, seq)
        hs = _modulate(xs, shs.reshape(1, ms, d), scs.reshape(1, ms, d), ms, tr=ms)
        return hp, hs

    def mlp(xp, xs, layer, shp, scp, gp, shs, scs, gs):
        hp, hs = modulate_both(xp, xs, shp, scp, shs, scs)
        (ap,), (as_,) = _proj([hp], [hs], [(mlp_w1, layer, 0)], mlp_w1.shape[2], _epi_relu2,
                              [BF16], name="mlp_up")
        return _down(ap, as_, mlp_w2, layer, xp, gp, xs, gs, seq)

    def residual_proj(lhs_p, lhs_s, w, xp, xs, gp, gs, name):
        (yp,), (ys,) = _proj(lhs_p, lhs_s, [(w, 0, 0)], d, _epi_residual, [F32],
                             ext=[("res_p", "prow", xp), ("gate_p", "pgrp", gp),
                                  ("res_s", "srow", xs), ("gate_s", "srow", gs)],
                             rows_per_group=seq, name=name)
        return yp, ys

    (sh1p, sc1p, g1p, sh2p, sc2p, g2p), (sh1s, sc1s, g1s, sh2s, sc2s, g2s) = mods(0)
    hp, hs = modulate_both(xp, xs, sh1p, sc1p, sh1s, sc1s)
    w0 = jnp.swapaxes(w_in0, 1, 2)
    qn = jnp.tile(q_norm[0], heads).reshape(1, fox_w)
    kn = jnp.tile(k_norm[0], heads).reshape(1, fox_w)
    (q_p,), (q_s,) = _proj([hp], [hs], [(w0, 0, 0)], fox_w, _epi_head_norm, [BF16],
                           ext=[("norm", "col", qn)], w_rows=True, name="in0_q")
    (k_p,), (k_s,) = _proj([hp], [hs], [(w0, 0, fox_w)], fox_w, _epi_head_norm, [F32],
                           ext=[("norm", "col", kn)], w_rows=True, name="in0_k")
    (v_p,), (v_s,) = _proj([hp], [hs], [(w0, 0, 2 * fox_w)], fox_w, _epi_plain, [F32],
                           w_rows=True, name="in0_v")
    fb = jnp.pad(f_bias[0], (0, LANES - heads)).reshape(1, LANES)
    (lf_p,), (lf_s,) = _proj([hp], [hs], [(w0, 0, 3 * fox_w)], LANES, _epi_log_forget, [F32],
                             ext=[("bias", "col", fb)], w_rows=True, name="in0_logf")
    (uv_p,), (uv_s,) = _proj([hp], [hs], [(w0, 0, 3 * fox_w + heads)], 2 * sgu_width, _epi_gelu,
                             [BF16], w_rows=True, name="in0_uv")

    f_col = _cumsum_rows(lf_p, nbp)
    oa_p = _fox_prompt(q_p, k_p, v_p, f_col, nbp, seq, heads)

    lf_s16 = lf_s[:, :heads].reshape(nbs, dseq, heads)
    tpad = 8
    lf_s_pad = jnp.pad(lf_s.reshape(nbs, dseq, LANES), ((0, 0), (0, tpad - dseq), (0, 0)))
    fn = _cumsum_rows(lf_s_pad.reshape(nbs * tpad, LANES), nbs).reshape(
        nbs, tpad, LANES)[:, :dseq, :heads]
    rows_s = dseq * heads
    assert rows_s <= LANES

    def new_rows(a):
        a = a.reshape(nbs, rows_s, HEAD_DIM)
        return jnp.pad(a, ((0, 0), (0, LANES - rows_s), (0, 0))).astype(BF16)

    fs_lane = jnp.pad(fn.reshape(nbs, 1, rows_s), ((0, 0), (0, 0), (0, LANES - rows_s)))
    oa_rows = _fox_sample(page_table, q_s.reshape(nbs, rows_s, HEAD_DIM),
                          cache_k[0].reshape(n_pool * page * heads, HEAD_DIM),
                          cache_v[0].reshape(n_pool * page * heads, HEAD_DIM),
                          cache_logf[0].reshape(n_pool, page * heads // LANES, LANES),
                          new_rows(k_s), new_rows(v_s), fn.reshape(nbs, rows_s, 1), fs_lane, heads)
    oa_s = oa_rows.reshape(ms, fox_w).astype(BF16)

    sn = sgu_norm[0].reshape(1, sgu_width)
    rows_p = min(seq, SGU_CHUNK)
    (ob_p,) = _sgu(uv_p, sn, sgu_w[0][:, :rows_p, :rows_p], sgu_b[0][:, :rows_p].T, rows_p, False)
    rows_g = min(dseq, SGU_CHUNK)
    w_small = sgu_w[0][:, :rows_g, :rows_g]
    w_big = jnp.einsum("ab,gts->gatbs", jnp.eye(nbs, dtype=F32), w_small).reshape(groups, ms, ms)
    b_big = jnp.tile(sgu_b[0][:, :rows_g].T, (nbs, 1))
    ob_s, sgu_v = _sgu(uv_s, sn, w_big, b_big, ms, True)

    xp, xs = residual_proj([oa_p, ob_p], [oa_s, ob_s], w_out0, xp, xs, g1p, g1s, "out0")
    xp, xs = mlp(xp, xs, 0, sh2p, sc2p, g2p, sh2s, sc2s, g2s)

    (sh1p, sc1p, g1p, sh2p, sc2p, g2p), (sh1s, sc1s, g1s, sh2s, sc2s, g2s) = mods(1)
    hp, hs = modulate_both(xp, xs, sh1p, sc1p, sh1s, sc1s)
    w1 = jnp.swapaxes(w_in1, 1, 2)
    cw = conv_width
    cwt = conv_w[0]
    n_tap = cwt.shape[0]
    (oc_p, ztail_p), (gb_s, z_s) = _proj(
        [hp], [hs], [(w1, 0, 0), (w1, 0, cw), (w1, 0, 2 * cw)], cw, _make_epi_conv(seq),
        [BF16, F32], ext=[("taps", "col", cwt)], tm=512, tn=256, rows_per_group=seq, w_rows=True,
        tail_rows=(1,), scratch=lambda tm, tn: [pltpu.VMEM((tm + 8, tn), F32)], name="in1_conv")
    conv_p = ztail_p.reshape(nbp, -1, 8, cw)[:, -1, 8 - (n_tap - 1):, :]
    (qkv_p,), (qkv_s,) = _proj([hp], [hs], [(w1, 0, 3 * cw)], 2 * gla_key + gla_val, _epi_plain,
                               [BF16], w_rows=True, name="in1_qkv")
    c_ga = 3 * cw + 2 * gla_key + gla_val
    (ga_p,), (ga_s,) = _proj([hp], [hs], [(w1, 0, c_ga)], LANES, _epi_plain, [BF16],
                             w_rows=True, name="in1_ga")
    w_a2 = jnp.pad(gla_a_w2, ((0, 0), (0, LANES - gla_rank), (0, 0)))
    (la_p,), (la_s,) = _proj([ga_p], [ga_s], [(w_a2, 0, 0)], gla_key, _epi_log_decay, [F32],
                             ext=[("bias", "col", gla_a_b[0].reshape(1, gla_key))], name="in1_log_a")
    (gg_p,), (gg_s,) = _proj([hp], [hs], [(w1, 0, c_ga + gla_rank)], gla_val, _epi_silu, [BF16],
                             w_rows=True, name="in1_gate")

    zp_s = jnp.concatenate([state_conv[0], z_s.reshape(nbs, dseq, cw)], axis=1)
    shifted = [zp_s[:, i:i + dseq].reshape(ms, cw) for i in range(cwt.shape[0])]
    oc_s = _conv_sample(shifted[0], shifted[1], shifted[2], gb_s, cwt)
    conv_s = zp_s[:, dseq:, :]

    gn = gla_norm[0].reshape(1, gla_val)
    s0_p = jnp.zeros((nbp, gla_heads, gla_dv, gla_dk), F32)
    od_p, st_p = _gla(qkv_p, la_p, gg_p, gn, s0_p, nbp, seq, gla_heads, GLA_CHUNK_ROWS)
    pad_t = 16

    def pad_rows(a):
        return jnp.pad(a.reshape(nbs, dseq, -1), ((0, 0), (0, pad_t - dseq), (0, 0))).reshape(
            nbs * pad_t, -1)

    od_s_pad, st_s = _gla(pad_rows(qkv_s), pad_rows(la_s), pad_rows(gg_s), gn,
                          state_gla[0].swapaxes(-1, -2), nbs, pad_t, gla_heads, pad_t)
    od_s = od_s_pad.reshape(nbs, pad_t, gla_val)[:, :dseq].reshape(ms, gla_val)

    xp, xs = residual_proj([oc_p, od_p], [oc_s, od_s], w_out1, xp, xs, g1p, g1s, "out1")
    xp, xs = mlp(xp, xs, 1, sh2p, sc2p, g2p, sh2s, sc2s, g2s)

    y_prompt = xp.reshape(nbp, seq, d)
    y_sample = xs.reshape(nbs, dseq, d)
    return (y_prompt, y_sample,
            _to_heads(k_p, heads).reshape(1, nbp, seq, heads, HEAD_DIM),
            _to_heads(v_p, heads).reshape(1, nbp, seq, heads, HEAD_DIM),
            lf_p[:, :heads].reshape(1, nbp, seq, heads),
            k_s.reshape(1, nbs, dseq, heads, HEAD_DIM), v_s.reshape(1, nbs, dseq, heads, HEAD_DIM),
            lf_s16[None], sgu_v.reshape(1, nbs, dseq, sgu_width),
            conv_p[None], conv_s[None],
            st_p.swapaxes(-1, -2)[None], st_s.swapaxes(-1, -2)[None])
```

```python
import functools
import math

import jax
import jax.numpy as jnp
from jax import lax
from jax.experimental import pallas as pl
from jax.experimental.pallas import tpu as pltpu

F32 = jnp.float32
BF16 = jnp.bfloat16
EPS = 1e-6
NEG_INF = -1e30
HEAD_DIM = 128
LANES = 128
SGU_CHUNK = 128
GLA_TAU = 16.0
GLA_CHUNK_ROWS = 128
FOX_PAGES_PER_STEP = 4
VMEM_LIMIT_BYTES = 56 * 1024 * 1024


def _params(*semantics):
    return pltpu.CompilerParams(dimension_semantics=semantics,
                                vmem_limit_bytes=VMEM_LIMIT_BYTES)


def _split3(x):
    hi = x.astype(BF16)
    r = x - hi.astype(F32)
    mid = r.astype(BF16)
    lo = (r - mid.astype(F32)).astype(BF16)
    return hi, mid, lo


def _log_sigmoid(x):
    return jnp.minimum(x, 0.0) - jnp.log1p(jnp.exp(-jnp.abs(x)))


def _gelu_tanh(x):
    c = 0.7978845608028654
    return 0.5 * x * (1.0 + jnp.tanh(c * (x + 0.044715 * (x * x * x))))


def _ada_kernel(c_ref, w_ref, b_ref, o_ref):
    o_ref[...] = jnp.dot(c_ref[...], w_ref[...].astype(BF16),
                         preferred_element_type=F32) + b_ref[...]


def _ada(c_rows, ada_w, ada_b, tn=512):
    n_layers, d, n = ada_w.shape
    r = c_rows.shape[0]
    tn = min(tn, n)
    return pl.pallas_call(
        _ada_kernel,
        grid=(n_layers, n // tn),
        in_specs=[pl.BlockSpec((r, d), lambda l, j: (0, 0)),
                  pl.BlockSpec((None, d, tn), lambda l, j: (l, 0, j)),
                  pl.BlockSpec((None, 1, tn), lambda l, j: (l, 0, j))],
        out_specs=pl.BlockSpec((None, r, tn), lambda l, j: (l, 0, j)),
        out_shape=jax.ShapeDtypeStruct((n_layers, r, n), F32),
        compiler_params=_params("arbitrary", "arbitrary"),
        name="ada",
    )(c_rows, ada_w, ada_b.reshape(n_layers, 1, n))


def _modulate_kernel(x_ref, sh_ref, sc_ref, o_ref):
    x = x_ref[...]
    y = x * lax.rsqrt(jnp.mean(x * x, axis=-1, keepdims=True) + EPS)
    o_ref[...] = (y * (1.0 + sc_ref[...]) + sh_ref[...]).astype(o_ref.dtype)


def _modulate(x, sh, sc, rows_per_group, tr=256):
    rows, d = x.shape
    tr = min(tr, rows)
    r = sh.shape[1]
    grp = lambda i: ((i * tr) // rows_per_group, 0, 0)
    return pl.pallas_call(
        _modulate_kernel,
        grid=(rows // tr,),
        in_specs=[pl.BlockSpec((tr, d), lambda i: (i, 0)),
                  pl.BlockSpec((None, r, d), grp),
                  pl.BlockSpec((None, r, d), grp)],
        out_specs=pl.BlockSpec((tr, d), lambda i: (i, 0)),
        out_shape=jax.ShapeDtypeStruct((rows, d), BF16),
        compiler_params=_params("arbitrary"),
        name="modulate",
    )(x, sh, sc)


def _proj_kernel(*refs, k_sizes, n_w, ext_names, n_out, epilogue, cast_rows, w_rows, n_scratch):
    n_lhs = len(k_sizes)
    pos = 0
    xp = refs[pos:pos + n_lhs]; pos += n_lhs
    xs = refs[pos:pos + n_lhs]; pos += n_lhs
    w = refs[pos:pos + n_w]; pos += n_w
    ext = dict(zip(ext_names, refs[pos:pos + len(ext_names)])); pos += len(ext_names)
    out_p = refs[pos:pos + n_out]; pos += n_out
    out_s = refs[pos:pos + n_out]; pos += n_out
    wbf = refs[pos:pos + n_w]; pos += n_w
    ext["scratch"] = refs[pos:pos + n_scratch]

    def accumulate(lhs):
        accs = []
        for m in range(n_w):
            acc = None
            off = 0
            for a, ka in enumerate(k_sizes):
                part = jnp.dot(lhs[a][...], wbf[m][off:off + ka, :],
                               preferred_element_type=F32)
                acc = part if acc is None else acc + part
                off += ka
            accs.append(acc)
        return accs

    def store(out_refs, vals):
        for o_ref, val in zip(out_refs, vals):
            o_ref[...] = val.astype(o_ref.dtype)

    @pl.when(pl.program_id(1) == 0)
    def _():
        accs_p, accs_s = [None] * n_w, [None] * n_w
        r0 = 0
        for a, ka in enumerate(k_sizes):
            for loc in range(0, ka, cast_rows):
                for m in range(n_w):
                    if w_rows:
                        chunk = w[m][:, r0:r0 + cast_rows].T.astype(BF16)
                    else:
                        chunk = w[m][r0:r0 + cast_rows, :].astype(BF16)
                    wbf[m][r0:r0 + cast_rows, :] = chunk
                    pp = jnp.dot(xp[a][:, loc:loc + cast_rows], chunk, preferred_element_type=F32)
                    ps = jnp.dot(xs[a][:, loc:loc + cast_rows], chunk, preferred_element_type=F32)
                    accs_p[m] = pp if accs_p[m] is None else accs_p[m] + pp
                    accs_s[m] = ps if accs_s[m] is None else accs_s[m] + ps
                r0 += cast_rows
        store(out_s, epilogue(accs_s, ext, True))
        store(out_p, epilogue(accs_p, ext, False))

    @pl.when(pl.program_id(1) != 0)
    def _():
        store(out_p, epilogue(accumulate(xp), ext, False))


def _proj(xp, xs, weights, n_cols, epilogue, out_dtypes, ext=(), *, tm=1024, tn=512,
          rows_per_group=None, w_rows=False, tail_rows=(), scratch=None, name="proj"):
    mp, ms = xp[0].shape[0], xs[0].shape[0]
    k_sizes = tuple(int(a.shape[1]) for a in xp)
    k_total = sum(k_sizes)
    tm = min(tm, mp, rows_per_group or mp)
    tn = min(tn, n_cols)
    assert mp % tm == 0 and n_cols % tn == 0
    for warr, _, c0 in weights:
        assert warr.ndim == 3
        if w_rows:
            assert warr.shape[2] == k_total and c0 % 8 == 0 and len(k_sizes) == 1
        else:
            assert warr.shape[1] == k_total and c0 % tn == 0
    in_specs, args = [], []
    for a in xp:
        in_specs.append(pl.BlockSpec((tm, a.shape[1]), lambda j, i: (i, 0)))
        args.append(a)
    for a in xs:
        in_specs.append(pl.BlockSpec((ms, a.shape[1]), lambda j, i: (0, 0)))
        args.append(a)
    for warr, layer, c0 in weights:
        if w_rows:
            spec = pl.BlockSpec((None, pl.Element(tn), pl.Element(k_total)),
                                lambda j, i, c=c0, l=layer: (l, pl.multiple_of(c + j * tn, 8), 0))
        else:
            spec = pl.BlockSpec((None, k_total, tn),
                                lambda j, i, cb=c0 // tn, l=layer: (l, 0, j + cb))
        in_specs.append(spec)
        args.append(warr)
    ext_names = []
    for ename, kind, arr in ext:
        ext_names.append(ename)
        if kind == "col":
            spec = pl.BlockSpec((arr.shape[0], tn), lambda j, i: (0, j))
        elif kind == "prow":
            spec = pl.BlockSpec((tm, tn), lambda j, i: (i, j))
        elif kind == "pgrp":
            spec = pl.BlockSpec((None, 1, tn),
                                lambda j, i: ((i * tm) // rows_per_group, 0, j))
        elif kind == "srow":
            spec = pl.BlockSpec((ms, tn), lambda j, i: (0, j))
        else:
            raise ValueError(kind)
        in_specs.append(spec)
        args.append(arr)
    n_out = len(out_dtypes)
    rows_p = [(8 * (mp // tm), 8) if o in tail_rows else (mp, tm) for o in range(n_out)]
    out_specs = ([pl.BlockSpec((br, tn), lambda j, i: (i, j)) for _, br in rows_p]
                 + [pl.BlockSpec((ms, tn), lambda j, i: (0, j))] * n_out)
    out_shape = ([jax.ShapeDtypeStruct((r, n_cols), dt) for (r, _), dt in zip(rows_p, out_dtypes)]
                 + [jax.ShapeDtypeStruct((ms, n_cols), dt) for dt in out_dtypes])
    cast_rows = min(512, *k_sizes)
    assert all(ka % cast_rows == 0 for ka in k_sizes)
    extra_scratch = scratch(tm, tn) if scratch else []
    outs = pl.pallas_call(
        functools.partial(_proj_kernel, k_sizes=k_sizes, n_w=len(weights),
                          ext_names=tuple(ext_names), n_out=n_out, epilogue=epilogue,
                          cast_rows=cast_rows, w_rows=w_rows, n_scratch=len(extra_scratch)),
        grid=(n_cols // tn, mp // tm),
        in_specs=in_specs,
        out_specs=out_specs,
        out_shape=out_shape,
        scratch_shapes=[pltpu.VMEM((k_total, tn), BF16) for _ in weights] + extra_scratch,
        compiler_params=_params("arbitrary", "arbitrary"),
        name=name,
    )(*args)
    return outs[:n_out], outs[n_out:]


def _epi_plain(accs, ext, is_sample):
    return (accs[0],)


def _epi_head_norm(accs, ext, is_sample):
    acc = accs[0]
    wn = ext["norm"][...]
    parts = []
    for c0 in range(0, acc.shape[1], HEAD_DIM):
        a = acc[:, c0:c0 + HEAD_DIM]
        y = a * lax.rsqrt(jnp.mean(a * a, axis=-1, keepdims=True) + EPS)
        parts.append(y * wn[:, c0:c0 + HEAD_DIM])
    return (jnp.concatenate(parts, axis=1),)


def _epi_log_forget(accs, ext, is_sample):
    return (_log_sigmoid(accs[0] + ext["bias"][...]),)


def _epi_gelu(accs, ext, is_sample):
    return (_gelu_tanh(accs[0]),)


def _epi_silu(accs, ext, is_sample):
    a = accs[0]
    return (a * jax.nn.sigmoid(a),)


def _epi_relu2(accs, ext, is_sample):
    r = jnp.maximum(accs[0], 0.0)
    return (r * r,)


def _make_epi_conv(seq):
    def epilogue(accs, ext, is_sample):
        z = accs[1] * accs[0]
        if is_sample:
            return accs[2], z
        (zbuf,) = ext["scratch"]
        tm = z.shape[0]

        @pl.when(pl.program_id(1) % (seq // tm) == 0)
        def _():
            zbuf[0:8, :] = jnp.zeros((8, zbuf.shape[1]), F32)

        zbuf[8:8 + tm, :] = z
        w = ext["taps"][...]
        y = w[0:1] * zbuf[6:6 + tm, :] + w[1:2] * zbuf[7:7 + tm, :] + w[2:3] * z
        tail = z[tm - 8:tm, :]
        zbuf[0:8, :] = tail
        return accs[2] * y, tail

    return epilogue


def _epi_log_decay(accs, ext, is_sample):
    return (_log_sigmoid(accs[0] + ext["bias"][...]) / GLA_TAU,)


def _epi_residual(accs, ext, is_sample):
    if is_sample:
        return (ext["res_s"][...] + ext["gate_s"][...] * accs[0],)
    return (ext["res_p"][...] + ext["gate_p"][...] * accs[0],)


def _down_kernel(xp_ref, xs_ref, w_ref, resp_ref, gp_ref, ress_ref, gs_ref,
                 op_ref, os_ref, accs_ref, wbf, *, nk, cast_rows):
    i = pl.program_id(1)
    k = pl.program_id(2)

    @pl.when(k == 0)
    def _():
        op_ref[...] = jnp.zeros_like(op_ref)

    total = None
    for r0 in range(0, w_ref.shape[0], cast_rows):
        wbf[r0:r0 + cast_rows, :] = w_ref[r0:r0 + cast_rows, :].astype(BF16)
        part = jnp.dot(xp_ref[:, r0:r0 + cast_rows], wbf[r0:r0 + cast_rows, :],
                       preferred_element_type=F32)
        total = part if total is None else total + part
    op_ref[...] += total

    @pl.when(k == nk - 1)
    def _():
        op_ref[...] = resp_ref[...] + gp_ref[...] * op_ref[...]

    @pl.when(i == 0)
    def _():
        ps = jnp.dot(xs_ref[...], wbf[...], preferred_element_type=F32)

        @pl.when(k == 0)
        def _():
            accs_ref[...] = ps

        @pl.when(k > 0)
        def _():
            accs_ref[...] += ps

        @pl.when(k == nk - 1)
        def _():
            os_ref[...] = ress_ref[...] + gs_ref[...] * accs_ref[...]


def _down(xp, xs, w, layer, res_p, gate_p, res_s, gate_s, rows_per_group, *,
          tm=1024, tn=1024, tk=2048):
    mp, kdim = xp.shape
    ms = xs.shape[0]
    n = w.shape[2]
    tm, tn, tk = min(tm, rows_per_group, mp), min(tn, n), min(tk, kdim)
    nk = kdim // tk
    cast_rows = min(512, tk)
    assert tk % cast_rows == 0
    return pl.pallas_call(
        functools.partial(_down_kernel, nk=nk, cast_rows=cast_rows),
        grid=(n // tn, mp // tm, nk),
        in_specs=[pl.BlockSpec((tm, tk), lambda j, i, k: (i, k)),
                  pl.BlockSpec((ms, tk), lambda j, i, k: (0, k)),
                  pl.BlockSpec((None, tk, tn), lambda j, i, k: (layer, k, j)),
                  pl.BlockSpec((tm, tn), lambda j, i, k: (i, j)),
                  pl.BlockSpec((None, 1, tn),
                               lambda j, i, k: ((i * tm) // rows_per_group, 0, j)),
                  pl.BlockSpec((ms, tn), lambda j, i, k: (0, j)),
                  pl.BlockSpec((ms, tn), lambda j, i, k: (0, j))],
        out_specs=[pl.BlockSpec((tm, tn), lambda j, i, k: (i, j)),
                   pl.BlockSpec((ms, tn), lambda j, i, k: (0, j))],
        out_shape=[jax.ShapeDtypeStruct((mp, n), F32),
                   jax.ShapeDtypeStruct((ms, n), F32)],
        scratch_shapes=[pltpu.VMEM((ms, tn), F32), pltpu.VMEM((tk, tn), BF16)],
        compiler_params=_params("arbitrary", "arbitrary", "arbitrary"),
        name="mlp_down",
    )(xp, xs, w, res_p, gate_p, res_s, gate_s)


def _to_heads_kernel(x_ref, o_ref):
    for h in range(o_ref.shape[1]):
        o_ref[:, h, :] = x_ref[:, h * HEAD_DIM:(h + 1) * HEAD_DIM]


def _to_heads(x, heads, tr=256):
    m, w = x.shape
    tr = min(tr, m)
    return pl.pallas_call(
        _to_heads_kernel,
        grid=(m // tr,),
        in_specs=[pl.BlockSpec((tr, w), lambda i: (i, 0))],
        out_specs=pl.BlockSpec((tr, heads, HEAD_DIM), lambda i: (i, 0, 0)),
        out_shape=jax.ShapeDtypeStruct((m, heads, HEAD_DIM), x.dtype),
        compiler_params=_params("arbitrary"),
        name="to_heads",
    )(x)


def _cumsum_kernel(x_ref, o_ref, carry_ref):
    @pl.when(pl.program_id(1) == 0)
    def _():
        carry_ref[...] = jnp.zeros_like(carry_ref)

    x = x_ref[...]
    t, w = x.shape
    r = lax.broadcasted_iota(jnp.int32, (t, t), 0)
    c = lax.broadcasted_iota(jnp.int32, (t, t), 1)
    tri = jnp.where(c <= r, 1.0, 0.0).astype(BF16)
    y = jnp.dot(tri, jnp.concatenate(_split3(x), axis=1), preferred_element_type=F32)
    out = y[:, :w] + y[:, w:2 * w] + y[:, 2 * w:] + carry_ref[...]
    o_ref[...] = out
    carry_ref[...] = out[t - 1:t, :]


def _cumsum_rows(x, n_groups, tc=256):
    rows, w = x.shape
    per = rows // n_groups
    tc = min(tc, per)
    nt = per // tc
    return pl.pallas_call(
        _cumsum_kernel,
        grid=(n_groups, nt),
        in_specs=[pl.BlockSpec((tc, w), lambda b, t: (b * nt + t, 0))],
        out_specs=pl.BlockSpec((tc, w), lambda b, t: (b * nt + t, 0)),
        out_shape=jax.ShapeDtypeStruct((rows, w), F32),
        scratch_shapes=[pltpu.VMEM((1, w), F32)],
        compiler_params=_params("arbitrary", "arbitrary"),
        name="cumsum_logf",
    )(x)


def _bias_lanes(col, own_first):
    hi, mid, lo = _split3(col)
    lane = lax.broadcasted_iota(jnp.int32, (col.shape[0], LANES), 1)
    own, other = (0, 3) if own_first else (3, 0)
    x = jnp.where(lane == own, hi.astype(F32),
                  jnp.where(lane == own + 1, mid.astype(F32),
                            jnp.where(lane == own + 2, lo.astype(F32), 0.0)))
    x = jnp.where((lane >= other) & (lane < other + 3), 1.0, x)
    return x.astype(BF16)


def _fox_prompt_kernel(q_ref, k_ref, v_ref, fq_ref, fk_ref, o_ref, kaug, vbf, *, tq, nq, scale):
    h = pl.program_id(1)
    qi = pl.program_id(2)

    def head_column(f):
        lane = lax.broadcasted_iota(jnp.int32, f.shape, 1)
        return jnp.sum(jnp.where(lane == h, f, 0.0), axis=1, keepdims=True)

    @pl.when(qi == 0)
    def _():
        kaug[:, :HEAD_DIM] = k_ref[...].astype(BF16)
        kaug[:, HEAD_DIM:] = _bias_lanes(head_column(fk_ref[...]) * (-1.0 / scale), False)
        vbf[...] = v_ref[...].astype(BF16)

    q = jnp.concatenate(
        [q_ref[...], _bias_lanes(head_column(fq_ref[...]) * (1.0 / scale), True)], axis=1)
    c2 = scale * 1.4426950408889634

    def step(j, carry, masked):
        m, l, acc = carry
        start = j * tq
        s = lax.dot_general(q, kaug[pl.ds(start, tq), :], (((1,), (1,)), ((), ())),
                            preferred_element_type=F32) * c2
        if masked:
            r = lax.broadcasted_iota(jnp.int32, (tq, tq), 0)
            c = lax.broadcasted_iota(jnp.int32, (tq, tq), 1)
            s = jnp.where(c <= r, s, NEG_INF)
        m_new = jnp.maximum(m, jnp.max(s, axis=1, keepdims=True))
        alpha = jnp.exp2(m - m_new)
        p = jnp.exp2(s - m_new)
        l = alpha * l + jnp.sum(p, axis=1, keepdims=True)
        acc = alpha * acc + jnp.dot(p.astype(BF16), vbf[pl.ds(start, tq), :],
                                    preferred_element_type=F32)
        return m_new, l, acc

    init = (jnp.full((tq, 1), NEG_INF, F32), jnp.zeros((tq, 1), F32),
            jnp.zeros((tq, HEAD_DIM), F32))
    for n_full in range(nq):
        @pl.when(qi == n_full)
        def _(n_full=n_full):
            carry = init
            for j in range(n_full):
                carry = step(j, carry, False)
            _, l, acc = step(n_full, carry, True)
            o_ref[...] = (acc / l).astype(o_ref.dtype)


def _fox_prompt(q, k, v, f_col, batch, seq, heads, tq=512):
    tq = min(tq, seq)
    nq = seq // tq
    return pl.pallas_call(
        functools.partial(_fox_prompt_kernel, tq=tq, nq=nq, scale=HEAD_DIM ** -0.5),
        grid=(batch, heads, nq),
        in_specs=[pl.BlockSpec((tq, HEAD_DIM), lambda b, h, i: (b * nq + i, h)),
                  pl.BlockSpec((seq, HEAD_DIM), lambda b, h, i: (b, h)),
                  pl.BlockSpec((seq, HEAD_DIM), lambda b, h, i: (b, h)),
                  pl.BlockSpec((tq, LANES), lambda b, h, i: (b * nq + i, 0)),
                  pl.BlockSpec((seq, LANES), lambda b, h, i: (b, 0))],
        out_specs=pl.BlockSpec((tq, HEAD_DIM), lambda b, h, i: (b * nq + i, h)),
        out_shape=jax.ShapeDtypeStruct(q.shape, BF16),
        scratch_shapes=[pltpu.VMEM((seq, 2 * HEAD_DIM), BF16), pltpu.VMEM((seq, HEAD_DIM), BF16)],
        compiler_params=_params("arbitrary", "arbitrary", "arbitrary"),
        name="fox_prompt",
    )(q, k, v, f_col, f_col)


def _fox_sample_kernel(pt_ref, q_ref, *refs, n_steps, pps, heads, scale):
    del pt_ref
    ck_refs, cv_refs, lf_refs = refs[:pps], refs[pps:2 * pps], refs[2 * pps:3 * pps]
    kn_ref, vn_ref, fn_ref, fs_ref, o_ref, m_ref, l_ref, acc_ref, tail_ref = refs[3 * pps:]
    p = pl.program_id(1)

    @pl.when(p == 0)
    def _():
        m_ref[...] = jnp.full(m_ref.shape, NEG_INF, F32)
        l_ref[...] = jnp.zeros_like(l_ref)
        acc_ref[...] = jnp.zeros_like(acc_ref)
        tail_ref[...] = jnp.zeros_like(tail_ref)

    q = q_ref[...]
    rows = q.shape[0]
    fn = fn_ref[...]
    row = lax.broadcasted_iota(jnp.int32, (rows, LANES), 0)
    lane = lax.broadcasted_iota(jnp.int32, (rows, LANES), 1)
    own_head = (lane & (heads - 1)) == (row & (heads - 1))

    log2e = 1.4426950408889634

    def attend(keys, vals, lane_bias, visible):
        s = lax.dot_general(q, keys, (((1,), (1,)), ((), ())), preferred_element_type=F32)
        bias2 = lane_bias * log2e
        fn2 = fn * log2e
        parts = []
        for g in range(keys.shape[0] // LANES):
            sg = s[:, g * LANES:(g + 1) * LANES] * (scale * log2e) + (fn2 + bias2[g:g + 1, :])
            parts.append(jnp.where(visible, sg, NEG_INF))
        s = jnp.concatenate(parts, axis=1)
        m_old = m_ref[...]
        m_new = jnp.maximum(m_old, jnp.max(s, axis=1, keepdims=True))
        alpha = jnp.exp2(m_old - m_new)
        pr = jnp.exp2(s - m_new)
        l_ref[...] = alpha * l_ref[...] + jnp.sum(pr, axis=1, keepdims=True)
        acc_ref[...] = alpha * acc_ref[...] + jnp.dot(pr.astype(BF16), vals,
                                                      preferred_element_type=F32)
        m_ref[...] = m_new

    a = lax.broadcasted_iota(jnp.int32, (LANES, LANES), 0)
    b = lax.broadcasted_iota(jnp.int32, (LANES, LANES), 1)
    same_head = (a & (heads - 1)) == (b & (heads - 1))
    later_in_row = jnp.where(same_head & (a > b), 1.0, 0.0).astype(BF16)
    whole_row = jnp.where(same_head, 1.0, 0.0).astype(BF16)
    sums = jnp.concatenate([later_in_row, whole_row], axis=1)
    n_rows = lf_refs[0].shape[0]
    ra = lax.broadcasted_iota(jnp.int32, (n_rows, n_rows), 0)
    rb = lax.broadcasted_iota(jnp.int32, (n_rows, n_rows), 1)
    later_rows = jnp.where(rb > ra, 1.0, 0.0).astype(BF16)

    tail = tail_ref[...]
    biases = []
    for c in range(pps):
        lf = lf_refs[c][...]
        y = jnp.dot(jnp.concatenate(_split3(lf), axis=0), sums, preferred_element_type=F32)
        y = y[:n_rows] + y[n_rows:2 * n_rows] + y[2 * n_rows:]
        within, row_tot = y[:, :LANES], y[:, LANES:]
        z = jnp.dot(later_rows, jnp.concatenate(_split3(row_tot), axis=1),
                    preferred_element_type=F32)
        biases.append(within + z[:, :LANES] + z[:, LANES:2 * LANES] + z[:, 2 * LANES:] + tail)
        tail = tail + jnp.sum(row_tot, axis=0, keepdims=True)
    tail_ref[...] = tail

    attend(jnp.concatenate([r[...].astype(BF16) for r in ck_refs], axis=0),
           jnp.concatenate([r[...].astype(BF16) for r in cv_refs], axis=0),
           jnp.concatenate(biases, axis=0), own_head)

    @pl.when(p == n_steps - 1)
    def _():
        attend(kn_ref[...], vn_ref[...], -fs_ref[...],
               own_head & (lane - (lane & (heads - 1)) <= row - (row & (heads - 1))))
        o_ref[...] = acc_ref[...] / l_ref[...]


def _fox_sample(page_table, q_rows, cache_k, cache_v, cache_lf, k_new, v_new, fn_col, fs_lane,
                heads):
    nb, n_pages = page_table.shape
    rows, hd = q_rows.shape[1], q_rows.shape[2]
    lf_rows = cache_lf.shape[1]
    page_rows = lf_rows * LANES
    assert heads & (heads - 1) == 0 and LANES % heads == 0
    pps = math.gcd(n_pages, FOX_PAGES_PER_STEP)
    n_steps = n_pages // pps
    per_b = lambda b, p, pt: (b, 0, 0)

    def page_of(c):
        return lambda b, p, pt: pt[b, n_pages - 1 - (p * pps + c)]

    kv_specs = [pl.BlockSpec((page_rows, hd), lambda b, p, pt, f=page_of(c): (f(b, p, pt), 0))
                for c in range(pps)]
    lf_specs = [pl.BlockSpec((None, lf_rows, LANES),
                             lambda b, p, pt, f=page_of(c): (f(b, p, pt), 0, 0))
                for c in range(pps)]
    grid_spec = pltpu.PrefetchScalarGridSpec(
        num_scalar_prefetch=1,
        grid=(nb, n_steps),
        in_specs=([pl.BlockSpec((None, rows, hd), per_b)] + kv_specs + kv_specs + lf_specs
                  + [pl.BlockSpec((None, LANES, hd), per_b),
                     pl.BlockSpec((None, LANES, hd), per_b),
                     pl.BlockSpec((None, rows, 1), per_b),
                     pl.BlockSpec((None, 1, LANES), per_b)]),
        out_specs=pl.BlockSpec((None, rows, hd), per_b),
        scratch_shapes=[pltpu.VMEM((rows, 1), F32), pltpu.VMEM((rows, 1), F32),
                        pltpu.VMEM((rows, hd), F32), pltpu.VMEM((1, LANES), F32)],
    )
    return pl.pallas_call(
        functools.partial(_fox_sample_kernel, n_steps=n_steps, pps=pps, heads=heads,
                          scale=HEAD_DIM ** -0.5),
        grid_spec=grid_spec,
        out_shape=jax.ShapeDtypeStruct((nb, rows, hd), F32),
        compiler_params=_params("arbitrary", "arbitrary"),
        name="fox_sample",
    )(page_table, q_rows, *([cache_k] * pps), *([cache_v] * pps), *([cache_lf] * pps),
      k_new, v_new, fn_col, fs_lane)


def _sgu_kernel(u_ref, vg_ref, nw_ref, w_ref, bt_ref, *out_refs, groups):
    o_ref = out_refs[0]
    g = vg_ref[...].astype(F32)
    v = g * lax.rsqrt(jnp.mean(g * g, axis=-1, keepdims=True) + EPS) * nw_ref[...]
    if len(out_refs) > 1:
        out_refs[1][...] = v
    vb = v.astype(BF16)
    rows = v.shape[0]
    cw = v.shape[1] // groups
    r = lax.broadcasted_iota(jnp.int32, (rows, rows), 0)
    c = lax.broadcasted_iota(jnp.int32, (rows, rows), 1)
    bt = bt_ref[...]
    for gi in range(groups):
        wm = jnp.where(c <= r, w_ref[gi], 0.0).astype(BF16)
        z = jnp.dot(wm, vb[:, gi * cw:(gi + 1) * cw], preferred_element_type=F32)
        z = z + bt[:, gi:gi + 1]
        u = u_ref[:, gi * cw:(gi + 1) * cw].astype(F32)
        o_ref[:, gi * cw:(gi + 1) * cw] = (u * z).astype(o_ref.dtype)


def _sgu(uv, norm_w, w_pos, bias_t, rows, emit_v):
    m, two_w = uv.shape
    width = two_w // 2
    groups = w_pos.shape[0]
    out_shape = [jax.ShapeDtypeStruct((m, width), BF16)]
    out_specs = [pl.BlockSpec((rows, width), lambda i: (i, 0))]
    if emit_v:
        out_shape.append(jax.ShapeDtypeStruct((m, width), F32))
        out_specs.append(pl.BlockSpec((rows, width), lambda i: (i, 0)))
    return pl.pallas_call(
        functools.partial(_sgu_kernel, groups=groups),
        grid=(m // rows,),
        in_specs=[pl.BlockSpec((rows, width), lambda i: (i, 0)),
                  pl.BlockSpec((rows, width), lambda i: (i, 1)),
                  pl.BlockSpec((1, width), lambda i: (0, 0)),
                  pl.BlockSpec((groups, rows, rows), lambda i: (0, 0, 0)),
                  pl.BlockSpec((rows, groups), lambda i: (0, 0))],
        out_specs=out_specs,
        out_shape=out_shape,
        compiler_params=_params("arbitrary"),
        name="sgu",
    )(uv, uv, norm_w, w_pos, bias_t)


def _conv_sample_kernel(z0_ref, z1_ref, z2_ref, gb_ref, w_ref, o_ref):
    w = w_ref[...]
    y = w[0:1] * z0_ref[...] + w[1:2] * z1_ref[...] + w[2:3] * z2_ref[...]
    o_ref[...] = (gb_ref[...].astype(F32) * y).astype(o_ref.dtype)


def _conv_sample(z0, z1, z2, gb, conv_w):
    return pl.pallas_call(
        _conv_sample_kernel,
        out_shape=jax.ShapeDtypeStruct(z0.shape, BF16),
        name="conv_sample",
    )(z0, z1, z2, gb, conv_w)


def _gla_kernel(q_ref, k_ref, v_ref, la_ref, gate_ref, nw_ref, s0_ref, o_ref, st_ref, st, *, nc):
    ci = pl.program_id(1)

    @pl.when(ci == 0)
    def _():
        st[...] = s0_ref[...]

    heads, dv, dk = st.shape
    la = la_ref[...]
    c, width = la.shape
    r = lax.broadcasted_iota(jnp.int32, (c, c), 0)
    cc = lax.broadcasted_iota(jnp.int32, (c, c), 1)
    tril = cc <= r
    y = jnp.dot(jnp.where(tril, 1.0, 0.0).astype(BF16),
                jnp.concatenate(_split3(la), axis=1), preferred_element_type=F32)
    bc = y[:, :width] + y[:, width:2 * width] + y[:, 2 * width:]
    q = q_ref[...].astype(F32) * dk ** -0.5
    k = k_ref[...].astype(F32)
    qt = (q * jnp.exp(bc)).astype(BF16)
    mid = bc[c // 2 - 1:c // 2, :]
    qa = (q * jnp.exp(bc - mid)).astype(BF16)
    ka = (k * jnp.exp(mid - bc)).astype(BF16)
    bl = bc[c - 1:c, :]
    kd = (k * jnp.exp(bl - bc)).astype(BF16)
    decay = jnp.exp(bl)
    for h in range(heads):
        kcol = slice(h * dk, (h + 1) * dk)
        vcol = slice(h * dv, (h + 1) * dv)
        v = v_ref[:, vcol]
        att = lax.dot_general(qa[:, kcol], ka[:, kcol], (((1,), (1,)), ((), ())),
                              preferred_element_type=F32)
        att = jnp.where(tril, att, 0.0)
        s_t = st[h]
        o = (lax.dot_general(qt[:, kcol], s_t.astype(BF16), (((1,), (1,)), ((), ())),
                             preferred_element_type=F32)
             + jnp.dot(att.astype(BF16), v, preferred_element_type=F32))
        st[h] = s_t * decay[:, kcol] + lax.dot_general(v, kd[:, kcol], (((0,), (0,)), ((), ())),
                                                       preferred_element_type=F32)
        on = o * lax.rsqrt(jnp.mean(o * o, axis=-1, keepdims=True) + EPS) * nw_ref[:, vcol]
        o_ref[:, vcol] = (on * gate_ref[:, vcol].astype(F32)).astype(o_ref.dtype)

    @pl.when(ci == nc - 1)
    def _():
        st_ref[...] = st[...]


def _gla(qkv, log_a, gate, norm_w, s0_t, batch, seq, heads, chunk):
    m = qkv.shape[0]
    key_w, val_w = log_a.shape[1], gate.shape[1]
    dk, dv = key_w // heads, val_w // heads
    assert (2 * key_w) % val_w == 0
    v0 = (2 * key_w) // val_w
    chunk = min(chunk, seq)
    nc = seq // chunk
    row = lambda b, c: b * nc + c
    return pl.pallas_call(
        functools.partial(_gla_kernel, nc=nc),
        grid=(batch, nc),
        in_specs=[pl.BlockSpec((chunk, key_w), lambda b, c: (row(b, c), 0)),
                  pl.BlockSpec((chunk, key_w), lambda b, c: (row(b, c), 1)),
                  pl.BlockSpec((chunk, val_w), lambda b, c: (row(b, c), v0)),
                  pl.BlockSpec((chunk, key_w), lambda b, c: (row(b, c), 0)),
                  pl.BlockSpec((chunk, val_w), lambda b, c: (row(b, c), 0)),
                  pl.BlockSpec((1, val_w), lambda b, c: (0, 0)),
                  pl.BlockSpec((None, heads, dv, dk), lambda b, c: (b, 0, 0, 0))],
        out_specs=[pl.BlockSpec((chunk, val_w), lambda b, c: (row(b, c), 0)),
                   pl.BlockSpec((None, heads, dv, dk), lambda b, c: (b, 0, 0, 0))],
        out_shape=[jax.ShapeDtypeStruct((m, val_w), BF16),
                   jax.ShapeDtypeStruct((batch, heads, dv, dk), F32)],
        scratch_shapes=[pltpu.VMEM((heads, dv, dk), F32)],
        compiler_params=_params("arbitrary", "arbitrary"),
        name="gla",
    )(qkv, qkv, qkv, log_a, gate, norm_w, s0_t)


def kernel(x_prompt, x_sample, cache_k, cache_v, cache_logf, state_conv, state_gla, page_table,
           c_prompt, c_sample, w_in0, f_bias, q_norm, k_norm, sgu_norm, sgu_w, sgu_b, w_out0,
           w_in1, conv_w, gla_a_w2, gla_a_b, gla_norm, w_out1, ada_w, ada_b, mlp_w1, mlp_w2):
    nbp, seq, d = x_prompt.shape
    nbs, dseq, _ = x_sample.shape
    mp, ms = nbp * seq, nbs * dseq
    fox_w = d // 2
    heads = fox_w // HEAD_DIM
    sgu_width = d // 2
    groups = sgu_w.shape[1]
    conv_width = state_conv.shape[-1]
    gla_heads, gla_dk, gla_dv = state_gla.shape[2], state_gla.shape[3], state_gla.shape[4]
    gla_key, gla_val = gla_heads * gla_dk, gla_heads * gla_dv
    gla_rank = gla_a_w2.shape[1]
    n_pool, page = cache_k.shape[1], cache_k.shape[2]
    n_pages = page_table.shape[1]

    r_c = nbp + nbs
    r_pad = -(-r_c // 16) * 16
    c_rows = jnp.pad(jnp.concatenate([c_prompt, c_sample], axis=0),
                     ((0, r_pad - r_c), (0, 0))).astype(BF16)
    mod = _ada(c_rows, ada_w, ada_b)

    def mods(layer):
        parts = jnp.split(mod[layer], 6, axis=-1)
        pp = [p[:nbp].reshape(nbp, 1, d) for p in parts]
        ps = [jnp.repeat(p[nbp:r_c], dseq, axis=0) for p in parts]
        return pp, ps

    xp = x_prompt.reshape(mp, d)
    xs = x_sample.reshape(ms, d)

    def modulate_both(xp, xs, shp, scp, shs, scs):
        hp = _modulate(xp, shp, scp, seq)
        hs = _modulate(xs, shs.reshape(1, ms, d), scs.reshape(1, ms, d), ms, tr=ms)
        return hp, hs

    def mlp(xp, xs, layer, shp, scp, gp, shs, scs, gs):
        hp, hs = modulate_both(xp, xs, shp, scp, shs, scs)
        (ap,), (as_,) = _proj([hp], [hs], [(mlp_w1, layer, 0)], mlp_w1.shape[2], _epi_relu2,
                              [BF16], name="mlp_up")
        return _down(ap, as_, mlp_w2, layer, xp, gp, xs, gs, seq)

    def residual_proj(lhs_p, lhs_s, w, xp, xs, gp, gs, name):
        (yp,), (ys,) = _proj(lhs_p, lhs_s, [(w, 0, 0)], d, _epi_residual, [F32],
                             ext=[("res_p", "prow", xp), ("gate_p", "pgrp", gp),
                                  ("res_s", "srow", xs), ("gate_s", "srow", gs)],
                             rows_per_group=seq, name=name)
        return yp, ys

    (sh1p, sc1p, g1p, sh2p, sc2p, g2p), (sh1s, sc1s, g1s, sh2s, sc2s, g2s) = mods(0)
    hp, hs = modulate_both(xp, xs, sh1p, sc1p, sh1s, sc1s)
    w0 = jnp.swapaxes(w_in0, 1, 2)
    qn = jnp.tile(q_norm[0], heads).reshape(1, fox_w)
    kn = jnp.tile(k_norm[0], heads).reshape(1, fox_w)
    (q_p,), (q_s,) = _proj([hp], [hs], [(w0, 0, 0)], fox_w, _epi_head_norm, [BF16],
                           ext=[("norm", "col", qn)], w_rows=True, name="in0_q")
    (k_p,), (k_s,) = _proj([hp], [hs], [(w0, 0, fox_w)], fox_w, _epi_head_norm, [F32],
                           ext=[("norm", "col", kn)], w_rows=True, name="in0_k")
    (v_p,), (v_s,) = _proj([hp], [hs], [(w0, 0, 2 * fox_w)], fox_w, _epi_plain, [F32],
                           w_rows=True, name="in0_v")
    fb = jnp.pad(f_bias[0], (0, LANES - heads)).reshape(1, LANES)
    (lf_p,), (lf_s,) = _proj([hp], [hs], [(w0, 0, 3 * fox_w)], LANES, _epi_log_forget, [F32],
                             ext=[("bias", "col", fb)], w_rows=True, name="in0_logf")
    (uv_p,), (uv_s,) = _proj([hp], [hs], [(w0, 0, 3 * fox_w + heads)], 2 * sgu_width, _epi_gelu,
                             [BF16], w_rows=True, name="in0_uv")

    f_col = _cumsum_rows(lf_p, nbp)
    oa_p = _fox_prompt(q_p, k_p, v_p, f_col, nbp, seq, heads)

    lf_s16 = lf_s[:, :heads].reshape(nbs, dseq, heads)
    tpad = 8
    lf_s_pad = jnp.pad(lf_s.reshape(nbs, dseq, LANES), ((0, 0), (0, tpad - dseq), (0, 0)))
    fn = _cumsum_rows(lf_s_pad.reshape(nbs * tpad, LANES), nbs).reshape(
        nbs, tpad, LANES)[:, :dseq, :heads]
    rows_s = dseq * heads
    assert rows_s <= LANES

    def new_rows(a):
        a = a.reshape(nbs, rows_s, HEAD_DIM)
        return jnp.pad(a, ((0, 0), (0, LANES - rows_s), (0, 0))).astype(BF16)

    fs_lane = jnp.pad(fn.reshape(nbs, 1, rows_s), ((0, 0), (0, 0), (0, LANES - rows_s)))
    oa_rows = _fox_sample(page_table, q_s.reshape(nbs, rows_s, HEAD_DIM),
                          cache_k[0].reshape(n_pool * page * heads, HEAD_DIM),
                          cache_v[0].reshape(n_pool * page * heads, HEAD_DIM),
                          cache_logf[0].reshape(n_pool, page * heads // LANES, LANES),
                          new_rows(k_s), new_rows(v_s), fn.reshape(nbs, rows_s, 1), fs_lane, heads)
    oa_s = oa_rows.reshape(ms, fox_w).astype(BF16)

    sn = sgu_norm[0].reshape(1, sgu_width)
    rows_p = min(seq, SGU_CHUNK)
    (ob_p,) = _sgu(uv_p, sn, sgu_w[0][:, :rows_p, :rows_p], sgu_b[0][:, :rows_p].T, rows_p, False)
    rows_g = min(dseq, SGU_CHUNK)
    w_small = sgu_w[0][:, :rows_g, :rows_g]
    w_big = jnp.einsum("ab,gts->gatbs", jnp.eye(nbs, dtype=F32), w_small).reshape(groups, ms, ms)
    b_big = jnp.tile(sgu_b[0][:, :rows_g].T, (nbs, 1))
    ob_s, sgu_v = _sgu(uv_s, sn, w_big, b_big, ms, True)

    xp, xs = residual_proj([oa_p, ob_p], [oa_s, ob_s], w_out0, xp, xs, g1p, g1s, "out0")
    xp, xs = mlp(xp, xs, 0, sh2p, sc2p, g2p, sh2s, sc2s, g2s)

    (sh1p, sc1p, g1p, sh2p, sc2p, g2p), (sh1s, sc1s, g1s, sh2s, sc2s, g2s) = mods(1)
    hp, hs = modulate_both(xp, xs, sh1p, sc1p, sh1s, sc1s)
    w1 = jnp.swapaxes(w_in1, 1, 2)
    cw = conv_width
    cwt = conv_w[0]
    n_tap = cwt.shape[0]
    (oc_p, ztail_p), (gb_s, z_s) = _proj(
        [hp], [hs], [(w1, 0, 0), (w1, 0, cw), (w1, 0, 2 * cw)], cw, _make_epi_conv(seq),
        [BF16, F32], ext=[("taps", "col", cwt)], tm=512, tn=256, rows_per_group=seq, w_rows=True,
        tail_rows=(1,), scratch=lambda tm, tn: [pltpu.VMEM((tm + 8, tn), F32)], name="in1_conv")
    conv_p = ztail_p.reshape(nbp, -1, 8, cw)[:, -1, 8 - (n_tap - 1):, :]
    (qkv_p,), (qkv_s,) = _proj([hp], [hs], [(w1, 0, 3 * cw)], 2 * gla_key + gla_val, _epi_plain,
                               [BF16], w_rows=True, name="in1_qkv")
    c_ga = 3 * cw + 2 * gla_key + gla_val
    (ga_p,), (ga_s,) = _proj([hp], [hs], [(w1, 0, c_ga)], LANES, _epi_plain, [BF16],
                             w_rows=True, name="in1_ga")
    w_a2 = jnp.pad(gla_a_w2, ((0, 0), (0, LANES - gla_rank), (0, 0)))
    (la_p,), (la_s,) = _proj([ga_p], [ga_s], [(w_a2, 0, 0)], gla_key, _epi_log_decay, [F32],
                             ext=[("bias", "col", gla_a_b[0].reshape(1, gla_key))], name="in1_log_a")
    (gg_p,), (gg_s,) = _proj([hp], [hs], [(w1, 0, c_ga + gla_rank)], gla_val, _epi_silu, [BF16],
                             w_rows=True, name="in1_gate")

    zp_s = jnp.concatenate([state_conv[0], z_s.reshape(nbs, dseq, cw)], axis=1)
    shifted = [zp_s[:, i:i + dseq].reshape(ms, cw) for i in range(cwt.shape[0])]
    oc_s = _conv_sample(shifted[0], shifted[1], shifted[2], gb_s, cwt)
    conv_s = zp_s[:, dseq:, :]

    gn = gla_norm[0].reshape(1, gla_val)
    s0_p = jnp.zeros((nbp, gla_heads, gla_dv, gla_dk), F32)
    od_p, st_p = _gla(qkv_p, la_p, gg_p, gn, s0_p, nbp, seq, gla_heads, GLA_CHUNK_ROWS)
    pad_t = 16

    def pad_rows(a):
        return jnp.pad(a.reshape(nbs, dseq, -1), ((0, 0), (0, pad_t - dseq), (0, 0))).reshape(
            nbs * pad_t, -1)

    od_s_pad, st_s = _gla(pad_rows(qkv_s), pad_rows(la_s), pad_rows(gg_s), gn,
                          state_gla[0].swapaxes(-1, -2), nbs, pad_t, gla_heads, pad_t)
    od_s = od_s_pad.reshape(nbs, pad_t, gla_val)[:, :dseq].reshape(ms, gla_val)

    xp, xs = residual_proj([oc_p, od_p], [oc_s, od_s], w_out1, xp, xs, g1p, g1s, "out1")
    xp, xs = mlp(xp, xs, 1, sh2p, sc2p, g2p, sh2s, sc2s, g2s)

    y_prompt = xp.reshape(nbp, seq, d)
    y_sample = xs.reshape(nbs, dseq, d)
    return (y_prompt, y_sample,
            _to_heads(k_p, heads).reshape(1, nbp, seq, heads, HEAD_DIM),
            _to_heads(v_p, heads).reshape(1, nbp, seq, heads, HEAD_DIM),
            lf_p[:, :heads].reshape(1, nbp, seq, heads),
            k_s.reshape(1, nbs, dseq, heads, HEAD_DIM), v_s.reshape(1, nbs, dseq, heads, HEAD_DIM),
            lf_s16[None], sgu_v.reshape(1, nbs, dseq, sgu_width),
            conv_p[None], conv_s[None],
            st_p.swapaxes(-1, -2)[None], st_s.swapaxes(-1, -2)[None])
```

```python
import functools
import math

import jax
import jax.numpy as jnp
from jax import lax
from jax.experimental import pallas as pl
from jax.experimental.pallas import tpu as pltpu

F32 = jnp.float32
BF16 = jnp.bfloat16
EPS = 1e-6
NEG_INF = -1e30
HEAD_DIM = 128
LANES = 128
SGU_CHUNK = 128
GLA_TAU = 16.0
GLA_CHUNK_ROWS = 128
FOX_PAGES_PER_STEP = 4
VMEM_LIMIT_BYTES = 56 * 1024 * 1024


def _params(*semantics):
    return pltpu.CompilerParams(dimension_semantics=semantics,
                                vmem_limit_bytes=VMEM_LIMIT_BYTES)


def _split3(x):
    hi = x.astype(BF16)
    r = x - hi.astype(F32)
    mid = r.astype(BF16)
    lo = (r - mid.astype(F32)).astype(BF16)
    return hi, mid, lo


def _log_sigmoid(x):
    return jnp.minimum(x, 0.0) - jnp.log1p(jnp.exp(-jnp.abs(x)))


def _gelu_tanh(x):
    c = 0.7978845608028654
    return 0.5 * x * (1.0 + jnp.tanh(c * (x + 0.044715 * (x * x * x))))


def _ada_kernel(c_ref, w_ref, b_ref, o_ref):
    o_ref[...] = jnp.dot(c_ref[...], w_ref[...].astype(BF16),
                         preferred_element_type=F32) + b_ref[...]


def _ada(c_rows, ada_w, ada_b, tn=512):
    n_layers, d, n = ada_w.shape
    r = c_rows.shape[0]
    tn = min(tn, n)
    return pl.pallas_call(
        _ada_kernel,
        grid=(n_layers, n // tn),
        in_specs=[pl.BlockSpec((r, d), lambda l, j: (0, 0)),
                  pl.BlockSpec((None, d, tn), lambda l, j: (l, 0, j)),
                  pl.BlockSpec((None, 1, tn), lambda l, j: (l, 0, j))],
        out_specs=pl.BlockSpec((None, r, tn), lambda l, j: (l, 0, j)),
        out_shape=jax.ShapeDtypeStruct((n_layers, r, n), F32),
        compiler_params=_params("arbitrary", "arbitrary"),
        name="ada",
    )(c_rows, ada_w, ada_b.reshape(n_layers, 1, n))


def _modulate_kernel(x_ref, sh_ref, sc_ref, o_ref):
    x = x_ref[...]
    y = x * lax.rsqrt(jnp.mean(x * x, axis=-1, keepdims=True) + EPS)
    o_ref[...] = (y * (1.0 + sc_ref[...]) + sh_ref[...]).astype(o_ref.dtype)


def _modulate(x, sh, sc, rows_per_group, tr=256):
    rows, d = x.shape
    tr = min(tr, rows)
    r = sh.shape[1]
    grp = lambda i: ((i * tr) // rows_per_group, 0, 0)
    return pl.pallas_call(
        _modulate_kernel,
        grid=(rows // tr,),
        in_specs=[pl.BlockSpec((tr, d), lambda i: (i, 0)),
                  pl.BlockSpec((None, r, d), grp),
                  pl.BlockSpec((None, r, d), grp)],
        out_specs=pl.BlockSpec((tr, d), lambda i: (i, 0)),
        out_shape=jax.ShapeDtypeStruct((rows, d), BF16),
        compiler_params=_params("arbitrary"),
        name="modulate",
    )(x, sh, sc)


def _proj_kernel(*refs, k_sizes, n_w, ext_names, n_out, epilogue, cast_rows, w_rows, n_scratch):
    n_lhs = len(k_sizes)
    pos = 0
    xp = refs[pos:pos + n_lhs]; pos += n_lhs
    xs = refs[pos:pos + n_lhs]; pos += n_lhs
    w = refs[pos:pos + n_w]; pos += n_w
    ext = dict(zip(ext_names, refs[pos:pos + len(ext_names)])); pos += len(ext_names)
    out_p = refs[pos:pos + n_out]; pos += n_out
    out_s = refs[pos:pos + n_out]; pos += n_out
    wbf = refs[pos:pos + n_w]; pos += n_w
    ext["scratch"] = refs[pos:pos + n_scratch]

    def accumulate(lhs):
        accs = []
        for m in range(n_w):
            acc = None
            off = 0
            for a, ka in enumerate(k_sizes):
                part = jnp.dot(lhs[a][...], wbf[m][off:off + ka, :],
                               preferred_element_type=F32)
                acc = part if acc is None else acc + part
                off += ka
            accs.append(acc)
        return accs

    def store(out_refs, vals):
        for o_ref, val in zip(out_refs, vals):
            o_ref[...] = val.astype(o_ref.dtype)

    @pl.when(pl.program_id(1) == 0)
    def _():
        accs_p, accs_s = [None] * n_w, [None] * n_w
        r0 = 0
        for a, ka in enumerate(k_sizes):
            for loc in range(0, ka, cast_rows):
                for m in range(n_w):
                    if w_rows:
                        chunk = w[m][:, r0:r0 + cast_rows].T.astype(BF16)
                    else:
                        chunk = w[m][r0:r0 + cast_rows, :].astype(BF16)
                    wbf[m][r0:r0 + cast_rows, :] = chunk
                    pp = jnp.dot(xp[a][:, loc:loc + cast_rows], chunk, preferred_element_type=F32)
                    ps = jnp.dot(xs[a][:, loc:loc + cast_rows], chunk, preferred_element_type=F32)
                    accs_p[m] = pp if accs_p[m] is None else accs_p[m] + pp
                    accs_s[m] = ps if accs_s[m] is None else accs_s[m] + ps
                r0 += cast_rows
        store(out_s, epilogue(accs_s, ext, True))
        store(out_p, epilogue(accs_p, ext, False))

    @pl.when(pl.program_id(1) != 0)
    def _():
        store(out_p, epilogue(accumulate(xp), ext, False))


def _proj(xp, xs, weights, n_cols, epilogue, out_dtypes, ext=(), *, tm=1024, tn=512,
          rows_per_group=None, w_rows=False, tail_rows=(), scratch=None, name="proj"):
    mp, ms = xp[0].shape[0], xs[0].shape[0]
    k_sizes = tuple(int(a.shape[1]) for a in xp)
    k_total = sum(k_sizes)
    tm = min(tm, mp, rows_per_group or mp)
    tn = min(tn, n_cols)
    assert mp % tm == 0 and n_cols % tn == 0
    for warr, _, c0 in weights:
        assert warr.ndim == 3
        if w_rows:
            assert warr.shape[2] == k_total and c0 % 8 == 0 and len(k_sizes) == 1
        else:
            assert warr.shape[1] == k_total and c0 % tn == 0
    in_specs, args = [], []
    for a in xp:
        in_specs.append(pl.BlockSpec((tm, a.shape[1]), lambda j, i: (i, 0)))
        args.append(a)
    for a in xs:
        in_specs.append(pl.BlockSpec((ms, a.shape[1]), lambda j, i: (0, 0)))
        args.append(a)
    for warr, layer, c0 in weights:
        if w_rows:
            spec = pl.BlockSpec((None, pl.Element(tn), pl.Element(k_total)),
                                lambda j, i, c=c0, l=layer: (l, pl.multiple_of(c + j * tn, 8), 0))
        else:
            spec = pl.BlockSpec((None, k_total, tn),
                                lambda j, i, cb=c0 // tn, l=layer: (l, 0, j + cb))
        in_specs.append(spec)
        args.append(warr)
    ext_names = []
    for ename, kind, arr in ext:
        ext_names.append(ename)
        if kind == "col":
            spec = pl.BlockSpec((arr.shape[0], tn), lambda j, i: (0, j))
        elif kind == "prow":
            spec = pl.BlockSpec((tm, tn), lambda j, i: (i, j))
        elif kind == "pgrp":
            spec = pl.BlockSpec((None, 1, tn),
                                lambda j, i: ((i * tm) // rows_per_group, 0, j))
        elif kind == "srow":
            spec = pl.BlockSpec((ms, tn), lambda j, i: (0, j))
        else:
            raise ValueError(kind)
        in_specs.append(spec)
        args.append(arr)
    n_out = len(out_dtypes)
    rows_p = [(8 * (mp // tm), 8) if o in tail_rows else (mp, tm) for o in range(n_out)]
    out_specs = ([pl.BlockSpec((br, tn), lambda j, i: (i, j)) for _, br in rows_p]
                 + [pl.BlockSpec((ms, tn), lambda j, i: (0, j))] * n_out)
    out_shape = ([jax.ShapeDtypeStruct((r, n_cols), dt) for (r, _), dt in zip(rows_p, out_dtypes)]
                 + [jax.ShapeDtypeStruct((ms, n_cols), dt) for dt in out_dtypes])
    cast_rows = min(512, *k_sizes)
    assert all(ka % cast_rows == 0 for ka in k_sizes)
    extra_scratch = scratch(tm, tn) if scratch else []
    outs = pl.pallas_call(
        functools.partial(_proj_kernel, k_sizes=k_sizes, n_w=len(weights),
                          ext_names=tuple(ext_names), n_out=n_out, epilogue=epilogue,
                          cast_rows=cast_rows, w_rows=w_rows, n_scratch=len(extra_scratch)),
        grid=(n_cols // tn, mp // tm),
        in_specs=in_specs,
        out_specs=out_specs,
        out_shape=out_shape,
        scratch_shapes=[pltpu.VMEM((k_total, tn), BF16) for _ in weights] + extra_scratch,
        compiler_params=_params("arbitrary", "arbitrary"),
        name=name,
    )(*args)
    return outs[:n_out], outs[n_out:]


def _epi_plain(accs, ext, is_sample):
    return (accs[0],)


def _epi_head_norm(accs, ext, is_sample):
    acc = accs[0]
    wn = ext["norm"][...]
    parts = []
    for c0 in range(0, acc.shape[1], HEAD_DIM):
        a = acc[:, c0:c0 + HEAD_DIM]
        y = a * lax.rsqrt(jnp.mean(a * a, axis=-1, keepdims=True) + EPS)
        parts.append(y * wn[:, c0:c0 + HEAD_DIM])
    return (jnp.concatenate(parts, axis=1),)


def _epi_log_forget(accs, ext, is_sample):
    return (_log_sigmoid(accs[0] + ext["bias"][...]),)


def _epi_gelu(accs, ext, is_sample):
    return (_gelu_tanh(accs[0]),)


def _epi_silu(accs, ext, is_sample):
    a = accs[0]
    return (a * jax.nn.sigmoid(a),)


def _epi_relu2(accs, ext, is_sample):
    r = jnp.maximum(accs[0], 0.0)
    return (r * r,)


def _make_epi_conv(seq):
    def epilogue(accs, ext, is_sample):
        z = accs[1] * accs[0]
        if is_sample:
            return accs[2], z
        (zbuf,) = ext["scratch"]
        tm = z.shape[0]

        @pl.when(pl.program_id(1) % (seq // tm) == 0)
        def _():
            zbuf[0:8, :] = jnp.zeros((8, zbuf.shape[1]), F32)

        zbuf[8:8 + tm, :] = z
        w = ext["taps"][...]
        y = w[0:1] * zbuf[6:6 + tm, :] + w[1:2] * zbuf[7:7 + tm, :] + w[2:3] * z
        tail = z[tm - 8:tm, :]
        zbuf[0:8, :] = tail
        return accs[2] * y, tail

    return epilogue


def _epi_log_decay(accs, ext, is_sample):
    return (_log_sigmoid(accs[0] + ext["bias"][...]) / GLA_TAU,)


def _epi_residual(accs, ext, is_sample):
    if is_sample:
        return (ext["res_s"][...] + ext["gate_s"][...] * accs[0],)
    return (ext["res_p"][...] + ext["gate_p"][...] * accs[0],)


def _down_kernel(xp_ref, xs_ref, w_ref, resp_ref, gp_ref, ress_ref, gs_ref,
                 op_ref, os_ref, accs_ref, wbf, *, nk, cast_rows):
    i = pl.program_id(1)
    k = pl.program_id(2)

    @pl.when(k == 0)
    def _():
        op_ref[...] = jnp.zeros_like(op_ref)

    total = None
    for r0 in range(0, w_ref.shape[0], cast_rows):
        wbf[r0:r0 + cast_rows, :] = w_ref[r0:r0 + cast_rows, :].astype(BF16)
        part = jnp.dot(xp_ref[:, r0:r0 + cast_rows], wbf[r0:r0 + cast_rows, :],
                       preferred_element_type=F32)
        total = part if total is None else total + part
    op_ref[...] += total

    @pl.when(k == nk - 1)
    def _():
        op_ref[...] = resp_ref[...] + gp_ref[...] * op_ref[...]

    @pl.when(i == 0)
    def _():
        ps = jnp.dot(xs_ref[...], wbf[...], preferred_element_type=F32)

        @pl.when(k == 0)
        def _():
            accs_ref[...] = ps

        @pl.when(k > 0)
        def _():
            accs_ref[...] += ps

        @pl.when(k == nk - 1)
        def _():
            os_ref[...] = ress_ref[...] + gs_ref[...] * accs_ref[...]


def _down(xp, xs, w, layer, res_p, gate_p, res_s, gate_s, rows_per_group, *,
          tm=1024, tn=1024, tk=2048):
    mp, kdim = xp.shape
    ms = xs.shape[0]
    n = w.shape[2]
    tm, tn, tk = min(tm, rows_per_group, mp), min(tn, n), min(tk, kdim)
    nk = kdim // tk
    cast_rows = min(512, tk)
    assert tk % cast_rows == 0
    return pl.pallas_call(
        functools.partial(_down_kernel, nk=nk, cast_rows=cast_rows),
        grid=(n // tn, mp // tm, nk),
        in_specs=[pl.BlockSpec((tm, tk), lambda j, i, k: (i, k)),
                  pl.BlockSpec((ms, tk), lambda j, i, k: (0, k)),
                  pl.BlockSpec((None, tk, tn), lambda j, i, k: (layer, k, j)),
                  pl.BlockSpec((tm, tn), lambda j, i, k: (i, j)),
                  pl.BlockSpec((None, 1, tn),
                               lambda j, i, k: ((i * tm) // rows_per_group, 0, j)),
                  pl.BlockSpec((ms, tn), lambda j, i, k: (0, j)),
                  pl.BlockSpec((ms, tn), lambda j, i, k: (0, j))],
        out_specs=[pl.BlockSpec((tm, tn), lambda j, i, k: (i, j)),
                   pl.BlockSpec((ms, tn), lambda j, i, k: (0, j))],
        out_shape=[jax.ShapeDtypeStruct((mp, n), F32),
                   jax.ShapeDtypeStruct((ms, n), F32)],
        scratch_shapes=[pltpu.VMEM((ms, tn), F32), pltpu.VMEM((tk, tn), BF16)],
        compiler_params=_params("arbitrary", "arbitrary", "arbitrary"),
        name="mlp_down",
    )(xp, xs, w, res_p, gate_p, res_s, gate_s)


def _to_heads_kernel(x_ref, o_ref):
    for h in range(o_ref.shape[1]):
        o_ref[:, h, :] = x_ref[:, h * HEAD_DIM:(h + 1) * HEAD_DIM]


def _to_heads(x, heads, tr=256):
    m, w = x.shape
    tr = min(tr, m)
    return pl.pallas_call(
        _to_heads_kernel,
        grid=(m // tr,),
        in_specs=[pl.BlockSpec((tr, w), lambda i: (i, 0))],
        out_specs=pl.BlockSpec((tr, heads, HEAD_DIM), lambda i: (i, 0, 0)),
        out_shape=jax.ShapeDtypeStruct((m, heads, HEAD_DIM), x.dtype),
        compiler_params=_params("arbitrary"),
        name="to_heads",
    )(x)


def _cumsum_kernel(x_ref, o_ref, carry_ref):
    @pl.when(pl.program_id(1) == 0)
    def _():
        carry_ref[...] = jnp.zeros_like(carry_ref)

    x = x_ref[...]
    t, w = x.shape
    r = lax.broadcasted_iota(jnp.int32, (t, t), 0)
    c = lax.broadcasted_iota(jnp.int32, (t, t), 1)
    tri = jnp.where(c <= r, 1.0, 0.0).astype(BF16)
    y = jnp.dot(tri, jnp.concatenate(_split3(x), axis=1), preferred_element_type=F32)
    out = y[:, :w] + y[:, w:2 * w] + y[:, 2 * w:] + carry_ref[...]
    o_ref[...] = out
    carry_ref[...] = out[t - 1:t, :]


def _cumsum_rows(x, n_groups, tc=256):
    rows, w = x.shape
    per = rows // n_groups
    tc = min(tc, per)
    nt = per // tc
    return pl.pallas_call(
        _cumsum_kernel,
        grid=(n_groups, nt),
        in_specs=[pl.BlockSpec((tc, w), lambda b, t: (b * nt + t, 0))],
        out_specs=pl.BlockSpec((tc, w), lambda b, t: (b * nt + t, 0)),
        out_shape=jax.ShapeDtypeStruct((rows, w), F32),
        scratch_shapes=[pltpu.VMEM((1, w), F32)],
        compiler_params=_params("arbitrary", "arbitrary"),
        name="cumsum_logf",
    )(x)


def _bias_lanes(col, own_first):
    hi, mid, lo = _split3(col)
    lane = lax.broadcasted_iota(jnp.int32, (col.shape[0], LANES), 1)
    own, other = (0, 3) if own_first else (3, 0)
    x = jnp.where(lane == own, hi.astype(F32),
                  jnp.where(lane == own + 1, mid.astype(F32),
                            jnp.where(lane == own + 2, lo.astype(F32), 0.0)))
    x = jnp.where((lane >= other) & (lane < other + 3), 1.0, x)
    return x.astype(BF16)


def _fox_prompt_step(q_ref, k_ref, v_ref, fq_ref, fk_ref, o_ref, kaug, vbf, *, tq, nq, scale,
                     alongside):
    h = pl.program_id(1)
    qi = pl.program_id(2)

    def head_column(f):
        lane = lax.broadcasted_iota(jnp.int32, f.shape, 1)
        return jnp.sum(jnp.where(lane == h, f, 0.0), axis=1, keepdims=True)

    @pl.when(qi == 0)
    def _():
        kaug[:, :HEAD_DIM] = k_ref[...].astype(BF16)
        kaug[:, HEAD_DIM:] = _bias_lanes(head_column(fk_ref[...]) * (-1.0 / scale), False)
        vbf[...] = v_ref[...].astype(BF16)

    q = jnp.concatenate(
        [q_ref[...], _bias_lanes(head_column(fq_ref[...]) * (1.0 / scale), True)], axis=1)
    c2 = scale * 1.4426950408889634

    def step(j, carry, masked):
        m, l, acc = carry
        start = j * tq
        s = lax.dot_general(q, kaug[pl.ds(start, tq), :], (((1,), (1,)), ((), ())),
                            preferred_element_type=F32) * c2
        if masked:
            r = lax.broadcasted_iota(jnp.int32, (tq, tq), 0)
            c = lax.broadcasted_iota(jnp.int32, (tq, tq), 1)
            s = jnp.where(c <= r, s, NEG_INF)
        m_new = jnp.maximum(m, jnp.max(s, axis=1, keepdims=True))
        alpha = jnp.exp2(m - m_new)
        p = jnp.exp2(s - m_new)
        l = alpha * l + jnp.sum(p, axis=1, keepdims=True)
        acc = alpha * acc + jnp.dot(p.astype(BF16), vbf[pl.ds(start, tq), :],
                                    preferred_element_type=F32)
        return m_new, l, acc

    init = (jnp.full((tq, 1), NEG_INF, F32), jnp.zeros((tq, 1), F32),
            jnp.zeros((tq, HEAD_DIM), F32))
    for n_full in range(nq):
        @pl.when(qi == n_full)
        def _(n_full=n_full):
            alongside()
            carry = init
            for j in range(n_full):
                carry = step(j, carry, False)
            _, l, acc = step(n_full, carry, True)
            o_ref[...] = (acc / l).astype(o_ref.dtype)


def _fox_sample_step(p, live, q_ref, *refs, n_steps, pps, heads, scale):
    ck_refs, cv_refs, lf_refs = refs[:pps], refs[pps:2 * pps], refs[2 * pps:3 * pps]
    kn_ref, vn_ref, fn_ref, fs_ref, o_ref, m_ref, l_ref, acc_ref, tail_ref = refs[3 * pps:]
    first = p == 0

    def carried(ref, start):
        return jnp.where(first, start, ref[...])

    def keep(ref, new, old):
        ref[...] = jnp.where(live, new, old)

    q = q_ref[...]
    rows = q.shape[0]
    fn = fn_ref[...]
    row = lax.broadcasted_iota(jnp.int32, (rows, LANES), 0)
    lane = lax.broadcasted_iota(jnp.int32, (rows, LANES), 1)
    own_head = (lane & (heads - 1)) == (row & (heads - 1))

    log2e = 1.4426950408889634

    def attend(keys, vals, lane_bias, visible, paged):
        s = lax.dot_general(q, keys, (((1,), (1,)), ((), ())), preferred_element_type=F32)
        bias2 = lane_bias * log2e
        fn2 = fn * log2e
        parts = []
        for g in range(keys.shape[0] // LANES):
            sg = s[:, g * LANES:(g + 1) * LANES] * (scale * log2e) + (fn2 + bias2[g:g + 1, :])
            parts.append(jnp.where(visible, sg, NEG_INF))
        s = jnp.concatenate(parts, axis=1)
        if paged:
            m_old, l_old, acc_old = (carried(m_ref, NEG_INF), carried(l_ref, 0.0),
                                     carried(acc_ref, 0.0))
        else:
            m_old, l_old, acc_old = m_ref[...], l_ref[...], acc_ref[...]
        m_new = jnp.maximum(m_old, jnp.max(s, axis=1, keepdims=True))
        alpha = jnp.exp2(m_old - m_new)
        pr = jnp.exp2(s - m_new)
        l_new = alpha * l_old + jnp.sum(pr, axis=1, keepdims=True)
        acc_new = alpha * acc_old + jnp.dot(pr.astype(BF16), vals, preferred_element_type=F32)
        if paged:
            keep(m_ref, m_new, m_ref[...])
            keep(l_ref, l_new, l_ref[...])
            keep(acc_ref, acc_new, acc_ref[...])
        else:
            m_ref[...], l_ref[...], acc_ref[...] = m_new, l_new, acc_new

    a = lax.broadcasted_iota(jnp.int32, (LANES, LANES), 0)
    b = lax.broadcasted_iota(jnp.int32, (LANES, LANES), 1)
    same_head = (a & (heads - 1)) == (b & (heads - 1))
    later_in_row = jnp.where(same_head & (a > b), 1.0, 0.0).astype(BF16)
    whole_row = jnp.where(same_head, 1.0, 0.0).astype(BF16)
    sums = jnp.concatenate([later_in_row, whole_row], axis=1)
    n_rows = lf_refs[0].shape[0]
    ra = lax.broadcasted_iota(jnp.int32, (n_rows, n_rows), 0)
    rb = lax.broadcasted_iota(jnp.int32, (n_rows, n_rows), 1)
    later_rows = jnp.where(rb > ra, 1.0, 0.0).astype(BF16)

    tail = carried(tail_ref, 0.0)
    biases = []
    for c in range(pps):
        lf = lf_refs[c][...]
        y = jnp.dot(jnp.concatenate(_split3(lf), axis=0), sums, preferred_element_type=F32)
        y = y[:n_rows] + y[n_rows:2 * n_rows] + y[2 * n_rows:]
        within, row_tot = y[:, :LANES], y[:, LANES:]
        z = jnp.dot(later_rows, jnp.concatenate(_split3(row_tot), axis=1),
                    preferred_element_type=F32)
        biases.append(within + z[:, :LANES] + z[:, LANES:2 * LANES] + z[:, 2 * LANES:] + tail)
        tail = tail + jnp.sum(row_tot, axis=0, keepdims=True)
    keep(tail_ref, tail, tail_ref[...])

    attend(jnp.concatenate([r[...].astype(BF16) for r in ck_refs], axis=0),
           jnp.concatenate([r[...].astype(BF16) for r in cv_refs], axis=0),
           jnp.concatenate(biases, axis=0), own_head, True)

    @pl.when((p == n_steps - 1) & live)
    def _():
        attend(kn_ref[...], vn_ref[...], -fs_ref[...],
               own_head & (lane - (lane & (heads - 1)) <= row - (row & (heads - 1))), False)
        o_ref[...] = acc_ref[...] / l_ref[...]


def _fox_kernel(pt_ref, *refs, n_prompt_in, n_sample_in, nq, heads, n_steps, n_live, tq, pps, scale):
    del pt_ref
    prompt_in = refs[:n_prompt_in]
    sample_in = refs[n_prompt_in:n_prompt_in + n_sample_in]
    op_ref, os_ref, kaug, vbf, m_ref, l_ref, acc_ref, tail_ref = refs[n_prompt_in + n_sample_in:]
    g = (pl.program_id(0) * heads + pl.program_id(1)) * nq + pl.program_id(2)

    @pl.when(g == 0)
    def _():
        m_ref[...] = jnp.full(m_ref.shape, NEG_INF, F32)
        l_ref[...] = jnp.zeros_like(l_ref)
        acc_ref[...] = jnp.zeros_like(acc_ref)
        tail_ref[...] = jnp.zeros_like(tail_ref)

    live = g < n_live
    p = jnp.minimum(g, n_live - 1) % n_steps

    def sample_step():
        _fox_sample_step(p, live, *sample_in, os_ref, m_ref, l_ref, acc_ref, tail_ref,
                         n_steps=n_steps, pps=pps, heads=heads, scale=scale)

    _fox_prompt_step(*prompt_in, op_ref, kaug, vbf, tq=tq, nq=nq, scale=scale,
                     alongside=sample_step)


def _fox(q, k, v, f_col, batch, seq, heads, page_table, q_rows, cache_k, cache_v, cache_lf,
         k_new, v_new, fn_col, fs_lane, tq=512):
    tq = min(tq, seq)
    nq = seq // tq
    nbs, n_pages = page_table.shape
    rows, hd = q_rows.shape[1], q_rows.shape[2]
    lf_rows = cache_lf.shape[1]
    page_rows = lf_rows * LANES
    assert heads & (heads - 1) == 0 and LANES % heads == 0
    pps = math.gcd(n_pages, FOX_PAGES_PER_STEP)
    n_steps = n_pages // pps
    n_live = nbs * n_steps
    assert n_live <= batch * heads * nq

    def walk(b, h, i):
        g = jnp.minimum((b * heads + h) * nq + i, n_live - 1)
        return g // n_steps, g % n_steps

    def per_bs(*tail):
        return lambda b, h, i, pt: (walk(b, h, i)[0],) + tail

    def page_of(c):
        def f(b, h, i, pt):
            bs, p = walk(b, h, i)
            return pt[bs, n_pages - 1 - (p * pps + c)]
        return f

    kv_specs = [pl.BlockSpec((page_rows, hd), lambda b, h, i, pt, f=page_of(c): (f(b, h, i, pt), 0))
                for c in range(pps)]
    lf_specs = [pl.BlockSpec((None, lf_rows, LANES),
                             lambda b, h, i, pt, f=page_of(c): (f(b, h, i, pt), 0, 0))
                for c in range(pps)]
    prompt_specs = [pl.BlockSpec((tq, HEAD_DIM), lambda b, h, i, pt: (b * nq + i, h)),
                    pl.BlockSpec((seq, HEAD_DIM), lambda b, h, i, pt: (b, h)),
                    pl.BlockSpec((seq, HEAD_DIM), lambda b, h, i, pt: (b, h)),
                    pl.BlockSpec((tq, LANES), lambda b, h, i, pt: (b * nq + i, 0)),
                    pl.BlockSpec((seq, LANES), lambda b, h, i, pt: (b, 0))]
    sample_specs = ([pl.BlockSpec((None, rows, hd), per_bs(0, 0))] + kv_specs + kv_specs + lf_specs
                    + [pl.BlockSpec((None, LANES, hd), per_bs(0, 0)),
                       pl.BlockSpec((None, LANES, hd), per_bs(0, 0)),
                       pl.BlockSpec((None, rows, 1), per_bs(0, 0)),
                       pl.BlockSpec((None, 1, LANES), per_bs(0, 0))])
    grid_spec = pltpu.PrefetchScalarGridSpec(
        num_scalar_prefetch=1,
        grid=(batch, heads, nq),
        in_specs=prompt_specs + sample_specs,
        out_specs=[pl.BlockSpec((tq, HEAD_DIM), lambda b, h, i, pt: (b * nq + i, h)),
                   pl.BlockSpec((None, rows, hd), per_bs(0, 0))],
        scratch_shapes=[pltpu.VMEM((seq, 2 * HEAD_DIM), BF16), pltpu.VMEM((seq, HEAD_DIM), BF16),
                        pltpu.VMEM((rows, 1), F32), pltpu.VMEM((rows, 1), F32),
                        pltpu.VMEM((rows, hd), F32), pltpu.VMEM((1, LANES), F32)],
    )
    return pl.pallas_call(
        functools.partial(_fox_kernel, n_prompt_in=len(prompt_specs),
                          n_sample_in=len(sample_specs), nq=nq, heads=heads, n_steps=n_steps,
                          n_live=n_live, tq=tq, pps=pps, scale=HEAD_DIM ** -0.5),
        grid_spec=grid_spec,
        out_shape=[jax.ShapeDtypeStruct(q.shape, BF16),
                   jax.ShapeDtypeStruct((nbs, rows, hd), F32)],
        compiler_params=_params("arbitrary", "arbitrary", "arbitrary"),
        name="fox",
    )(page_table, q, k, v, f_col, f_col, q_rows, *([cache_k] * pps), *([cache_v] * pps),
      *([cache_lf] * pps), k_new, v_new, fn_col, fs_lane)


def _sgu_kernel(u_ref, vg_ref, nw_ref, w_ref, bt_ref, *out_refs, groups):
    o_ref = out_refs[0]
    g = vg_ref[...].astype(F32)
    v = g * lax.rsqrt(jnp.mean(g * g, axis=-1, keepdims=True) + EPS) * nw_ref[...]
    if len(out_refs) > 1:
        out_refs[1][...] = v
    vb = v.astype(BF16)
    rows = v.shape[0]
    cw = v.shape[1] // groups
    r = lax.broadcasted_iota(jnp.int32, (rows, rows), 0)
    c = lax.broadcasted_iota(jnp.int32, (rows, rows), 1)
    bt = bt_ref[...]
    for gi in range(groups):
        wm = jnp.where(c <= r, w_ref[gi], 0.0).astype(BF16)
        z = jnp.dot(wm, vb[:, gi * cw:(gi + 1) * cw], preferred_element_type=F32)
        z = z + bt[:, gi:gi + 1]
        u = u_ref[:, gi * cw:(gi + 1) * cw].astype(F32)
        o_ref[:, gi * cw:(gi + 1) * cw] = (u * z).astype(o_ref.dtype)


def _sgu(uv, norm_w, w_pos, bias_t, rows, emit_v):
    m, two_w = uv.shape
    width = two_w // 2
    groups = w_pos.shape[0]
    out_shape = [jax.ShapeDtypeStruct((m, width), BF16)]
    out_specs = [pl.BlockSpec((rows, width), lambda i: (i, 0))]
    if emit_v:
        out_shape.append(jax.ShapeDtypeStruct((m, width), F32))
        out_specs.append(pl.BlockSpec((rows, width), lambda i: (i, 0)))
    return pl.pallas_call(
        functools.partial(_sgu_kernel, groups=groups),
        grid=(m // rows,),
        in_specs=[pl.BlockSpec((rows, width), lambda i: (i, 0)),
                  pl.BlockSpec((rows, width), lambda i: (i, 1)),
                  pl.BlockSpec((1, width), lambda i: (0, 0)),
                  pl.BlockSpec((groups, rows, rows), lambda i: (0, 0, 0)),
                  pl.BlockSpec((rows, groups), lambda i: (0, 0))],
        out_specs=out_specs,
        out_shape=out_shape,
        compiler_params=_params("arbitrary"),
        name="sgu",
    )(uv, uv, norm_w, w_pos, bias_t)


def _conv_sample_kernel(z0_ref, z1_ref, z2_ref, gb_ref, w_ref, o_ref):
    w = w_ref[...]
    y = w[0:1] * z0_ref[...] + w[1:2] * z1_ref[...] + w[2:3] * z2_ref[...]
    o_ref[...] = (gb_ref[...].astype(F32) * y).astype(o_ref.dtype)


def _conv_sample(z0, z1, z2, gb, conv_w):
    return pl.pallas_call(
        _conv_sample_kernel,
        out_shape=jax.ShapeDtypeStruct(z0.shape, BF16),
        name="conv_sample",
    )(z0, z1, z2, gb, conv_w)


def _gla_kernel(q_ref, k_ref, v_ref, la_ref, gate_ref, nw_ref, s0_ref, o_ref, st_ref, st, *, nc):
    ci = pl.program_id(1)

    @pl.when(ci == 0)
    def _():
        st[...] = s0_ref[...]

    heads, dv, dk = st.shape
    la = la_ref[...]
    c, width = la.shape
    r = lax.broadcasted_iota(jnp.int32, (c, c), 0)
    cc = lax.broadcasted_iota(jnp.int32, (c, c), 1)
    tril = cc <= r
    y = jnp.dot(jnp.where(tril, 1.0, 0.0).astype(BF16),
                jnp.concatenate(_split3(la), axis=1), preferred_element_type=F32)
    bc = y[:, :width] + y[:, width:2 * width] + y[:, 2 * width:]
    q = q_ref[...].astype(F32) * dk ** -0.5
    k = k_ref[...].astype(F32)
    qt = (q * jnp.exp(bc)).astype(BF16)
    mid = bc[c // 2 - 1:c // 2, :]
    qa = (q * jnp.exp(bc - mid)).astype(BF16)
    ka = (k * jnp.exp(mid - bc)).astype(BF16)
    bl = bc[c - 1:c, :]
    kd = (k * jnp.exp(bl - bc)).astype(BF16)
    decay = jnp.exp(bl)
    for h in range(heads):
        kcol = slice(h * dk, (h + 1) * dk)
        vcol = slice(h * dv, (h + 1) * dv)
        v = v_ref[:, vcol]
        att = lax.dot_general(qa[:, kcol], ka[:, kcol], (((1,), (1,)), ((), ())),
                              preferred_element_type=F32)
        att = jnp.where(tril, att, 0.0)
        s_t = st[h]
        o = (lax.dot_general(qt[:, kcol], s_t.astype(BF16), (((1,), (1,)), ((), ())),
                             preferred_element_type=F32)
             + jnp.dot(att.astype(BF16), v, preferred_element_type=F32))
        st[h] = s_t * decay[:, kcol] + lax.dot_general(v, kd[:, kcol], (((0,), (0,)), ((), ())),
                                                       preferred_element_type=F32)
        on = o * lax.rsqrt(jnp.mean(o * o, axis=-1, keepdims=True) + EPS) * nw_ref[:, vcol]
        o_ref[:, vcol] = (on * gate_ref[:, vcol].astype(F32)).astype(o_ref.dtype)

    @pl.when(ci == nc - 1)
    def _():
        st_ref[...] = st[...]


def _gla(qkv, log_a, gate, norm_w, s0_t, batch, seq, heads, chunk):
    m = qkv.shape[0]
    key_w, val_w = log_a.shape[1], gate.shape[1]
    dk, dv = key_w // heads, val_w // heads
    assert (2 * key_w) % val_w == 0
    v0 = (2 * key_w) // val_w
    chunk = min(chunk, seq)
    nc = seq // chunk
    row = lambda b, c: b * nc + c
    return pl.pallas_call(
        functools.partial(_gla_kernel, nc=nc),
        grid=(batch, nc),
        in_specs=[pl.BlockSpec((chunk, key_w), lambda b, c: (row(b, c), 0)),
                  pl.BlockSpec((chunk, key_w), lambda b, c: (row(b, c), 1)),
                  pl.BlockSpec((chunk, val_w), lambda b, c: (row(b, c), v0)),
                  pl.BlockSpec((chunk, key_w), lambda b, c: (row(b, c), 0)),
                  pl.BlockSpec((chunk, val_w), lambda b, c: (row(b, c), 0)),
                  pl.BlockSpec((1, val_w), lambda b, c: (0, 0)),
                  pl.BlockSpec((None, heads, dv, dk), lambda b, c: (b, 0, 0, 0))],
        out_specs=[pl.BlockSpec((chunk, val_w), lambda b, c: (row(b, c), 0)),
                   pl.BlockSpec((None, heads, dv, dk), lambda b, c: (b, 0, 0, 0))],
        out_shape=[jax.ShapeDtypeStruct((m, val_w), BF16),
                   jax.ShapeDtypeStruct((batch, heads, dv, dk), F32)],
        scratch_shapes=[pltpu.VMEM((heads, dv, dk), F32)],
        compiler_params=_params("arbitrary", "arbitrary"),
        name="gla",
    )(qkv, qkv, qkv, log_a, gate, norm_w, s0_t)


def kernel(x_prompt, x_sample, cache_k, cache_v, cache_logf, state_conv, state_gla, page_table,
           c_prompt, c_sample, w_in0, f_bias, q_norm, k_norm, sgu_norm, sgu_w, sgu_b, w_out0,
           w_in1, conv_w, gla_a_w2, gla_a_b, gla_norm, w_out1, ada_w, ada_b, mlp_w1, mlp_w2):
    nbp, seq, d = x_prompt.shape
    nbs, dseq, _ = x_sample.shape
    mp, ms = nbp * seq, nbs * dseq
    fox_w = d // 2
    heads = fox_w // HEAD_DIM
    sgu_width = d // 2
    groups = sgu_w.shape[1]
    conv_width = state_conv.shape[-1]
    gla_heads, gla_dk, gla_dv = state_gla.shape[2], state_gla.shape[3], state_gla.shape[4]
    gla_key, gla_val = gla_heads * gla_dk, gla_heads * gla_dv
    gla_rank = gla_a_w2.shape[1]
    n_pool, page = cache_k.shape[1], cache_k.shape[2]
    n_pages = page_table.shape[1]

    r_c = nbp + nbs
    r_pad = -(-r_c // 16) * 16
    c_rows = jnp.pad(jnp.concatenate([c_prompt, c_sample], axis=0),
                     ((0, r_pad - r_c), (0, 0))).astype(BF16)
    mod = _ada(c_rows, ada_w, ada_b)

    def mods(layer):
        parts = jnp.split(mod[layer], 6, axis=-1)
        pp = [p[:nbp].reshape(nbp, 1, d) for p in parts]
        ps = [jnp.repeat(p[nbp:r_c], dseq, axis=0) for p in parts]
        return pp, ps

    xp = x_prompt.reshape(mp, d)
    xs = x_sample.reshape(ms, d)

    def modulate_both(xp, xs, shp, scp, shs, scs):
        hp = _modulate(xp, shp, scp, seq)
        hs = _modulate(xs, shs.reshape(1, ms, d), scs.reshape(1, ms, d), ms, tr=ms)
        return hp, hs

    def mlp(xp, xs, layer, shp, scp, gp, shs, scs, gs):
        hp, hs = modulate_both(xp, xs, shp, scp, shs, scs)
        (ap,), (as_,) = _proj([hp], [hs], [(mlp_w1, layer, 0)], mlp_w1.shape[2], _epi_relu2,
                              [BF16], name="mlp_up")
        return _down(ap, as_, mlp_w2, layer, xp, gp, xs, gs, seq)

    def residual_proj(lhs_p, lhs_s, w, xp, xs, gp, gs, name):
        (yp,), (ys,) = _proj(lhs_p, lhs_s, [(w, 0, 0)], d, _epi_residual, [F32],
                             ext=[("res_p", "prow", xp), ("gate_p", "pgrp", gp),
                                  ("res_s", "srow", xs), ("gate_s", "srow", gs)],
                             rows_per_group=seq, name=name)
        return yp, ys

    (sh1p, sc1p, g1p, sh2p, sc2p, g2p), (sh1s, sc1s, g1s, sh2s, sc2s, g2s) = mods(0)
    hp, hs = modulate_both(xp, xs, sh1p, sc1p, sh1s, sc1s)
    w0 = jnp.swapaxes(w_in0, 1, 2)
    qn = jnp.tile(q_norm[0], heads).reshape(1, fox_w)
    kn = jnp.tile(k_norm[0], heads).reshape(1, fox_w)
    (q_p,), (q_s,) = _proj([hp], [hs], [(w0, 0, 0)], fox_w, _epi_head_norm, [BF16],
                           ext=[("norm", "col", qn)], w_rows=True, name="in0_q")
    (k_p,), (k_s,) = _proj([hp], [hs], [(w0, 0, fox_w)], fox_w, _epi_head_norm, [F32],
                           ext=[("norm", "col", kn)], w_rows=True, name="in0_k")
    (v_p,), (v_s,) = _proj([hp], [hs], [(w0, 0, 2 * fox_w)], fox_w, _epi_plain, [F32],
                           w_rows=True, name="in0_v")
    fb = jnp.pad(f_bias[0], (0, LANES - heads)).reshape(1, LANES)
    (lf_p,), (lf_s,) = _proj([hp], [hs], [(w0, 0, 3 * fox_w)], LANES, _epi_log_forget, [F32],
                             ext=[("bias", "col", fb)], w_rows=True, name="in0_logf")
    (uv_p,), (uv_s,) = _proj([hp], [hs], [(w0, 0, 3 * fox_w + heads)], 2 * sgu_width, _epi_gelu,
                             [BF16], w_rows=True, name="in0_uv")

    f_col = _cumsum_rows(lf_p, nbp)
    lf_s16 = lf_s[:, :heads].reshape(nbs, dseq, heads)
    tpad = 8
    lf_s_pad = jnp.pad(lf_s.reshape(nbs, dseq, LANES), ((0, 0), (0, tpad - dseq), (0, 0)))
    fn = _cumsum_rows(lf_s_pad.reshape(nbs * tpad, LANES), nbs).reshape(
        nbs, tpad, LANES)[:, :dseq, :heads]
    rows_s = dseq * heads
    assert rows_s <= LANES

    def new_rows(a):
        a = a.reshape(nbs, rows_s, HEAD_DIM)
        return jnp.pad(a, ((0, 0), (0, LANES - rows_s), (0, 0))).astype(BF16)

    fs_lane = jnp.pad(fn.reshape(nbs, 1, rows_s), ((0, 0), (0, 0), (0, LANES - rows_s)))
    oa_p, oa_rows = _fox(q_p, k_p, v_p, f_col, nbp, seq, heads,
                         page_table, q_s.reshape(nbs, rows_s, HEAD_DIM),
                         cache_k[0].reshape(n_pool * page * heads, HEAD_DIM),
                         cache_v[0].reshape(n_pool * page * heads, HEAD_DIM),
                         cache_logf[0].reshape(n_pool, page * heads // LANES, LANES),
                         new_rows(k_s), new_rows(v_s), fn.reshape(nbs, rows_s, 1), fs_lane)
    oa_s = oa_rows.reshape(ms, fox_w).astype(BF16)

    sn = sgu_norm[0].reshape(1, sgu_width)
    rows_p = min(seq, SGU_CHUNK)
    (ob_p,) = _sgu(uv_p, sn, sgu_w[0][:, :rows_p, :rows_p], sgu_b[0][:, :rows_p].T, rows_p, False)
    rows_g = min(dseq, SGU_CHUNK)
    w_small = sgu_w[0][:, :rows_g, :rows_g]
    w_big = jnp.einsum("ab,gts->gatbs", jnp.eye(nbs, dtype=F32), w_small).reshape(groups, ms, ms)
    b_big = jnp.tile(sgu_b[0][:, :rows_g].T, (nbs, 1))
    ob_s, sgu_v = _sgu(uv_s, sn, w_big, b_big, ms, True)

    xp, xs = residual_proj([oa_p, ob_p], [oa_s, ob_s], w_out0, xp, xs, g1p, g1s, "out0")
    xp, xs = mlp(xp, xs, 0, sh2p, sc2p, g2p, sh2s, sc2s, g2s)

    (sh1p, sc1p, g1p, sh2p, sc2p, g2p), (sh1s, sc1s, g1s, sh2s, sc2s, g2s) = mods(1)
    hp, hs = modulate_both(xp, xs, sh1p, sc1p, sh1s, sc1s)
    w1 = jnp.swapaxes(w_in1, 1, 2)
    cw = conv_width
    cwt = conv_w[0]
    n_tap = cwt.shape[0]
    (oc_p, ztail_p), (gb_s, z_s) = _proj(
        [hp], [hs], [(w1, 0, 0), (w1, 0, cw), (w1, 0, 2 * cw)], cw, _make_epi_conv(seq),
        [BF16, F32], ext=[("taps", "col", cwt)], tm=512, tn=256, rows_per_group=seq, w_rows=True,
        tail_rows=(1,), scratch=lambda tm, tn: [pltpu.VMEM((tm + 8, tn), F32)], name="in1_conv")
    conv_p = ztail_p.reshape(nbp, -1, 8, cw)[:, -1, 8 - (n_tap - 1):, :]
    (qkv_p,), (qkv_s,) = _proj([hp], [hs], [(w1, 0, 3 * cw)], 2 * gla_key + gla_val, _epi_plain,
                               [BF16], w_rows=True, name="in1_qkv")
    c_ga = 3 * cw + 2 * gla_key + gla_val
    (ga_p,), (ga_s,) = _proj([hp], [hs], [(w1, 0, c_ga)], LANES, _epi_plain, [BF16],
                             w_rows=True, name="in1_ga")
    w_a2 = jnp.pad(gla_a_w2, ((0, 0), (0, LANES - gla_rank), (0, 0)))
    (la_p,), (la_s,) = _proj([ga_p], [ga_s], [(w_a2, 0, 0)], gla_key, _epi_log_decay, [F32],
                             ext=[("bias", "col", gla_a_b[0].reshape(1, gla_key))], name="in1_log_a")
    (gg_p,), (gg_s,) = _proj([hp], [hs], [(w1, 0, c_ga + gla_rank)], gla_val, _epi_silu, [BF16],
                             w_rows=True, name="in1_gate")

    zp_s = jnp.concatenate([state_conv[0], z_s.reshape(nbs, dseq, cw)], axis=1)
    shifted = [zp_s[:, i:i + dseq].reshape(ms, cw) for i in range(cwt.shape[0])]
    oc_s = _conv_sample(shifted[0], shifted[1], shifted[2], gb_s, cwt)
    conv_s = zp_s[:, dseq:, :]

    gn = gla_norm[0].reshape(1, gla_val)
    s0_p = jnp.zeros((nbp, gla_heads, gla_dv, gla_dk), F32)
    od_p, st_p = _gla(qkv_p, la_p, gg_p, gn, s0_p, nbp, seq, gla_heads, GLA_CHUNK_ROWS)
    pad_t = 16

    def pad_rows(a):
        return jnp.pad(a.reshape(nbs, dseq, -1), ((0, 0), (0, pad_t - dseq), (0, 0))).reshape(
            nbs * pad_t, -1)

    od_s_pad, st_s = _gla(pad_rows(qkv_s), pad_rows(la_s), pad_rows(gg_s), gn,
                          state_gla[0].swapaxes(-1, -2), nbs, pad_t, gla_heads, pad_t)
    od_s = od_s_pad.reshape(nbs, pad_t, gla_val)[:, :dseq].reshape(ms, gla_val)

    xp, xs = residual_proj([oc_p, od_p], [oc_s, od_s], w_out1, xp, xs, g1p, g1s, "out1")
    xp, xs = mlp(xp, xs, 1, sh2p, sc2p, g2p, sh2s, sc2s, g2s)

    y_prompt = xp.reshape(nbp, seq, d)
    y_sample = xs.reshape(nbs, dseq, d)
    return (y_prompt, y_sample,
            _to_heads(k_p, heads).reshape(1, nbp, seq, heads, HEAD_DIM),
            _to_heads(v_p, heads).reshape(1, nbp, seq, heads, HEAD_DIM),
            lf_p[:, :heads].reshape(1, nbp, seq, heads),
            k_s.reshape(1, nbs, dseq, heads, HEAD_DIM), v_s.reshape(1, nbs, dseq, heads, HEAD_DIM),
            lf_s16[None], sgu_v.reshape(1, nbs, dseq, sgu_width),
            conv_p[None], conv_s[None],
            st_p.swapaxes(-1, -2)[None], st_s.swapaxes(-1, -2)[None])
```

```python
import functools
import math

import jax
import jax.numpy as jnp
from jax import lax
from jax.experimental import pallas as pl
from jax.experimental.pallas import tpu as pltpu

F32 = jnp.float32
BF16 = jnp.bfloat16
EPS = 1e-6
NEG_INF = -1e30
HEAD_DIM = 128
LANES = 128
SGU_CHUNK = 128
GLA_TAU = 16.0
GLA_CHUNK_ROWS = 128
FOX_PAGES_PER_STEP = 4
VMEM_LIMIT_BYTES = 56 * 1024 * 1024


def _params(*semantics):
    return pltpu.CompilerParams(dimension_semantics=semantics,
                                vmem_limit_bytes=VMEM_LIMIT_BYTES)


def _split3(x):
    hi = x.astype(BF16)
    r = x - hi.astype(F32)
    mid = r.astype(BF16)
    lo = (r - mid.astype(F32)).astype(BF16)
    return hi, mid, lo


def _log_sigmoid(x):
    return jnp.minimum(x, 0.0) - jnp.log1p(jnp.exp(-jnp.abs(x)))


def _gelu_tanh(x):
    c = 0.7978845608028654
    return 0.5 * x * (1.0 + jnp.tanh(c * (x + 0.044715 * (x * x * x))))


def _ada_kernel(c_ref, w_ref, b_ref, o_ref):
    o_ref[...] = jnp.dot(c_ref[...], w_ref[...].astype(BF16),
                         preferred_element_type=F32) + b_ref[...]


def _ada(c_rows, ada_w, ada_b, tn=512):
    n_layers, d, n = ada_w.shape
    r = c_rows.shape[0]
    tn = min(tn, n)
    return pl.pallas_call(
        _ada_kernel,
        grid=(n_layers, n // tn),
        in_specs=[pl.BlockSpec((r, d), lambda l, j: (0, 0)),
                  pl.BlockSpec((None, d, tn), lambda l, j: (l, 0, j)),
                  pl.BlockSpec((None, 1, tn), lambda l, j: (l, 0, j))],
        out_specs=pl.BlockSpec((None, r, tn), lambda l, j: (l, 0, j)),
        out_shape=jax.ShapeDtypeStruct((n_layers, r, n), F32),
        compiler_params=_params("arbitrary", "arbitrary"),
        name="ada",
    )(c_rows, ada_w, ada_b.reshape(n_layers, 1, n))


def _modulate_kernel(x_ref, sh_ref, sc_ref, o_ref):
    x = x_ref[...]
    y = x * lax.rsqrt(jnp.mean(x * x, axis=-1, keepdims=True) + EPS)
    o_ref[...] = (y * (1.0 + sc_ref[...]) + sh_ref[...]).astype(o_ref.dtype)


def _modulate(x, sh, sc, rows_per_group, tr=256):
    rows, d = x.shape
    tr = min(tr, rows)
    r = sh.shape[1]
    grp = lambda i: ((i * tr) // rows_per_group, 0, 0)
    return pl.pallas_call(
        _modulate_kernel,
        grid=(rows // tr,),
        in_specs=[pl.BlockSpec((tr, d), lambda i: (i, 0)),
                  pl.BlockSpec((None, r, d), grp),
                  pl.BlockSpec((None, r, d), grp)],
        out_specs=pl.BlockSpec((tr, d), lambda i: (i, 0)),
        out_shape=jax.ShapeDtypeStruct((rows, d), BF16),
        compiler_params=_params("arbitrary"),
        name="modulate",
    )(x, sh, sc)


def _proj_kernel(*refs, k_sizes, n_w, ext_names, n_out, epilogue, cast_rows, w_rows, n_scratch):
    n_lhs = len(k_sizes)
    pos = 0
    xp = refs[pos:pos + n_lhs]; pos += n_lhs
    xs = refs[pos:pos + n_lhs]; pos += n_lhs
    w = refs[pos:pos + n_w]; pos += n_w
    ext = dict(zip(ext_names, refs[pos:pos + len(ext_names)])); pos += len(ext_names)
    out_p = refs[pos:pos + n_out]; pos += n_out
    out_s = refs[pos:pos + n_out]; pos += n_out
    wbf = refs[pos:pos + n_w]; pos += n_w
    ext["scratch"] = refs[pos:pos + n_scratch]

    def accumulate(lhs):
        accs = []
        for m in range(n_w):
            acc = None
            off = 0
            for a, ka in enumerate(k_sizes):
                part = jnp.dot(lhs[a][...], wbf[m][off:off + ka, :],
                               preferred_element_type=F32)
                acc = part if acc is None else acc + part
                off += ka
            accs.append(acc)
        return accs

    def store(out_refs, vals):
        for o_ref, val in zip(out_refs, vals):
            o_ref[...] = val.astype(o_ref.dtype)

    @pl.when(pl.program_id(1) == 0)
    def _():
        accs_p, accs_s = [None] * n_w, [None] * n_w
        r0 = 0
        for a, ka in enumerate(k_sizes):
            for loc in range(0, ka, cast_rows):
                for m in range(n_w):
                    if w_rows:
                        chunk = w[m][:, r0:r0 + cast_rows].T.astype(BF16)
                    else:
                        chunk = w[m][r0:r0 + cast_rows, :].astype(BF16)
                    wbf[m][r0:r0 + cast_rows, :] = chunk
                    pp = jnp.dot(xp[a][:, loc:loc + cast_rows], chunk, preferred_element_type=F32)
                    ps = jnp.dot(xs[a][:, loc:loc + cast_rows], chunk, preferred_element_type=F32)
                    accs_p[m] = pp if accs_p[m] is None else accs_p[m] + pp
                    accs_s[m] = ps if accs_s[m] is None else accs_s[m] + ps
                r0 += cast_rows
        store(out_s, epilogue(accs_s, ext, True))
        store(out_p, epilogue(accs_p, ext, False))

    @pl.when(pl.program_id(1) != 0)
    def _():
        store(out_p, epilogue(accumulate(xp), ext, False))


def _proj(xp, xs, weights, n_cols, epilogue, out_dtypes, ext=(), *, tm=1024, tn=512,
          rows_per_group=None, w_rows=False, tail_rows=(), scratch=None, name="proj"):
    mp, ms = xp[0].shape[0], xs[0].shape[0]
    k_sizes = tuple(int(a.shape[1]) for a in xp)
    k_total = sum(k_sizes)
    tm = min(tm, mp, rows_per_group or mp)
    tn = min(tn, n_cols)
    assert mp % tm == 0 and n_cols % tn == 0
    for warr, _, c0 in weights:
        assert warr.ndim == 3
        if w_rows:
            assert warr.shape[2] == k_total and c0 % 8 == 0 and len(k_sizes) == 1
        else:
            assert warr.shape[1] == k_total and c0 % tn == 0
    in_specs, args = [], []
    for a in xp:
        in_specs.append(pl.BlockSpec((tm, a.shape[1]), lambda j, i: (i, 0)))
        args.append(a)
    for a in xs:
        in_specs.append(pl.BlockSpec((ms, a.shape[1]), lambda j, i: (0, 0)))
        args.append(a)
    for warr, layer, c0 in weights:
        if w_rows:
            spec = pl.BlockSpec((None, pl.Element(tn), pl.Element(k_total)),
                                lambda j, i, c=c0, l=layer: (l, pl.multiple_of(c + j * tn, 8), 0))
        else:
            spec = pl.BlockSpec((None, k_total, tn),
                                lambda j, i, cb=c0 // tn, l=layer: (l, 0, j + cb))
        in_specs.append(spec)
        args.append(warr)
    ext_names = []
    for ename, kind, arr in ext:
        ext_names.append(ename)
        if kind == "col":
            spec = pl.BlockSpec((arr.shape[0], tn), lambda j, i: (0, j))
        elif kind == "prow":
            spec = pl.BlockSpec((tm, tn), lambda j, i: (i, j))
        elif kind == "pgrp":
            spec = pl.BlockSpec((None, 1, tn),
                                lambda j, i: ((i * tm) // rows_per_group, 0, j))
        elif kind == "srow":
            spec = pl.BlockSpec((ms, tn), lambda j, i: (0, j))
        else:
            raise ValueError(kind)
        in_specs.append(spec)
        args.append(arr)
    n_out = len(out_dtypes)
    rows_p = [(8 * (mp // tm), 8) if o in tail_rows else (mp, tm) for o in range(n_out)]
    out_specs = ([pl.BlockSpec((br, tn), lambda j, i: (i, j)) for _, br in rows_p]
                 + [pl.BlockSpec((ms, tn), lambda j, i: (0, j))] * n_out)
    out_shape = ([jax.ShapeDtypeStruct((r, n_cols), dt) for (r, _), dt in zip(rows_p, out_dtypes)]
                 + [jax.ShapeDtypeStruct((ms, n_cols), dt) for dt in out_dtypes])
    cast_rows = min(512, *k_sizes)
    assert all(ka % cast_rows == 0 for ka in k_sizes)
    extra_scratch = scratch(tm, tn) if scratch else []
    outs = pl.pallas_call(
        functools.partial(_proj_kernel, k_sizes=k_sizes, n_w=len(weights),
                          ext_names=tuple(ext_names), n_out=n_out, epilogue=epilogue,
                          cast_rows=cast_rows, w_rows=w_rows, n_scratch=len(extra_scratch)),
        grid=(n_cols // tn, mp // tm),
        in_specs=in_specs,
        out_specs=out_specs,
        out_shape=out_shape,
        scratch_shapes=[pltpu.VMEM((k_total, tn), BF16) for _ in weights] + extra_scratch,
        compiler_params=_params("arbitrary", "arbitrary"),
        name=name,
    )(*args)
    return outs[:n_out], outs[n_out:]


def _epi_plain(accs, ext, is_sample):
    return (accs[0],)


def _epi_head_norm(accs, ext, is_sample):
    acc = accs[0]
    wn = ext["norm"][...]
    parts = []
    for c0 in range(0, acc.shape[1], HEAD_DIM):
        a = acc[:, c0:c0 + HEAD_DIM]
        y = a * lax.rsqrt(jnp.mean(a * a, axis=-1, keepdims=True) + EPS)
        parts.append(y * wn[:, c0:c0 + HEAD_DIM])
    return (jnp.concatenate(parts, axis=1),)


def _epi_log_forget(accs, ext, is_sample):
    return (_log_sigmoid(accs[0] + ext["bias"][...]),)


def _epi_gelu(accs, ext, is_sample):
    return (_gelu_tanh(accs[0]),)


def _epi_silu(accs, ext, is_sample):
    a = accs[0]
    return (a * jax.nn.sigmoid(a),)


def _epi_relu2(accs, ext, is_sample):
    r = jnp.maximum(accs[0], 0.0)
    return (r * r,)


def _make_epi_conv(seq):
    def epilogue(accs, ext, is_sample):
        z = accs[1] * accs[0]
        if is_sample:
            return accs[2], z
        (zbuf,) = ext["scratch"]
        tm = z.shape[0]

        @pl.when(pl.program_id(1) % (seq // tm) == 0)
        def _():
            zbuf[0:8, :] = jnp.zeros((8, zbuf.shape[1]), F32)

        zbuf[8:8 + tm, :] = z
        w = ext["taps"][...]
        y = w[0:1] * zbuf[6:6 + tm, :] + w[1:2] * zbuf[7:7 + tm, :] + w[2:3] * z
        tail = z[tm - 8:tm, :]
        zbuf[0:8, :] = tail
        return accs[2] * y, tail

    return epilogue


def _epi_log_decay(accs, ext, is_sample):
    return (_log_sigmoid(accs[0] + ext["bias"][...]) / GLA_TAU,)


def _epi_residual(accs, ext, is_sample):
    if is_sample:
        return (ext["res_s"][...] + ext["gate_s"][...] * accs[0],)
    return (ext["res_p"][...] + ext["gate_p"][...] * accs[0],)


def _down_kernel(xp_ref, xs_ref, w_ref, resp_ref, gp_ref, ress_ref, gs_ref,
                 op_ref, os_ref, accs_ref, wbf, *, nk, cast_rows):
    i = pl.program_id(1)
    k = pl.program_id(2)

    @pl.when(k == 0)
    def _():
        op_ref[...] = jnp.zeros_like(op_ref)

    total = None
    for r0 in range(0, w_ref.shape[0], cast_rows):
        wbf[r0:r0 + cast_rows, :] = w_ref[r0:r0 + cast_rows, :].astype(BF16)
        part = jnp.dot(xp_ref[:, r0:r0 + cast_rows], wbf[r0:r0 + cast_rows, :],
                       preferred_element_type=F32)
        total = part if total is None else total + part
    op_ref[...] += total

    @pl.when(k == nk - 1)
    def _():
        op_ref[...] = resp_ref[...] + gp_ref[...] * op_ref[...]

    @pl.when(i == 0)
    def _():
        ps = jnp.dot(xs_ref[...], wbf[...], preferred_element_type=F32)

        @pl.when(k == 0)
        def _():
            accs_ref[...] = ps

        @pl.when(k > 0)
        def _():
            accs_ref[...] += ps

        @pl.when(k == nk - 1)
        def _():
            os_ref[...] = ress_ref[...] + gs_ref[...] * accs_ref[...]


def _down(xp, xs, w, layer, res_p, gate_p, res_s, gate_s, rows_per_group, *,
          tm=1024, tn=1024, tk=2048):
    mp, kdim = xp.shape
    ms = xs.shape[0]
    n = w.shape[2]
    tm, tn, tk = min(tm, rows_per_group, mp), min(tn, n), min(tk, kdim)
    nk = kdim // tk
    cast_rows = min(512, tk)
    assert tk % cast_rows == 0
    return pl.pallas_call(
        functools.partial(_down_kernel, nk=nk, cast_rows=cast_rows),
        grid=(n // tn, mp // tm, nk),
        in_specs=[pl.BlockSpec((tm, tk), lambda j, i, k: (i, k)),
                  pl.BlockSpec((ms, tk), lambda j, i, k: (0, k)),
                  pl.BlockSpec((None, tk, tn), lambda j, i, k: (layer, k, j)),
                  pl.BlockSpec((tm, tn), lambda j, i, k: (i, j)),
                  pl.BlockSpec((None, 1, tn),
                               lambda j, i, k: ((i * tm) // rows_per_group, 0, j)),
                  pl.BlockSpec((ms, tn), lambda j, i, k: (0, j)),
                  pl.BlockSpec((ms, tn), lambda j, i, k: (0, j))],
        out_specs=[pl.BlockSpec((tm, tn), lambda j, i, k: (i, j)),
                   pl.BlockSpec((ms, tn), lambda j, i, k: (0, j))],
        out_shape=[jax.ShapeDtypeStruct((mp, n), F32),
                   jax.ShapeDtypeStruct((ms, n), F32)],
        scratch_shapes=[pltpu.VMEM((ms, tn), F32), pltpu.VMEM((tk, tn), BF16)],
        compiler_params=_params("arbitrary", "arbitrary", "arbitrary"),
        name="mlp_down",
    )(xp, xs, w, res_p, gate_p, res_s, gate_s)


def _cumsum_kernel(x_ref, o_ref, carry_ref):
    @pl.when(pl.program_id(1) == 0)
    def _():
        carry_ref[...] = jnp.zeros_like(carry_ref)

    x = x_ref[...]
    t, w = x.shape
    r = lax.broadcasted_iota(jnp.int32, (t, t), 0)
    c = lax.broadcasted_iota(jnp.int32, (t, t), 1)
    tri = jnp.where(c <= r, 1.0, 0.0).astype(BF16)
    y = jnp.dot(tri, jnp.concatenate(_split3(x), axis=1), preferred_element_type=F32)
    out = y[:, :w] + y[:, w:2 * w] + y[:, 2 * w:] + carry_ref[...]
    o_ref[...] = out
    carry_ref[...] = out[t - 1:t, :]


def _cumsum_rows(x, n_groups, tc=256):
    rows, w = x.shape
    per = rows // n_groups
    tc = min(tc, per)
    nt = per // tc
    return pl.pallas_call(
        _cumsum_kernel,
        grid=(n_groups, nt),
        in_specs=[pl.BlockSpec((tc, w), lambda b, t: (b * nt + t, 0))],
        out_specs=pl.BlockSpec((tc, w), lambda b, t: (b * nt + t, 0)),
        out_shape=jax.ShapeDtypeStruct((rows, w), F32),
        scratch_shapes=[pltpu.VMEM((1, w), F32)],
        compiler_params=_params("arbitrary", "arbitrary"),
        name="cumsum_logf",
    )(x)


def _bias_lanes(col, own_first):
    hi, mid, lo = _split3(col)
    lane = lax.broadcasted_iota(jnp.int32, (col.shape[0], LANES), 1)
    own, other = (0, 3) if own_first else (3, 0)
    x = jnp.where(lane == own, hi.astype(F32),
                  jnp.where(lane == own + 1, mid.astype(F32),
                            jnp.where(lane == own + 2, lo.astype(F32), 0.0)))
    x = jnp.where((lane >= other) & (lane < other + 3), 1.0, x)
    return x.astype(BF16)


def _fox_prompt_step(q_ref, k_ref, v_ref, fq_ref, fk_ref, o_ref, k5_hbm, v5_hbm, kaug, vbf, sem,
                     *, tq, nq, scale, alongside):
    b = pl.program_id(0)
    h = pl.program_id(1)
    qi = pl.program_id(2)
    seq = k_ref.shape[0]

    def head_copy(src_ref, dst_hbm, slot):
        return pltpu.make_async_copy(src_ref, dst_hbm.at[pl.ds(b * seq, seq), h], sem.at[slot])

    def head_column(f):
        lane = lax.broadcasted_iota(jnp.int32, f.shape, 1)
        return jnp.sum(jnp.where(lane == h, f, 0.0), axis=1, keepdims=True)

    @pl.when(qi == 0)
    def _():
        head_copy(k_ref, k5_hbm, 0).start()
        head_copy(v_ref, v5_hbm, 1).start()
        kaug[:, :HEAD_DIM] = k_ref[...].astype(BF16)
        kaug[:, HEAD_DIM:] = _bias_lanes(head_column(fk_ref[...]) * (-1.0 / scale), False)
        vbf[...] = v_ref[...].astype(BF16)

    q = jnp.concatenate(
        [q_ref[...], _bias_lanes(head_column(fq_ref[...]) * (1.0 / scale), True)], axis=1)
    c2 = scale * 1.4426950408889634

    def step(j, carry, masked):
        m, l, acc = carry
        start = j * tq
        s = lax.dot_general(q, kaug[pl.ds(start, tq), :], (((1,), (1,)), ((), ())),
                            preferred_element_type=F32) * c2
        if masked:
            r = lax.broadcasted_iota(jnp.int32, (tq, tq), 0)
            c = lax.broadcasted_iota(jnp.int32, (tq, tq), 1)
            s = jnp.where(c <= r, s, NEG_INF)
        m_new = jnp.maximum(m, jnp.max(s, axis=1, keepdims=True))
        alpha = jnp.exp2(m - m_new)
        p = jnp.exp2(s - m_new)
        l = alpha * l + jnp.sum(p, axis=1, keepdims=True)
        acc = alpha * acc + jnp.dot(p.astype(BF16), vbf[pl.ds(start, tq), :],
                                    preferred_element_type=F32)
        return m_new, l, acc

    init = (jnp.full((tq, 1), NEG_INF, F32), jnp.zeros((tq, 1), F32),
            jnp.zeros((tq, HEAD_DIM), F32))
    for n_full in range(nq):
        @pl.when(qi == n_full)
        def _(n_full=n_full):
            alongside()
            carry = init
            for j in range(n_full):
                carry = step(j, carry, False)
            _, l, acc = step(n_full, carry, True)
            o_ref[...] = (acc / l).astype(o_ref.dtype)
            if n_full == 0:
                head_copy(k_ref, k5_hbm, 0).wait()
                head_copy(v_ref, v5_hbm, 1).wait()


def _fox_sample_step(p, live, q_ref, *refs, n_steps, pps, heads, scale):
    ck_refs, cv_refs, lf_refs = refs[:pps], refs[pps:2 * pps], refs[2 * pps:3 * pps]
    kn_ref, vn_ref, fn_ref, fs_ref, o_ref, m_ref, l_ref, acc_ref, tail_ref = refs[3 * pps:]
    first = p == 0

    def carried(ref, start):
        return jnp.where(first, start, ref[...])

    def keep(ref, new, old):
        ref[...] = jnp.where(live, new, old)

    q = q_ref[...]
    rows = q.shape[0]
    fn = fn_ref[...]
    row = lax.broadcasted_iota(jnp.int32, (rows, LANES), 0)
    lane = lax.broadcasted_iota(jnp.int32, (rows, LANES), 1)
    own_head = (lane & (heads - 1)) == (row & (heads - 1))

    log2e = 1.4426950408889634

    def attend(keys, vals, lane_bias, visible, paged):
        s = lax.dot_general(q, keys, (((1,), (1,)), ((), ())), preferred_element_type=F32)
        bias2 = lane_bias * log2e
        base = jnp.where(visible, fn * log2e, NEG_INF)
        s = jnp.concatenate(
            [s[:, g * LANES:(g + 1) * LANES] * (scale * log2e) + (base + bias2[g:g + 1, :])
             for g in range(keys.shape[0] // LANES)], axis=1)
        if paged:
            m_old, l_old, acc_old = (carried(m_ref, NEG_INF), carried(l_ref, 0.0),
                                     carried(acc_ref, 0.0))
        else:
            m_old, l_old, acc_old = m_ref[...], l_ref[...], acc_ref[...]
        m_new = jnp.maximum(m_old, jnp.max(s, axis=1, keepdims=True))
        alpha = jnp.exp2(m_old - m_new)
        pr = jnp.exp2(s - m_new)
        l_new = alpha * l_old + jnp.sum(pr, axis=1, keepdims=True)
        acc_new = alpha * acc_old + jnp.dot(pr.astype(BF16), vals, preferred_element_type=F32)
        if paged:
            keep(m_ref, m_new, m_ref[...])
            keep(l_ref, l_new, l_ref[...])
            keep(acc_ref, acc_new, acc_ref[...])
        else:
            m_ref[...], l_ref[...], acc_ref[...] = m_new, l_new, acc_new

    a = lax.broadcasted_iota(jnp.int32, (LANES, LANES), 0)
    b = lax.broadcasted_iota(jnp.int32, (LANES, LANES), 1)
    same_head = (a & (heads - 1)) == (b & (heads - 1))
    later_in_row = jnp.where(same_head & (a > b), 1.0, 0.0).astype(BF16)
    whole_row = jnp.where(same_head, 1.0, 0.0).astype(BF16)
    sums = jnp.concatenate([later_in_row, whole_row], axis=1)
    n_rows = lf_refs[0].shape[0]
    ra = lax.broadcasted_iota(jnp.int32, (n_rows, n_rows), 0)
    rb = lax.broadcasted_iota(jnp.int32, (n_rows, n_rows), 1)
    later_rows = jnp.where(rb > ra, 1.0, 0.0).astype(BF16)

    tail = carried(tail_ref, 0.0)
    biases = []
    for c in range(pps):
        lf = lf_refs[c][...]
        y = jnp.dot(jnp.concatenate(_split3(lf), axis=0), sums, preferred_element_type=F32)
        y = y[:n_rows] + y[n_rows:2 * n_rows] + y[2 * n_rows:]
        within, row_tot = y[:, :LANES], y[:, LANES:]
        z = jnp.dot(later_rows, jnp.concatenate(_split3(row_tot), axis=1),
                    preferred_element_type=F32)
        biases.append(within + z[:, :LANES] + z[:, LANES:2 * LANES] + z[:, 2 * LANES:] + tail)
        tail = tail + jnp.sum(row_tot, axis=0, keepdims=True)
    keep(tail_ref, tail, tail_ref[...])

    attend(jnp.concatenate([r[...].astype(BF16) for r in ck_refs], axis=0),
           jnp.concatenate([r[...].astype(BF16) for r in cv_refs], axis=0),
           jnp.concatenate(biases, axis=0), own_head, True)

    @pl.when((p == n_steps - 1) & live)
    def _():
        attend(kn_ref[...], vn_ref[...], -fs_ref[...],
               own_head & (lane - (lane & (heads - 1)) <= row - (row & (heads - 1))), False)
        o_ref[...] = acc_ref[...] / l_ref[...]


def _fox_kernel(pt_ref, *refs, n_prompt_in, n_sample_in, nq, heads, n_steps, n_live, tq, pps, scale):
    del pt_ref
    prompt_in = refs[:n_prompt_in]
    sample_in = refs[n_prompt_in:n_prompt_in + n_sample_in]
    (op_ref, os_ref, k5_hbm, v5_hbm, kaug, vbf, m_ref, l_ref, acc_ref, tail_ref,
     sem) = refs[n_prompt_in + n_sample_in:]
    g = (pl.program_id(0) * heads + pl.program_id(1)) * nq + pl.program_id(2)

    @pl.when(g == 0)
    def _():
        m_ref[...] = jnp.full(m_ref.shape, NEG_INF, F32)
        l_ref[...] = jnp.zeros_like(l_ref)
        acc_ref[...] = jnp.zeros_like(acc_ref)
        tail_ref[...] = jnp.zeros_like(tail_ref)

    live = g < n_live
    p = jnp.minimum(g, n_live - 1) % n_steps

    def sample_step():
        _fox_sample_step(p, live, *sample_in, os_ref, m_ref, l_ref, acc_ref, tail_ref,
                         n_steps=n_steps, pps=pps, heads=heads, scale=scale)

    _fox_prompt_step(*prompt_in, op_ref, k5_hbm, v5_hbm, kaug, vbf, sem, tq=tq, nq=nq,
                     scale=scale, alongside=sample_step)


def _fox(q, k, v, f_col, batch, seq, heads, page_table, q_rows, cache_k, cache_v, cache_lf,
         k_new, v_new, fn_col, fs_lane, tq=512):
    tq = min(tq, seq)
    nq = seq // tq
    nbs, n_pages = page_table.shape
    rows, hd = q_rows.shape[1], q_rows.shape[2]
    lf_rows = cache_lf.shape[1]
    page_rows = lf_rows * LANES
    assert heads & (heads - 1) == 0 and LANES % heads == 0
    pps = math.gcd(n_pages, FOX_PAGES_PER_STEP)
    n_steps = n_pages // pps
    n_live = nbs * n_steps
    assert n_live <= batch * heads * nq

    def walk(b, h, i):
        g = jnp.minimum((b * heads + h) * nq + i, n_live - 1)
        return g // n_steps, g % n_steps

    def per_bs(*tail):
        return lambda b, h, i, pt: (walk(b, h, i)[0],) + tail

    def page_of(c):
        def f(b, h, i, pt):
            bs, p = walk(b, h, i)
            return pt[bs, n_pages - 1 - (p * pps + c)]
        return f

    kv_specs = [pl.BlockSpec((page_rows, hd), lambda b, h, i, pt, f=page_of(c): (f(b, h, i, pt), 0))
                for c in range(pps)]
    lf_specs = [pl.BlockSpec((None, lf_rows, LANES),
                             lambda b, h, i, pt, f=page_of(c): (f(b, h, i, pt), 0, 0))
                for c in range(pps)]
    prompt_specs = [pl.BlockSpec((tq, HEAD_DIM), lambda b, h, i, pt: (b * nq + i, h)),
                    pl.BlockSpec((seq, HEAD_DIM), lambda b, h, i, pt: (b, h)),
                    pl.BlockSpec((seq, HEAD_DIM), lambda b, h, i, pt: (b, h)),
                    pl.BlockSpec((tq, LANES), lambda b, h, i, pt: (b * nq + i, 0)),
                    pl.BlockSpec((seq, LANES), lambda b, h, i, pt: (b, 0))]
    sample_specs = ([pl.BlockSpec((None, rows, hd), per_bs(0, 0))] + kv_specs + kv_specs + lf_specs
                    + [pl.BlockSpec((None, LANES, hd), per_bs(0, 0)),
                       pl.BlockSpec((None, LANES, hd), per_bs(0, 0)),
                       pl.BlockSpec((None, rows, 1), per_bs(0, 0)),
                       pl.BlockSpec((None, 1, LANES), per_bs(0, 0))])
    grid_spec = pltpu.PrefetchScalarGridSpec(
        num_scalar_prefetch=1,
        grid=(batch, heads, nq),
        in_specs=prompt_specs + sample_specs,
        out_specs=[pl.BlockSpec((tq, HEAD_DIM), lambda b, h, i, pt: (b * nq + i, h)),
                   pl.BlockSpec((None, rows, hd), per_bs(0, 0)),
                   pl.BlockSpec(memory_space=pl.ANY), pl.BlockSpec(memory_space=pl.ANY)],
        scratch_shapes=[pltpu.VMEM((seq, 2 * HEAD_DIM), BF16), pltpu.VMEM((seq, HEAD_DIM), BF16),
                        pltpu.VMEM((rows, 1), F32), pltpu.VMEM((rows, 1), F32),
                        pltpu.VMEM((rows, hd), F32), pltpu.VMEM((1, LANES), F32),
                        pltpu.SemaphoreType.DMA((2,))],
    )
    kv_heads = jax.ShapeDtypeStruct((batch * seq, heads, HEAD_DIM), F32)
    return pl.pallas_call(
        functools.partial(_fox_kernel, n_prompt_in=len(prompt_specs),
                          n_sample_in=len(sample_specs), nq=nq, heads=heads, n_steps=n_steps,
                          n_live=n_live, tq=tq, pps=pps, scale=HEAD_DIM ** -0.5),
        grid_spec=grid_spec,
        out_shape=[jax.ShapeDtypeStruct(q.shape, BF16),
                   jax.ShapeDtypeStruct((nbs, rows, hd), F32), kv_heads, kv_heads],
        compiler_params=_params("arbitrary", "arbitrary", "arbitrary"),
        name="fox",
    )(page_table, q, k, v, f_col, f_col, q_rows, *([cache_k] * pps), *([cache_v] * pps),
      *([cache_lf] * pps), k_new, v_new, fn_col, fs_lane)


def _sgu_kernel(u_ref, vg_ref, nw_ref, w_ref, bt_ref, *out_refs, groups):
    o_ref = out_refs[0]
    g = vg_ref[...].astype(F32)
    v = g * lax.rsqrt(jnp.mean(g * g, axis=-1, keepdims=True) + EPS) * nw_ref[...]
    if len(out_refs) > 1:
        out_refs[1][...] = v
    vb = v.astype(BF16)
    rows = v.shape[0]
    cw = v.shape[1] // groups
    r = lax.broadcasted_iota(jnp.int32, (rows, rows), 0)
    c = lax.broadcasted_iota(jnp.int32, (rows, rows), 1)
    bt = bt_ref[...]
    for gi in range(groups):
        wm = jnp.where(c <= r, w_ref[gi], 0.0).astype(BF16)
        z = jnp.dot(wm, vb[:, gi * cw:(gi + 1) * cw], preferred_element_type=F32)
        z = z + bt[:, gi:gi + 1]
        u = u_ref[:, gi * cw:(gi + 1) * cw].astype(F32)
        o_ref[:, gi * cw:(gi + 1) * cw] = (u * z).astype(o_ref.dtype)


def _sgu(uv, norm_w, w_pos, bias_t, rows, emit_v):
    m, two_w = uv.shape
    width = two_w // 2
    groups = w_pos.shape[0]
    out_shape = [jax.ShapeDtypeStruct((m, width), BF16)]
    out_specs = [pl.BlockSpec((rows, width), lambda i: (i, 0))]
    if emit_v:
        out_shape.append(jax.ShapeDtypeStruct((m, width), F32))
        out_specs.append(pl.BlockSpec((rows, width), lambda i: (i, 0)))
    return pl.pallas_call(
        functools.partial(_sgu_kernel, groups=groups),
        grid=(m // rows,),
        in_specs=[pl.BlockSpec((rows, width), lambda i: (i, 0)),
                  pl.BlockSpec((rows, width), lambda i: (i, 1)),
                  pl.BlockSpec((1, width), lambda i: (0, 0)),
                  pl.BlockSpec((groups, rows, rows), lambda i: (0, 0, 0)),
                  pl.BlockSpec((rows, groups), lambda i: (0, 0))],
        out_specs=out_specs,
        out_shape=out_shape,
        compiler_params=_params("arbitrary"),
        name="sgu",
    )(uv, uv, norm_w, w_pos, bias_t)


def _conv_sample_kernel(z0_ref, z1_ref, z2_ref, gb_ref, w_ref, o_ref):
    w = w_ref[...]
    y = w[0:1] * z0_ref[...] + w[1:2] * z1_ref[...] + w[2:3] * z2_ref[...]
    o_ref[...] = (gb_ref[...].astype(F32) * y).astype(o_ref.dtype)


def _conv_sample(z0, z1, z2, gb, conv_w):
    return pl.pallas_call(
        _conv_sample_kernel,
        out_shape=jax.ShapeDtypeStruct(z0.shape, BF16),
        name="conv_sample",
    )(z0, z1, z2, gb, conv_w)


def _gla_kernel(q_ref, k_ref, v_ref, la_ref, gate_ref, nw_ref, s0_ref, o_ref, st_ref, st, *, nc):
    ci = pl.program_id(1)

    @pl.when(ci == 0)
    def _():
        st[...] = s0_ref[...]

    heads, dv, dk = st.shape
    la = la_ref[...]
    c, width = la.shape
    r = lax.broadcasted_iota(jnp.int32, (c, c), 0)
    cc = lax.broadcasted_iota(jnp.int32, (c, c), 1)
    tril = cc <= r
    y = jnp.dot(jnp.where(tril, 1.0, 0.0).astype(BF16),
                jnp.concatenate(_split3(la), axis=1), preferred_element_type=F32)
    bc = y[:, :width] + y[:, width:2 * width] + y[:, 2 * width:]
    q = q_ref[...].astype(F32) * dk ** -0.5
    k = k_ref[...].astype(F32)
    qt = (q * jnp.exp(bc)).astype(BF16)
    mid = bc[c // 2 - 1:c // 2, :]
    qa = (q * jnp.exp(bc - mid)).astype(BF16)
    ka = (k * jnp.exp(mid - bc)).astype(BF16)
    bl = bc[c - 1:c, :]
    kd = (k * jnp.exp(bl - bc)).astype(BF16)
    decay = jnp.exp(bl)
    for h in range(heads):
        kcol = slice(h * dk, (h + 1) * dk)
        vcol = slice(h * dv, (h + 1) * dv)
        v = v_ref[:, vcol]
        att = lax.dot_general(qa[:, kcol], ka[:, kcol], (((1,), (1,)), ((), ())),
                              preferred_element_type=F32)
        att = jnp.where(tril, att, 0.0)
        s_t = st[h]
        o = (lax.dot_general(qt[:, kcol], s_t.astype(BF16), (((1,), (1,)), ((), ())),
                             preferred_element_type=F32)
             + jnp.dot(att.astype(BF16), v, preferred_element_type=F32))
        st[h] = s_t * decay[:, kcol] + lax.dot_general(v, kd[:, kcol], (((0,), (0,)), ((), ())),
                                                       preferred_element_type=F32)
        on = o * lax.rsqrt(jnp.mean(o * o, axis=-1, keepdims=True) + EPS) * nw_ref[:, vcol]
        o_ref[:, vcol] = (on * gate_ref[:, vcol].astype(F32)).astype(o_ref.dtype)

    @pl.when(ci == nc - 1)
    def _():
        st_ref[...] = st[...]


def _gla(qkv, log_a, gate, norm_w, s0_t, batch, seq, heads, chunk):
    m = qkv.shape[0]
    key_w, val_w = log_a.shape[1], gate.shape[1]
    dk, dv = key_w // heads, val_w // heads
    assert (2 * key_w) % val_w == 0
    v0 = (2 * key_w) // val_w
    chunk = min(chunk, seq)
    nc = seq // chunk
    row = lambda b, c: b * nc + c
    return pl.pallas_call(
        functools.partial(_gla_kernel, nc=nc),
        grid=(batch, nc),
        in_specs=[pl.BlockSpec((chunk, key_w), lambda b, c: (row(b, c), 0)),
                  pl.BlockSpec((chunk, key_w), lambda b, c: (row(b, c), 1)),
                  pl.BlockSpec((chunk, val_w), lambda b, c: (row(b, c), v0)),
                  pl.BlockSpec((chunk, key_w), lambda b, c: (row(b, c), 0)),
                  pl.BlockSpec((chunk, val_w), lambda b, c: (row(b, c), 0)),
                  pl.BlockSpec((1, val_w), lambda b, c: (0, 0)),
                  pl.BlockSpec((None, heads, dv, dk), lambda b, c: (b, 0, 0, 0))],
        out_specs=[pl.BlockSpec((chunk, val_w), lambda b, c: (row(b, c), 0)),
                   pl.BlockSpec((None, heads, dv, dk), lambda b, c: (b, 0, 0, 0))],
        out_shape=[jax.ShapeDtypeStruct((m, val_w), BF16),
                   jax.ShapeDtypeStruct((batch, heads, dv, dk), F32)],
        scratch_shapes=[pltpu.VMEM((heads, dv, dk), F32)],
        compiler_params=_params("arbitrary", "arbitrary"),
        name="gla",
    )(qkv, qkv, qkv, log_a, gate, norm_w, s0_t)


def kernel(x_prompt, x_sample, cache_k, cache_v, cache_logf, state_conv, state_gla, page_table,
           c_prompt, c_sample, w_in0, f_bias, q_norm, k_norm, sgu_norm, sgu_w, sgu_b, w_out0,
           w_in1, conv_w, gla_a_w2, gla_a_b, gla_norm, w_out1, ada_w, ada_b, mlp_w1, mlp_w2):
    nbp, seq, d = x_prompt.shape
    nbs, dseq, _ = x_sample.shape
    mp, ms = nbp * seq, nbs * dseq
    fox_w = d // 2
    heads = fox_w // HEAD_DIM
    sgu_width = d // 2
    groups = sgu_w.shape[1]
    conv_width = state_conv.shape[-1]
    gla_heads, gla_dk, gla_dv = state_gla.shape[2], state_gla.shape[3], state_gla.shape[4]
    gla_key, gla_val = gla_heads * gla_dk, gla_heads * gla_dv
    gla_rank = gla_a_w2.shape[1]
    n_pool, page = cache_k.shape[1], cache_k.shape[2]
    n_pages = page_table.shape[1]

    r_c = nbp + nbs
    r_pad = -(-r_c // 16) * 16
    c_rows = jnp.pad(jnp.concatenate([c_prompt, c_sample], axis=0),
                     ((0, r_pad - r_c), (0, 0))).astype(BF16)
    mod = _ada(c_rows, ada_w, ada_b)

    def mods(layer):
        parts = jnp.split(mod[layer], 6, axis=-1)
        pp = [p[:nbp].reshape(nbp, 1, d) for p in parts]
        ps = [jnp.repeat(p[nbp:r_c], dseq, axis=0) for p in parts]
        return pp, ps

    xp = x_prompt.reshape(mp, d)
    xs = x_sample.reshape(ms, d)

    def modulate_both(xp, xs, shp, scp, shs, scs):
        hp = _modulate(xp, shp, scp, seq)
        hs = _modulate(xs, shs.reshape(1, ms, d), scs.reshape(1, ms, d), ms, tr=ms)
        return hp, hs

    def mlp(xp, xs, layer, shp, scp, gp, shs, scs, gs):
        hp, hs = modulate_both(xp, xs, shp, scp, shs, scs)
        (ap,), (as_,) = _proj([hp], [hs], [(mlp_w1, layer, 0)], mlp_w1.shape[2], _epi_relu2,
                              [BF16], name="mlp_up")
        return _down(ap, as_, mlp_w2, layer, xp, gp, xs, gs, seq)

    def residual_proj(lhs_p, lhs_s, w, xp, xs, gp, gs, name):
        (yp,), (ys,) = _proj(lhs_p, lhs_s, [(w, 0, 0)], d, _epi_residual, [F32],
                             ext=[("res_p", "prow", xp), ("gate_p", "pgrp", gp),
                                  ("res_s", "srow", xs), ("gate_s", "srow", gs)],
                             rows_per_group=seq, name=name)
        return yp, ys

    (sh1p, sc1p, g1p, sh2p, sc2p, g2p), (sh1s, sc1s, g1s, sh2s, sc2s, g2s) = mods(0)
    hp, hs = modulate_both(xp, xs, sh1p, sc1p, sh1s, sc1s)
    w0 = jnp.swapaxes(w_in0, 1, 2)
    qn = jnp.tile(q_norm[0], heads).reshape(1, fox_w)
    kn = jnp.tile(k_norm[0], heads).reshape(1, fox_w)
    (q_p,), (q_s,) = _proj([hp], [hs], [(w0, 0, 0)], fox_w, _epi_head_norm, [BF16],
                           ext=[("norm", "col", qn)], w_rows=True, name="in0_q")
    (k_p,), (k_s,) = _proj([hp], [hs], [(w0, 0, fox_w)], fox_w, _epi_head_norm, [F32],
                           ext=[("norm", "col", kn)], w_rows=True, name="in0_k")
    (v_p,), (v_s,) = _proj([hp], [hs], [(w0, 0, 2 * fox_w)], fox_w, _epi_plain, [F32],
                           w_rows=True, name="in0_v")
    fb = jnp.pad(f_bias[0], (0, LANES - heads)).reshape(1, LANES)
    (lf_p,), (lf_s,) = _proj([hp], [hs], [(w0, 0, 3 * fox_w)], LANES, _epi_log_forget, [F32],
                             ext=[("bias", "col", fb)], w_rows=True, name="in0_logf")
    (uv_p,), (uv_s,) = _proj([hp], [hs], [(w0, 0, 3 * fox_w + heads)], 2 * sgu_width, _epi_gelu,
                             [BF16], w_rows=True, name="in0_uv")

    f_col = _cumsum_rows(lf_p, nbp)
    lf_s16 = lf_s[:, :heads].reshape(nbs, dseq, heads)
    tpad = 8
    lf_s_pad = jnp.pad(lf_s.reshape(nbs, dseq, LANES), ((0, 0), (0, tpad - dseq), (0, 0)))
    fn = _cumsum_rows(lf_s_pad.reshape(nbs * tpad, LANES), nbs).reshape(
        nbs, tpad, LANES)[:, :dseq, :heads]
    rows_s = dseq * heads
    assert rows_s <= LANES

    def new_rows(a):
        a = a.reshape(nbs, rows_s, HEAD_DIM)
        return jnp.pad(a, ((0, 0), (0, LANES - rows_s), (0, 0))).astype(BF16)

    fs_lane = jnp.pad(fn.reshape(nbs, 1, rows_s), ((0, 0), (0, 0), (0, LANES - rows_s)))
    oa_p, oa_rows, k5_p, v5_p = _fox(q_p, k_p, v_p, f_col, nbp, seq, heads,
                         page_table, q_s.reshape(nbs, rows_s, HEAD_DIM),
                         cache_k[0].reshape(n_pool * page * heads, HEAD_DIM),
                         cache_v[0].reshape(n_pool * page * heads, HEAD_DIM),
                         cache_logf[0].reshape(n_pool, page * heads // LANES, LANES),
                         new_rows(k_s), new_rows(v_s), fn.reshape(nbs, rows_s, 1), fs_lane)
    oa_s = oa_rows.reshape(ms, fox_w).astype(BF16)

    sn = sgu_norm[0].reshape(1, sgu_width)
    rows_p = min(seq, SGU_CHUNK)
    (ob_p,) = _sgu(uv_p, sn, sgu_w[0][:, :rows_p, :rows_p], sgu_b[0][:, :rows_p].T, rows_p, False)
    rows_g = min(dseq, SGU_CHUNK)
    w_small = sgu_w[0][:, :rows_g, :rows_g]
    w_big = jnp.einsum("ab,gts->gatbs", jnp.eye(nbs, dtype=F32), w_small).reshape(groups, ms, ms)
    b_big = jnp.tile(sgu_b[0][:, :rows_g].T, (nbs, 1))
    ob_s, sgu_v = _sgu(uv_s, sn, w_big, b_big, ms, True)

    xp, xs = residual_proj([oa_p, ob_p], [oa_s, ob_s], w_out0, xp, xs, g1p, g1s, "out0")
    xp, xs = mlp(xp, xs, 0, sh2p, sc2p, g2p, sh2s, sc2s, g2s)

    (sh1p, sc1p, g1p, sh2p, sc2p, g2p), (sh1s, sc1s, g1s, sh2s, sc2s, g2s) = mods(1)
    hp, hs = modulate_both(xp, xs, sh1p, sc1p, sh1s, sc1s)
    w1 = jnp.swapaxes(w_in1, 1, 2)
    cw = conv_width
    cwt = conv_w[0]
    n_tap = cwt.shape[0]
    (oc_p, ztail_p), (gb_s, z_s) = _proj(
        [hp], [hs], [(w1, 0, 0), (w1, 0, cw), (w1, 0, 2 * cw)], cw, _make_epi_conv(seq),
        [BF16, F32], ext=[("taps", "col", cwt)], tm=512, tn=256, rows_per_group=seq, w_rows=True,
        tail_rows=(1,), scratch=lambda tm, tn: [pltpu.VMEM((tm + 8, tn), F32)], name="in1_conv")
    conv_p = ztail_p.reshape(nbp, -1, 8, cw)[:, -1, 8 - (n_tap - 1):, :]
    (qkv_p,), (qkv_s,) = _proj([hp], [hs], [(w1, 0, 3 * cw)], 2 * gla_key + gla_val, _epi_plain,
                               [BF16], w_rows=True, name="in1_qkv")
    c_ga = 3 * cw + 2 * gla_key + gla_val
    (ga_p,), (ga_s,) = _proj([hp], [hs], [(w1, 0, c_ga)], LANES, _epi_plain, [BF16],
                             w_rows=True, name="in1_ga")
    w_a2 = jnp.pad(gla_a_w2, ((0, 0), (0, LANES - gla_rank), (0, 0)))
    (la_p,), (la_s,) = _proj([ga_p], [ga_s], [(w_a2, 0, 0)], gla_key, _epi_log_decay, [F32],
                             ext=[("bias", "col", gla_a_b[0].reshape(1, gla_key))], name="in1_log_a")
    (gg_p,), (gg_s,) = _proj([hp], [hs], [(w1, 0, c_ga + gla_rank)], gla_val, _epi_silu, [BF16],
                             w_rows=True, name="in1_gate")

    zp_s = jnp.concatenate([state_conv[0], z_s.reshape(nbs, dseq, cw)], axis=1)
    shifted = [zp_s[:, i:i + dseq].reshape(ms, cw) for i in range(cwt.shape[0])]
    oc_s = _conv_sample(shifted[0], shifted[1], shifted[2], gb_s, cwt)
    conv_s = zp_s[:, dseq:, :]

    gn = gla_norm[0].reshape(1, gla_val)
    s0_p = jnp.zeros((nbp, gla_heads, gla_dv, gla_dk), F32)
    od_p, st_p = _gla(qkv_p, la_p, gg_p, gn, s0_p, nbp, seq, gla_heads, GLA_CHUNK_ROWS)
    pad_t = 16

    def pad_rows(a):
        return jnp.pad(a.reshape(nbs, dseq, -1), ((0, 0), (0, pad_t - dseq), (0, 0))).reshape(
            nbs * pad_t, -1)

    od_s_pad, st_s = _gla(pad_rows(qkv_s), pad_rows(la_s), pad_rows(gg_s), gn,
                          state_gla[0].swapaxes(-1, -2), nbs, pad_t, gla_heads, pad_t)
    od_s = od_s_pad.reshape(nbs, pad_t, gla_val)[:, :dseq].reshape(ms, gla_val)

    xp, xs = residual_proj([oc_p, od_p], [oc_s, od_s], w_out1, xp, xs, g1p, g1s, "out1")
    xp, xs = mlp(xp, xs, 1, sh2p, sc2p, g2p, sh2s, sc2s, g2s)

    y_prompt = xp.reshape(nbp, seq, d)
    y_sample = xs.reshape(nbs, dseq, d)
    return (y_prompt, y_sample,
            k5_p.reshape(1, nbp, seq, heads, HEAD_DIM), v5_p.reshape(1, nbp, seq, heads, HEAD_DIM),
            lf_p[:, :heads].reshape(1, nbp, seq, heads),
            k_s.reshape(1, nbs, dseq, heads, HEAD_DIM), v_s.reshape(1, nbs, dseq, heads, HEAD_DIM),
            lf_s16[None], sgu_v.reshape(1, nbs, dseq, sgu_width),
            conv_p[None], conv_s[None],
            st_p.swapaxes(-1, -2)[None], st_s.swapaxes(-1, -2)[None])
```

```python
import functools
import math

import jax
import jax.numpy as jnp
from jax import lax
from jax.experimental import pallas as pl
from jax.experimental.pallas import tpu as pltpu

F32 = jnp.float32
BF16 = jnp.bfloat16
EPS = 1e-6
NEG_INF = -1e30
HEAD_DIM = 128
LANES = 128
SGU_CHUNK = 128
GLA_TAU = 16.0
GLA_CHUNK_ROWS = 128
FOX_PAGES_PER_STEP = 4
VMEM_LIMIT_BYTES = 56 * 1024 * 1024


def _params(*semantics):
    return pltpu.CompilerParams(dimension_semantics=semantics,
                                vmem_limit_bytes=VMEM_LIMIT_BYTES)


def _split3(x):
    hi = x.astype(BF16)
    r = x - hi.astype(F32)
    mid = r.astype(BF16)
    lo = (r - mid.astype(F32)).astype(BF16)
    return hi, mid, lo


def _log_sigmoid(x):
    return jnp.minimum(x, 0.0) - jnp.log1p(jnp.exp(-jnp.abs(x)))


def _gelu_tanh(x):
    c = 0.7978845608028654
    return 0.5 * x * (1.0 + jnp.tanh(c * (x + 0.044715 * (x * x * x))))


def _ada_kernel(c_ref, w_ref, b_ref, o_ref):
    o_ref[...] = jnp.dot(c_ref[...], w_ref[...].astype(BF16),
                         preferred_element_type=F32) + b_ref[...]


def _ada(c_rows, ada_w, ada_b, tn=1024):
    n_layers, d, n = ada_w.shape
    r = c_rows.shape[0]
    tn = min(tn, n)
    return pl.pallas_call(
        _ada_kernel,
        grid=(n_layers, n // tn),
        in_specs=[pl.BlockSpec((r, d), lambda l, j: (0, 0)),
                  pl.BlockSpec((None, d, tn), lambda l, j: (l, 0, j)),
                  pl.BlockSpec((None, 1, tn), lambda l, j: (l, 0, j))],
        out_specs=pl.BlockSpec((None, r, tn), lambda l, j: (l, 0, j)),
        out_shape=jax.ShapeDtypeStruct((n_layers, r, n), F32),
        compiler_params=_params("arbitrary", "arbitrary"),
        name="ada",
    )(c_rows, ada_w, ada_b.reshape(n_layers, 1, n))


def _modulate_kernel(x_ref, sh_ref, sc_ref, o_ref):
    x = x_ref[...]
    y = x * lax.rsqrt(jnp.mean(x * x, axis=-1, keepdims=True) + EPS)
    o_ref[...] = (y * (1.0 + sc_ref[...]) + sh_ref[...]).astype(o_ref.dtype)


def _modulate(x, sh, sc, rows_per_group, tr=512):
    rows, d = x.shape
    tr = min(tr, rows, rows_per_group)
    r = sh.shape[1]
    grp = lambda i: ((i * tr) // rows_per_group, 0, 0)
    return pl.pallas_call(
        _modulate_kernel,
        grid=(rows // tr,),
        in_specs=[pl.BlockSpec((tr, d), lambda i: (i, 0)),
                  pl.BlockSpec((None, r, d), grp),
                  pl.BlockSpec((None, r, d), grp)],
        out_specs=pl.BlockSpec((tr, d), lambda i: (i, 0)),
        out_shape=jax.ShapeDtypeStruct((rows, d), BF16),
        compiler_params=_params("arbitrary"),
        name="modulate",
    )(x, sh, sc)


def _proj_kernel(*refs, k_sizes, n_w, ext_names, n_out, epilogue, cast_rows, w_rows, n_scratch):
    n_lhs = len(k_sizes)
    pos = 0
    xp = refs[pos:pos + n_lhs]; pos += n_lhs
    xs = refs[pos:pos + n_lhs]; pos += n_lhs
    w = refs[pos:pos + n_w]; pos += n_w
    ext = dict(zip(ext_names, refs[pos:pos + len(ext_names)])); pos += len(ext_names)
    out_p = refs[pos:pos + n_out]; pos += n_out
    out_s = refs[pos:pos + n_out]; pos += n_out
    wbf = refs[pos:pos + n_w]; pos += n_w
    ext["scratch"] = refs[pos:pos + n_scratch]

    def accumulate(lhs):
        accs = []
        for m in range(n_w):
            acc = None
            off = 0
            for a, ka in enumerate(k_sizes):
                part = jnp.dot(lhs[a][...], wbf[m][off:off + ka, :],
                               preferred_element_type=F32)
                acc = part if acc is None else acc + part
                off += ka
            accs.append(acc)
        return accs

    def store(out_refs, vals):
        for o_ref, val in zip(out_refs, vals):
            o_ref[...] = val.astype(o_ref.dtype)

    @pl.when(pl.program_id(1) == 0)
    def _():
        accs_p, accs_s = [None] * n_w, [None] * n_w
        r0 = 0
        for a, ka in enumerate(k_sizes):
            for loc in range(0, ka, cast_rows):
                for m in range(n_w):
                    if w_rows:
                        chunk = w[m][:, r0:r0 + cast_rows].T.astype(BF16)
                    else:
                        chunk = w[m][r0:r0 + cast_rows, :].astype(BF16)
                    wbf[m][r0:r0 + cast_rows, :] = chunk
                    pp = jnp.dot(xp[a][:, loc:loc + cast_rows], chunk, preferred_element_type=F32)
                    ps = jnp.dot(xs[a][:, loc:loc + cast_rows], chunk, preferred_element_type=F32)
                    accs_p[m] = pp if accs_p[m] is None else accs_p[m] + pp
                    accs_s[m] = ps if accs_s[m] is None else accs_s[m] + ps
                r0 += cast_rows
        store(out_s, epilogue(accs_s, ext, True))
        store(out_p, epilogue(accs_p, ext, False))

    @pl.when(pl.program_id(1) != 0)
    def _():
        store(out_p, epilogue(accumulate(xp), ext, False))


def _proj(xp, xs, weights, n_cols, epilogue, out_dtypes, ext=(), *, tm=1024, tn=512,
          rows_per_group=None, w_rows=False, tail_rows=(), scratch=None, name="proj"):
    mp, ms = xp[0].shape[0], xs[0].shape[0]
    k_sizes = tuple(int(a.shape[1]) for a in xp)
    k_total = sum(k_sizes)
    tm = min(tm, mp, rows_per_group or mp)
    tn = min(tn, n_cols)
    assert mp % tm == 0 and n_cols % tn == 0
    for warr, _, c0 in weights:
        assert warr.ndim == 3
        if w_rows:
            assert warr.shape[2] == k_total and c0 % 8 == 0 and len(k_sizes) == 1
        else:
            assert warr.shape[1] == k_total and c0 % tn == 0
    in_specs, args = [], []
    for a in xp:
        in_specs.append(pl.BlockSpec((tm, a.shape[1]), lambda j, i: (i, 0)))
        args.append(a)
    for a in xs:
        in_specs.append(pl.BlockSpec((ms, a.shape[1]), lambda j, i: (0, 0)))
        args.append(a)
    for warr, layer, c0 in weights:
        if w_rows:
            spec = pl.BlockSpec((None, pl.Element(tn), pl.Element(k_total)),
                                lambda j, i, c=c0, l=layer: (l, pl.multiple_of(c + j * tn, 8), 0))
        else:
            spec = pl.BlockSpec((None, k_total, tn),
                                lambda j, i, cb=c0 // tn, l=layer: (l, 0, j + cb))
        in_specs.append(spec)
        args.append(warr)
    ext_names = []
    for ename, kind, arr in ext:
        ext_names.append(ename)
        if kind == "col":
            spec = pl.BlockSpec((arr.shape[0], tn), lambda j, i: (0, j))
        elif kind == "prow":
            spec = pl.BlockSpec((tm, tn), lambda j, i: (i, j))
        elif kind == "pgrp":
            spec = pl.BlockSpec((None, 1, tn),
                                lambda j, i: ((i * tm) // rows_per_group, 0, j))
        elif kind == "srow":
            spec = pl.BlockSpec((ms, tn), lambda j, i: (0, j))
        else:
            raise ValueError(kind)
        in_specs.append(spec)
        args.append(arr)
    n_out = len(out_dtypes)
    rows_p = [(8 * (mp // tm), 8) if o in tail_rows else (mp, tm) for o in range(n_out)]
    out_specs = ([pl.BlockSpec((br, tn), lambda j, i: (i, j)) for _, br in rows_p]
                 + [pl.BlockSpec((ms, tn), lambda j, i: (0, j))] * n_out)
    out_shape = ([jax.ShapeDtypeStruct((r, n_cols), dt) for (r, _), dt in zip(rows_p, out_dtypes)]
                 + [jax.ShapeDtypeStruct((ms, n_cols), dt) for dt in out_dtypes])
    cast_rows = min(512, *k_sizes)
    assert all(ka % cast_rows == 0 for ka in k_sizes)
    extra_scratch = scratch(tm, tn) if scratch else []
    outs = pl.pallas_call(
        functools.partial(_proj_kernel, k_sizes=k_sizes, n_w=len(weights),
                          ext_names=tuple(ext_names), n_out=n_out, epilogue=epilogue,
                          cast_rows=cast_rows, w_rows=w_rows, n_scratch=len(extra_scratch)),
        grid=(n_cols // tn, mp // tm),
        in_specs=in_specs,
        out_specs=out_specs,
        out_shape=out_shape,
        scratch_shapes=[pltpu.VMEM((k_total, tn), BF16) for _ in weights] + extra_scratch,
        compiler_params=_params("arbitrary", "arbitrary"),
        name=name,
    )(*args)
    return outs[:n_out], outs[n_out:]


def _epi_plain(accs, ext, is_sample):
    return (accs[0],)


def _epi_head_norm(accs, ext, is_sample):
    acc = accs[0]
    wn = ext["norm"][...]
    parts = []
    for c0 in range(0, acc.shape[1], HEAD_DIM):
        a = acc[:, c0:c0 + HEAD_DIM]
        y = a * lax.rsqrt(jnp.mean(a * a, axis=-1, keepdims=True) + EPS)
        parts.append(y * wn[:, c0:c0 + HEAD_DIM])
    return (jnp.concatenate(parts, axis=1),)


def _epi_log_forget(accs, ext, is_sample):
    return (_log_sigmoid(accs[0] + ext["bias"][...]),)


def _epi_gelu(accs, ext, is_sample):
    return (_gelu_tanh(accs[0]),)


def _epi_silu(accs, ext, is_sample):
    a = accs[0]
    return (a * jax.nn.sigmoid(a),)


def _epi_relu2(accs, ext, is_sample):
    r = jnp.maximum(accs[0], 0.0)
    return (r * r,)


def _make_epi_conv(seq):
    def epilogue(accs, ext, is_sample):
        z = accs[1] * accs[0]
        if is_sample:
            return accs[2], z
        (zbuf,) = ext["scratch"]
        tm = z.shape[0]

        @pl.when(pl.program_id(1) % (seq // tm) == 0)
        def _():
            zbuf[0:8, :] = jnp.zeros((8, zbuf.shape[1]), F32)

        zbuf[8:8 + tm, :] = z
        w = ext["taps"][...]
        y = w[0:1] * zbuf[6:6 + tm, :] + w[1:2] * zbuf[7:7 + tm, :] + w[2:3] * z
        tail = z[tm - 8:tm, :]
        zbuf[0:8, :] = tail
        return accs[2] * y, tail

    return epilogue


def _epi_log_decay(accs, ext, is_sample):
    return (_log_sigmoid(accs[0] + ext["bias"][...]) / GLA_TAU,)


def _epi_residual(accs, ext, is_sample):
    if is_sample:
        return (ext["res_s"][...] + ext["gate_s"][...] * accs[0],)
    return (ext["res_p"][...] + ext["gate_p"][...] * accs[0],)


def _down_kernel(xp_ref, xs_ref, w_ref, resp_ref, gp_ref, ress_ref, gs_ref,
                 op_ref, os_ref, accs_ref, wbf, *, nk, cast_rows):
    i = pl.program_id(1)
    k = pl.program_id(2)

    @pl.when(k == 0)
    def _():
        op_ref[...] = jnp.zeros_like(op_ref)

    total = None
    for r0 in range(0, w_ref.shape[0], cast_rows):
        wbf[r0:r0 + cast_rows, :] = w_ref[r0:r0 + cast_rows, :].astype(BF16)
        part = jnp.dot(xp_ref[:, r0:r0 + cast_rows], wbf[r0:r0 + cast_rows, :],
                       preferred_element_type=F32)
        total = part if total is None else total + part
    op_ref[...] += total

    @pl.when(k == nk - 1)
    def _():
        op_ref[...] = resp_ref[...] + gp_ref[...] * op_ref[...]

    @pl.when(i == 0)
    def _():
        ps = jnp.dot(xs_ref[...], wbf[...], preferred_element_type=F32)

        @pl.when(k == 0)
        def _():
            accs_ref[...] = ps

        @pl.when(k > 0)
        def _():
            accs_ref[...] += ps

        @pl.when(k == nk - 1)
        def _():
            os_ref[...] = ress_ref[...] + gs_ref[...] * accs_ref[...]


def _down(xp, xs, w, layer, res_p, gate_p, res_s, gate_s, rows_per_group, *,
          tm=1024, tn=1024, tk=2048):
    mp, kdim = xp.shape
    ms = xs.shape[0]
    n = w.shape[2]
    tm, tn, tk = min(tm, rows_per_group, mp), min(tn, n), min(tk, kdim)
    nk = kdim // tk
    cast_rows = min(512, tk)
    assert tk % cast_rows == 0
    return pl.pallas_call(
        functools.partial(_down_kernel, nk=nk, cast_rows=cast_rows),
        grid=(n // tn, mp // tm, nk),
        in_specs=[pl.BlockSpec((tm, tk), lambda j, i, k: (i, k)),
                  pl.BlockSpec((ms, tk), lambda j, i, k: (0, k)),
                  pl.BlockSpec((None, tk, tn), lambda j, i, k: (layer, k, j)),
                  pl.BlockSpec((tm, tn), lambda j, i, k: (i, j)),
                  pl.BlockSpec((None, 1, tn),
                               lambda j, i, k: ((i * tm) // rows_per_group, 0, j)),
                  pl.BlockSpec((ms, tn), lambda j, i, k: (0, j)),
                  pl.BlockSpec((ms, tn), lambda j, i, k: (0, j))],
        out_specs=[pl.BlockSpec((tm, tn), lambda j, i, k: (i, j)),
                   pl.BlockSpec((ms, tn), lambda j, i, k: (0, j))],
        out_shape=[jax.ShapeDtypeStruct((mp, n), F32),
                   jax.ShapeDtypeStruct((ms, n), F32)],
        scratch_shapes=[pltpu.VMEM((ms, tn), F32), pltpu.VMEM((tk, tn), BF16)],
        compiler_params=_params("arbitrary", "arbitrary", "arbitrary"),
        name="mlp_down",
    )(xp, xs, w, res_p, gate_p, res_s, gate_s)


def _cumsum_kernel(x_ref, o_ref, carry_ref):
    @pl.when(pl.program_id(1) == 0)
    def _():
        carry_ref[...] = jnp.zeros_like(carry_ref)

    x = x_ref[...]
    t, w = x.shape
    r = lax.broadcasted_iota(jnp.int32, (t, t), 0)
    c = lax.broadcasted_iota(jnp.int32, (t, t), 1)
    tri = jnp.where(c <= r, 1.0, 0.0).astype(BF16)
    y = jnp.dot(tri, jnp.concatenate(_split3(x), axis=1), preferred_element_type=F32)
    out = y[:, :w] + y[:, w:2 * w] + y[:, 2 * w:] + carry_ref[...]
    o_ref[...] = out
    carry_ref[...] = out[t - 1:t, :]


def _cumsum_rows(x, n_groups, tc=256):
    rows, w = x.shape
    per = rows // n_groups
    tc = min(tc, per)
    nt = per // tc
    return pl.pallas_call(
        _cumsum_kernel,
        grid=(n_groups, nt),
        in_specs=[pl.BlockSpec((tc, w), lambda b, t: (b * nt + t, 0))],
        out_specs=pl.BlockSpec((tc, w), lambda b, t: (b * nt + t, 0)),
        out_shape=jax.ShapeDtypeStruct((rows, w), F32),
        scratch_shapes=[pltpu.VMEM((1, w), F32)],
        compiler_params=_params("arbitrary", "arbitrary"),
        name="cumsum_logf",
    )(x)


def _bias_lanes(col, own_first):
    hi, mid, lo = _split3(col)
    lane = lax.broadcasted_iota(jnp.int32, (col.shape[0], LANES), 1)
    own, other = (0, 3) if own_first else (3, 0)
    x = jnp.where(lane == own, hi.astype(F32),
                  jnp.where(lane == own + 1, mid.astype(F32),
                            jnp.where(lane == own + 2, lo.astype(F32), 0.0)))
    x = jnp.where((lane >= other) & (lane < other + 3), 1.0, x)
    return x.astype(BF16)


def _fox_prompt_step(q_ref, k_ref, v_ref, fq_ref, fk_ref, o_ref, k5_hbm, v5_hbm, kaug, vbf, sem,
                     *, tq, nq, scale, alongside):
    b = pl.program_id(0)
    h = pl.program_id(1)
    qi = pl.program_id(2)
    seq = k_ref.shape[0]

    def head_copy(src_ref, dst_hbm, slot):
        return pltpu.make_async_copy(src_ref, dst_hbm.at[pl.ds(b * seq, seq), h], sem.at[slot])

    def head_column(f):
        lane = lax.broadcasted_iota(jnp.int32, f.shape, 1)
        return jnp.sum(jnp.where(lane == h, f, 0.0), axis=1, keepdims=True)

    @pl.when(qi == 0)
    def _():
        head_copy(k_ref, k5_hbm, 0).start()
        head_copy(v_ref, v5_hbm, 1).start()
        kaug[:, :HEAD_DIM] = k_ref[...].astype(BF16)
        kaug[:, HEAD_DIM:] = _bias_lanes(head_column(fk_ref[...]) * (-1.0 / scale), False)
        vbf[...] = v_ref[...].astype(BF16)

    q = jnp.concatenate(
        [q_ref[...], _bias_lanes(head_column(fq_ref[...]) * (1.0 / scale), True)], axis=1)
    c2 = scale * 1.4426950408889634

    def step(j, carry, masked):
        m, l, acc = carry
        start = j * tq
        s = lax.dot_general(q, kaug[pl.ds(start, tq), :], (((1,), (1,)), ((), ())),
                            preferred_element_type=F32) * c2
        if masked:
            r = lax.broadcasted_iota(jnp.int32, (tq, tq), 0)
            c = lax.broadcasted_iota(jnp.int32, (tq, tq), 1)
            s = jnp.where(c <= r, s, NEG_INF)
        m_new = jnp.maximum(m, jnp.max(s, axis=1, keepdims=True))
        alpha = jnp.exp2(m - m_new)
        p = jnp.exp2(s - m_new)
        l = alpha * l + jnp.sum(p, axis=1, keepdims=True)
        acc = alpha * acc + jnp.dot(p.astype(BF16), vbf[pl.ds(start, tq), :],
                                    preferred_element_type=F32)
        return m_new, l, acc

    init = (jnp.full((tq, 1), NEG_INF, F32), jnp.zeros((tq, 1), F32),
            jnp.zeros((tq, HEAD_DIM), F32))
    for n_full in range(nq):
        @pl.when(qi == n_full)
        def _(n_full=n_full):
            alongside()
            carry = init
            for j in range(n_full):
                carry = step(j, carry, False)
            _, l, acc = step(n_full, carry, True)
            o_ref[...] = (acc / l).astype(o_ref.dtype)
            if n_full == 0:
                head_copy(k_ref, k5_hbm, 0).wait()
                head_copy(v_ref, v5_hbm, 1).wait()


def _fox_sample_step(p, live, q_ref, *refs, n_steps, pps, heads, scale):
    ck_refs, cv_refs, lf_refs = refs[:pps], refs[pps:2 * pps], refs[2 * pps:3 * pps]
    kn_ref, vn_ref, fn_ref, fs_ref, o_ref, m_ref, l_ref, acc_ref, tail_ref = refs[3 * pps:]
    first = p == 0

    def carried(ref, start):
        return jnp.where(first, start, ref[...])

    def keep(ref, new, old):
        ref[...] = jnp.where(live, new, old)

    q = q_ref[...]
    rows = q.shape[0]
    fn = fn_ref[...]
    row = lax.broadcasted_iota(jnp.int32, (rows, LANES), 0)
    lane = lax.broadcasted_iota(jnp.int32, (rows, LANES), 1)
    own_head = (lane & (heads - 1)) == (row & (heads - 1))

    log2e = 1.4426950408889634

    def attend(keys, vals, lane_bias, visible, paged):
        s = lax.dot_general(q, keys, (((1,), (1,)), ((), ())), preferred_element_type=F32)
        bias2 = lane_bias * log2e
        base = jnp.where(visible, fn * log2e, NEG_INF)
        s = jnp.concatenate(
            [s[:, g * LANES:(g + 1) * LANES] * (scale * log2e) + (base + bias2[g:g + 1, :])
             for g in range(keys.shape[0] // LANES)], axis=1)
        if paged:
            m_old, l_old, acc_old = (carried(m_ref, NEG_INF), carried(l_ref, 0.0),
                                     carried(acc_ref, 0.0))
        else:
            m_old, l_old, acc_old = m_ref[...], l_ref[...], acc_ref[...]
        m_new = jnp.maximum(m_old, jnp.max(s, axis=1, keepdims=True))
        alpha = jnp.exp2(m_old - m_new)
        pr = jnp.exp2(s - m_new)
        l_new = alpha * l_old + jnp.sum(pr, axis=1, keepdims=True)
        acc_new = alpha * acc_old + jnp.dot(pr.astype(BF16), vals, preferred_element_type=F32)
        if paged:
            keep(m_ref, m_new, m_ref[...])
            keep(l_ref, l_new, l_ref[...])
            keep(acc_ref, acc_new, acc_ref[...])
        else:
            m_ref[...], l_ref[...], acc_ref[...] = m_new, l_new, acc_new

    a = lax.broadcasted_iota(jnp.int32, (LANES, LANES), 0)
    b = lax.broadcasted_iota(jnp.int32, (LANES, LANES), 1)
    same_head = (a & (heads - 1)) == (b & (heads - 1))
    later_in_row = jnp.where(same_head & (a > b), 1.0, 0.0).astype(BF16)
    whole_row = jnp.where(same_head, 1.0, 0.0).astype(BF16)
    sums = jnp.concatenate([later_in_row, whole_row], axis=1)
    n_rows = lf_refs[0].shape[0]
    ra = lax.broadcasted_iota(jnp.int32, (n_rows, n_rows), 0)
    rb = lax.broadcasted_iota(jnp.int32, (n_rows, n_rows), 1)
    later_rows = jnp.where(rb > ra, 1.0, 0.0).astype(BF16)

    tail = carried(tail_ref, 0.0)
    biases = []
    for c in range(pps):
        lf = lf_refs[c][...]
        y = jnp.dot(jnp.concatenate(_split3(lf), axis=0), sums, preferred_element_type=F32)
        y = y[:n_rows] + y[n_rows:2 * n_rows] + y[2 * n_rows:]
        within, row_tot = y[:, :LANES], y[:, LANES:]
        z = jnp.dot(later_rows, jnp.concatenate(_split3(row_tot), axis=1),
                    preferred_element_type=F32)
        biases.append(within + z[:, :LANES] + z[:, LANES:2 * LANES] + z[:, 2 * LANES:] + tail)
        tail = tail + jnp.sum(row_tot, axis=0, keepdims=True)
    keep(tail_ref, tail, tail_ref[...])

    attend(jnp.concatenate([r[...].astype(BF16) for r in ck_refs], axis=0),
           jnp.concatenate([r[...].astype(BF16) for r in cv_refs], axis=0),
           jnp.concatenate(biases, axis=0), own_head, True)

    @pl.when((p == n_steps - 1) & live)
    def _():
        attend(kn_ref[...], vn_ref[...], -fs_ref[...],
               own_head & (lane - (lane & (heads - 1)) <= row - (row & (heads - 1))), False)
        o_ref[...] = acc_ref[...] / l_ref[...]


def _fox_kernel(pt_ref, *refs, n_prompt_in, n_sample_in, nq, heads, n_steps, n_live, tq, pps, scale):
    del pt_ref
    prompt_in = refs[:n_prompt_in]
    sample_in = refs[n_prompt_in:n_prompt_in + n_sample_in]
    (op_ref, os_ref, k5_hbm, v5_hbm, kaug, vbf, m_ref, l_ref, acc_ref, tail_ref,
     sem) = refs[n_prompt_in + n_sample_in:]
    g = (pl.program_id(0) * heads + pl.program_id(1)) * nq + pl.program_id(2)

    @pl.when(g == 0)
    def _():
        m_ref[...] = jnp.full(m_ref.shape, NEG_INF, F32)
        l_ref[...] = jnp.zeros_like(l_ref)
        acc_ref[...] = jnp.zeros_like(acc_ref)
        tail_ref[...] = jnp.zeros_like(tail_ref)

    live = g < n_live
    p = jnp.minimum(g, n_live - 1) % n_steps

    def sample_step():
        _fox_sample_step(p, live, *sample_in, os_ref, m_ref, l_ref, acc_ref, tail_ref,
                         n_steps=n_steps, pps=pps, heads=heads, scale=scale)

    _fox_prompt_step(*prompt_in, op_ref, k5_hbm, v5_hbm, kaug, vbf, sem, tq=tq, nq=nq,
                     scale=scale, alongside=sample_step)


def _fox(q, k, v, f_col, batch, seq, heads, page_table, q_rows, cache_k, cache_v, cache_lf,
         k_new, v_new, fn_col, fs_lane, tq=512):
    tq = min(tq, seq)
    nq = seq // tq
    nbs, n_pages = page_table.shape
    rows, hd = q_rows.shape[1], q_rows.shape[2]
    lf_rows = cache_lf.shape[1]
    page_rows = lf_rows * LANES
    assert heads & (heads - 1) == 0 and LANES % heads == 0
    pps = math.gcd(n_pages, FOX_PAGES_PER_STEP)
    n_steps = n_pages // pps
    n_live = nbs * n_steps
    assert n_live <= batch * heads * nq

    def walk(b, h, i):
        g = jnp.minimum((b * heads + h) * nq + i, n_live - 1)
        return g // n_steps, g % n_steps

    def per_bs(*tail):
        return lambda b, h, i, pt: (walk(b, h, i)[0],) + tail

    def page_of(c):
        def f(b, h, i, pt):
            bs, p = walk(b, h, i)
            return pt[bs, n_pages - 1 - (p * pps + c)]
        return f

    kv_specs = [pl.BlockSpec((page_rows, hd), lambda b, h, i, pt, f=page_of(c): (f(b, h, i, pt), 0))
                for c in range(pps)]
    lf_specs = [pl.BlockSpec((None, lf_rows, LANES),
                             lambda b, h, i, pt, f=page_of(c): (f(b, h, i, pt), 0, 0))
                for c in range(pps)]
    prompt_specs = [pl.BlockSpec((tq, HEAD_DIM), lambda b, h, i, pt: (b * nq + i, h)),
                    pl.BlockSpec((seq, HEAD_DIM), lambda b, h, i, pt: (b, h)),
                    pl.BlockSpec((seq, HEAD_DIM), lambda b, h, i, pt: (b, h)),
                    pl.BlockSpec((tq, LANES), lambda b, h, i, pt: (b * nq + i, 0)),
                    pl.BlockSpec((seq, LANES), lambda b, h, i, pt: (b, 0))]
    sample_specs = ([pl.BlockSpec((None, rows, hd), per_bs(0, 0))] + kv_specs + kv_specs + lf_specs
                    + [pl.BlockSpec((None, LANES, hd), per_bs(0, 0)),
                       pl.BlockSpec((None, LANES, hd), per_bs(0, 0)),
                       pl.BlockSpec((None, rows, 1), per_bs(0, 0)),
                       pl.BlockSpec((None, 1, LANES), per_bs(0, 0))])
    grid_spec = pltpu.PrefetchScalarGridSpec(
        num_scalar_prefetch=1,
        grid=(batch, heads, nq),
        in_specs=prompt_specs + sample_specs,
        out_specs=[pl.BlockSpec((tq, HEAD_DIM), lambda b, h, i, pt: (b * nq + i, h)),
                   pl.BlockSpec((None, rows, hd), per_bs(0, 0)),
                   pl.BlockSpec(memory_space=pl.ANY), pl.BlockSpec(memory_space=pl.ANY)],
        scratch_shapes=[pltpu.VMEM((seq, 2 * HEAD_DIM), BF16), pltpu.VMEM((seq, HEAD_DIM), BF16),
                        pltpu.VMEM((rows, 1), F32), pltpu.VMEM((rows, 1), F32),
                        pltpu.VMEM((rows, hd), F32), pltpu.VMEM((1, LANES), F32),
                        pltpu.SemaphoreType.DMA((2,))],
    )
    kv_heads = jax.ShapeDtypeStruct((batch * seq, heads, HEAD_DIM), F32)
    return pl.pallas_call(
        functools.partial(_fox_kernel, n_prompt_in=len(prompt_specs),
                          n_sample_in=len(sample_specs), nq=nq, heads=heads, n_steps=n_steps,
                          n_live=n_live, tq=tq, pps=pps, scale=HEAD_DIM ** -0.5),
        grid_spec=grid_spec,
        out_shape=[jax.ShapeDtypeStruct(q.shape, BF16),
                   jax.ShapeDtypeStruct((nbs, rows, hd), F32), kv_heads, kv_heads],
        compiler_params=_params("arbitrary", "arbitrary", "arbitrary"),
        name="fox",
    )(page_table, q, k, v, f_col, f_col, q_rows, *([cache_k] * pps), *([cache_v] * pps),
      *([cache_lf] * pps), k_new, v_new, fn_col, fs_lane)


def _sgu_kernel(u_ref, vg_ref, nw_ref, w_ref, bt_ref, *out_refs, groups):
    o_ref = out_refs[0]
    g = vg_ref[...].astype(F32)
    v = g * lax.rsqrt(jnp.mean(g * g, axis=-1, keepdims=True) + EPS) * nw_ref[...]
    if len(out_refs) > 1:
        out_refs[1][...] = v
    vb = v.astype(BF16)
    rows = v.shape[0]
    cw = v.shape[1] // groups
    r = lax.broadcasted_iota(jnp.int32, (rows, rows), 0)
    c = lax.broadcasted_iota(jnp.int32, (rows, rows), 1)
    bt = bt_ref[...]
    for gi in range(groups):
        wm = jnp.where(c <= r, w_ref[gi], 0.0).astype(BF16)
        z = jnp.dot(wm, vb[:, gi * cw:(gi + 1) * cw], preferred_element_type=F32)
        z = z + bt[:, gi:gi + 1]
        u = u_ref[:, gi * cw:(gi + 1) * cw].astype(F32)
        o_ref[:, gi * cw:(gi + 1) * cw] = (u * z).astype(o_ref.dtype)


def _sgu(uv, norm_w, w_pos, bias_t, rows, emit_v):
    m, two_w = uv.shape
    width = two_w // 2
    groups = w_pos.shape[0]
    out_shape = [jax.ShapeDtypeStruct((m, width), BF16)]
    out_specs = [pl.BlockSpec((rows, width), lambda i: (i, 0))]
    if emit_v:
        out_shape.append(jax.ShapeDtypeStruct((m, width), F32))
        out_specs.append(pl.BlockSpec((rows, width), lambda i: (i, 0)))
    return pl.pallas_call(
        functools.partial(_sgu_kernel, groups=groups),
        grid=(m // rows,),
        in_specs=[pl.BlockSpec((rows, width), lambda i: (i, 0)),
                  pl.BlockSpec((rows, width), lambda i: (i, 1)),
                  pl.BlockSpec((1, width), lambda i: (0, 0)),
                  pl.BlockSpec((groups, rows, rows), lambda i: (0, 0, 0)),
                  pl.BlockSpec((rows, groups), lambda i: (0, 0))],
        out_specs=out_specs,
        out_shape=out_shape,
        compiler_params=_params("arbitrary"),
        name="sgu",
    )(uv, uv, norm_w, w_pos, bias_t)


def _conv_sample_kernel(z0_ref, z1_ref, z2_ref, gb_ref, w_ref, o_ref):
    w = w_ref[...]
    y = w[0:1] * z0_ref[...] + w[1:2] * z1_ref[...] + w[2:3] * z2_ref[...]
    o_ref[...] = (gb_ref[...].astype(F32) * y).astype(o_ref.dtype)


def _conv_sample(z0, z1, z2, gb, conv_w):
    return pl.pallas_call(
        _conv_sample_kernel,
        out_shape=jax.ShapeDtypeStruct(z0.shape, BF16),
        name="conv_sample",
    )(z0, z1, z2, gb, conv_w)


def _gla_kernel(q_ref, k_ref, v_ref, la_ref, gate_ref, nw_ref, s0_ref, o_ref, st_ref, st, *, nc):
    ci = pl.program_id(1)

    @pl.when(ci == 0)
    def _():
        st[...] = s0_ref[...]

    heads, dv, dk = st.shape
    la = la_ref[...]
    c, width = la.shape
    r = lax.broadcasted_iota(jnp.int32, (c, c), 0)
    cc = lax.broadcasted_iota(jnp.int32, (c, c), 1)
    tril = cc <= r
    y = jnp.dot(jnp.where(tril, 1.0, 0.0).astype(BF16),
                jnp.concatenate(_split3(la), axis=1), preferred_element_type=F32)
    bc = y[:, :width] + y[:, width:2 * width] + y[:, 2 * width:]
    q = q_ref[...].astype(F32) * dk ** -0.5
    k = k_ref[...].astype(F32)
    qt = (q * jnp.exp(bc)).astype(BF16)
    mid = bc[c // 2 - 1:c // 2, :]
    qa = (q * jnp.exp(bc - mid)).astype(BF16)
    ka = (k * jnp.exp(mid - bc)).astype(BF16)
    bl = bc[c - 1:c, :]
    kd = (k * jnp.exp(bl - bc)).astype(BF16)
    decay = jnp.exp(bl)
    for h in range(heads):
        kcol = slice(h * dk, (h + 1) * dk)
        vcol = slice(h * dv, (h + 1) * dv)
        v = v_ref[:, vcol]
        att = lax.dot_general(qa[:, kcol], ka[:, kcol], (((1,), (1,)), ((), ())),
                              preferred_element_type=F32)
        att = jnp.where(tril, att, 0.0)
        s_t = st[h]
        o = (lax.dot_general(qt[:, kcol], s_t.astype(BF16), (((1,), (1,)), ((), ())),
                             preferred_element_type=F32)
             + jnp.dot(att.astype(BF16), v, preferred_element_type=F32))
        st[h] = s_t * decay[:, kcol] + lax.dot_general(v, kd[:, kcol], (((0,), (0,)), ((), ())),
                                                       preferred_element_type=F32)
        on = o * lax.rsqrt(jnp.mean(o * o, axis=-1, keepdims=True) + EPS) * nw_ref[:, vcol]
        o_ref[:, vcol] = (on * gate_ref[:, vcol].astype(F32)).astype(o_ref.dtype)

    @pl.when(ci == nc - 1)
    def _():
        st_ref[...] = st[...]


def _gla(qkv, log_a, gate, norm_w, s0_t, batch, seq, heads, chunk):
    m = qkv.shape[0]
    key_w, val_w = log_a.shape[1], gate.shape[1]
    dk, dv = key_w // heads, val_w // heads
    assert (2 * key_w) % val_w == 0
    v0 = (2 * key_w) // val_w
    chunk = min(chunk, seq)
    nc = seq // chunk
    row = lambda b, c: b * nc + c
    return pl.pallas_call(
        functools.partial(_gla_kernel, nc=nc),
        grid=(batch, nc),
        in_specs=[pl.BlockSpec((chunk, key_w), lambda b, c: (row(b, c), 0)),
                  pl.BlockSpec((chunk, key_w), lambda b, c: (row(b, c), 1)),
                  pl.BlockSpec((chunk, val_w), lambda b, c: (row(b, c), v0)),
                  pl.BlockSpec((chunk, key_w), lambda b, c: (row(b, c), 0)),
                  pl.BlockSpec((chunk, val_w), lambda b, c: (row(b, c), 0)),
                  pl.BlockSpec((1, val_w), lambda b, c: (0, 0)),
                  pl.BlockSpec((None, heads, dv, dk), lambda b, c: (b, 0, 0, 0))],
        out_specs=[pl.BlockSpec((chunk, val_w), lambda b, c: (row(b, c), 0)),
                   pl.BlockSpec((None, heads, dv, dk), lambda b, c: (b, 0, 0, 0))],
        out_shape=[jax.ShapeDtypeStruct((m, val_w), BF16),
                   jax.ShapeDtypeStruct((batch, heads, dv, dk), F32)],
        scratch_shapes=[pltpu.VMEM((heads, dv, dk), F32)],
        compiler_params=_params("arbitrary", "arbitrary"),
        name="gla",
    )(qkv, qkv, qkv, log_a, gate, norm_w, s0_t)


def kernel(x_prompt, x_sample, cache_k, cache_v, cache_logf, state_conv, state_gla, page_table,
           c_prompt, c_sample, w_in0, f_bias, q_norm, k_norm, sgu_norm, sgu_w, sgu_b, w_out0,
           w_in1, conv_w, gla_a_w2, gla_a_b, gla_norm, w_out1, ada_w, ada_b, mlp_w1, mlp_w2):
    nbp, seq, d = x_prompt.shape
    nbs, dseq, _ = x_sample.shape
    mp, ms = nbp * seq, nbs * dseq
    fox_w = d // 2
    heads = fox_w // HEAD_DIM
    sgu_width = d // 2
    groups = sgu_w.shape[1]
    conv_width = state_conv.shape[-1]
    gla_heads, gla_dk, gla_dv = state_gla.shape[2], state_gla.shape[3], state_gla.shape[4]
    gla_key, gla_val = gla_heads * gla_dk, gla_heads * gla_dv
    gla_rank = gla_a_w2.shape[1]
    n_pool, page = cache_k.shape[1], cache_k.shape[2]
    n_pages = page_table.shape[1]

    r_c = nbp + nbs
    r_pad = -(-r_c // 16) * 16
    c_rows = jnp.pad(jnp.concatenate([c_prompt, c_sample], axis=0),
                     ((0, r_pad - r_c), (0, 0))).astype(BF16)
    mod = _ada(c_rows, ada_w, ada_b)

    def mods(layer):
        parts = jnp.split(mod[layer], 6, axis=-1)
        pp = [p[:nbp].reshape(nbp, 1, d) for p in parts]
        ps = [jnp.repeat(p[nbp:r_c], dseq, axis=0) for p in parts]
        return pp, ps

    xp = x_prompt.reshape(mp, d)
    xs = x_sample.reshape(ms, d)

    def modulate_both(xp, xs, shp, scp, shs, scs):
        hp = _modulate(xp, shp, scp, seq)
        hs = _modulate(xs, shs.reshape(1, ms, d), scs.reshape(1, ms, d), ms, tr=ms)
        return hp, hs

    def mlp(xp, xs, layer, shp, scp, gp, shs, scs, gs):
        hp, hs = modulate_both(xp, xs, shp, scp, shs, scs)
        (ap,), (as_,) = _proj([hp], [hs], [(mlp_w1, layer, 0)], mlp_w1.shape[2], _epi_relu2,
                              [BF16], name="mlp_up")
        return _down(ap, as_, mlp_w2, layer, xp, gp, xs, gs, seq)

    def residual_proj(lhs_p, lhs_s, w, xp, xs, gp, gs, name):
        (yp,), (ys,) = _proj(lhs_p, lhs_s, [(w, 0, 0)], d, _epi_residual, [F32],
                             ext=[("res_p", "prow", xp), ("gate_p", "pgrp", gp),
                                  ("res_s", "srow", xs), ("gate_s", "srow", gs)],
                             rows_per_group=seq, name=name)
        return yp, ys

    (sh1p, sc1p, g1p, sh2p, sc2p, g2p), (sh1s, sc1s, g1s, sh2s, sc2s, g2s) = mods(0)
    hp, hs = modulate_both(xp, xs, sh1p, sc1p, sh1s, sc1s)
    w0 = jnp.swapaxes(w_in0, 1, 2)
    qn = jnp.tile(q_norm[0], heads).reshape(1, fox_w)
    kn = jnp.tile(k_norm[0], heads).reshape(1, fox_w)
    (q_p,), (q_s,) = _proj([hp], [hs], [(w0, 0, 0)], fox_w, _epi_head_norm, [BF16],
                           ext=[("norm", "col", qn)], w_rows=True, name="in0_q")
    (k_p,), (k_s,) = _proj([hp], [hs], [(w0, 0, fox_w)], fox_w, _epi_head_norm, [F32],
                           ext=[("norm", "col", kn)], w_rows=True, name="in0_k")
    (v_p,), (v_s,) = _proj([hp], [hs], [(w0, 0, 2 * fox_w)], fox_w, _epi_plain, [F32],
                           w_rows=True, name="in0_v")
    fb = jnp.pad(f_bias[0], (0, LANES - heads)).reshape(1, LANES)
    (lf_p,), (lf_s,) = _proj([hp], [hs], [(w0, 0, 3 * fox_w)], LANES, _epi_log_forget, [F32],
                             ext=[("bias", "col", fb)], w_rows=True, name="in0_logf")
    (uv_p,), (uv_s,) = _proj([hp], [hs], [(w0, 0, 3 * fox_w + heads)], 2 * sgu_width, _epi_gelu,
                             [BF16], w_rows=True, name="in0_uv")

    f_col = _cumsum_rows(lf_p, nbp)
    lf_s16 = lf_s[:, :heads].reshape(nbs, dseq, heads)
    tpad = 8
    lf_s_pad = jnp.pad(lf_s.reshape(nbs, dseq, LANES), ((0, 0), (0, tpad - dseq), (0, 0)))
    fn = _cumsum_rows(lf_s_pad.reshape(nbs * tpad, LANES), nbs).reshape(
        nbs, tpad, LANES)[:, :dseq, :heads]
    rows_s = dseq * heads
    assert rows_s <= LANES

    def new_rows(a):
        a = a.reshape(nbs, rows_s, HEAD_DIM)
        return jnp.pad(a, ((0, 0), (0, LANES - rows_s), (0, 0))).astype(BF16)

    fs_lane = jnp.pad(fn.reshape(nbs, 1, rows_s), ((0, 0), (0, 0), (0, LANES - rows_s)))
    oa_p, oa_rows, k5_p, v5_p = _fox(q_p, k_p, v_p, f_col, nbp, seq, heads,
                         page_table, q_s.reshape(nbs, rows_s, HEAD_DIM),
                         cache_k[0].reshape(n_pool * page * heads, HEAD_DIM),
                         cache_v[0].reshape(n_pool * page * heads, HEAD_DIM),
                         cache_logf[0].reshape(n_pool, page * heads // LANES, LANES),
                         new_rows(k_s), new_rows(v_s), fn.reshape(nbs, rows_s, 1), fs_lane)
    oa_s = oa_rows.reshape(ms, fox_w).astype(BF16)

    sn = sgu_norm[0].reshape(1, sgu_width)
    rows_p = min(seq, SGU_CHUNK)
    (ob_p,) = _sgu(uv_p, sn, sgu_w[0][:, :rows_p, :rows_p], sgu_b[0][:, :rows_p].T, rows_p, False)
    rows_g = min(dseq, SGU_CHUNK)
    w_small = sgu_w[0][:, :rows_g, :rows_g]
    w_big = jnp.einsum("ab,gts->gatbs", jnp.eye(nbs, dtype=F32), w_small).reshape(groups, ms, ms)
    b_big = jnp.tile(sgu_b[0][:, :rows_g].T, (nbs, 1))
    ob_s, sgu_v = _sgu(uv_s, sn, w_big, b_big, ms, True)

    xp, xs = residual_proj([oa_p, ob_p], [oa_s, ob_s], w_out0, xp, xs, g1p, g1s, "out0")
    xp, xs = mlp(xp, xs, 0, sh2p, sc2p, g2p, sh2s, sc2s, g2s)

    (sh1p, sc1p, g1p, sh2p, sc2p, g2p), (sh1s, sc1s, g1s, sh2s, sc2s, g2s) = mods(1)
    hp, hs = modulate_both(xp, xs, sh1p, sc1p, sh1s, sc1s)
    w1 = jnp.swapaxes(w_in1, 1, 2)
    cw = conv_width
    cwt = conv_w[0]
    n_tap = cwt.shape[0]
    (oc_p, ztail_p), (gb_s, z_s) = _proj(
        [hp], [hs], [(w1, 0, 0), (w1, 0, cw), (w1, 0, 2 * cw)], cw, _make_epi_conv(seq),
        [BF16, F32], ext=[("taps", "col", cwt)], tm=512, tn=256, rows_per_group=seq, w_rows=True,
        tail_rows=(1,), scratch=lambda tm, tn: [pltpu.VMEM((tm + 8, tn), F32)], name="in1_conv")
    conv_p = ztail_p.reshape(nbp, -1, 8, cw)[:, -1, 8 - (n_tap - 1):, :]
    (qkv_p,), (qkv_s,) = _proj([hp], [hs], [(w1, 0, 3 * cw)], 2 * gla_key + gla_val, _epi_plain,
                               [BF16], w_rows=True, name="in1_qkv")
    c_ga = 3 * cw + 2 * gla_key + gla_val
    (ga_p,), (ga_s,) = _proj([hp], [hs], [(w1, 0, c_ga)], LANES, _epi_plain, [BF16],
                             w_rows=True, name="in1_ga")
    w_a2 = jnp.pad(gla_a_w2, ((0, 0), (0, LANES - gla_rank), (0, 0)))
    (la_p,), (la_s,) = _proj([ga_p], [ga_s], [(w_a2, 0, 0)], gla_key, _epi_log_decay, [F32],
                             ext=[("bias", "col", gla_a_b[0].reshape(1, gla_key))], name="in1_log_a")
    (gg_p,), (gg_s,) = _proj([hp], [hs], [(w1, 0, c_ga + gla_rank)], gla_val, _epi_silu, [BF16],
                             w_rows=True, name="in1_gate")

    zp_s = jnp.concatenate([state_conv[0], z_s.reshape(nbs, dseq, cw)], axis=1)
    shifted = [zp_s[:, i:i + dseq].reshape(ms, cw) for i in range(cwt.shape[0])]
    oc_s = _conv_sample(shifted[0], shifted[1], shifted[2], gb_s, cwt)
    conv_s = zp_s[:, dseq:, :]

    gn = gla_norm[0].reshape(1, gla_val)
    s0_p = jnp.zeros((nbp, gla_heads, gla_dv, gla_dk), F32)
    od_p, st_p = _gla(qkv_p, la_p, gg_p, gn, s0_p, nbp, seq, gla_heads, GLA_CHUNK_ROWS)
    pad_t = 16

    def pad_rows(a):
        return jnp.pad(a.reshape(nbs, dseq, -1), ((0, 0), (0, pad_t - dseq), (0, 0))).reshape(
            nbs * pad_t, -1)

    od_s_pad, st_s = _gla(pad_rows(qkv_s), pad_rows(la_s), pad_rows(gg_s), gn,
                          state_gla[0].swapaxes(-1, -2), nbs, pad_t, gla_heads, pad_t)
    od_s = od_s_pad.reshape(nbs, pad_t, gla_val)[:, :dseq].reshape(ms, gla_val)

    xp, xs = residual_proj([oc_p, od_p], [oc_s, od_s], w_out1, xp, xs, g1p, g1s, "out1")
    xp, xs = mlp(xp, xs, 1, sh2p, sc2p, g2p, sh2s, sc2s, g2s)

    y_prompt = xp.reshape(nbp, seq, d)
    y_sample = xs.reshape(nbs, dseq, d)
    return (y_prompt, y_sample,
            k5_p.reshape(1, nbp, seq, heads, HEAD_DIM), v5_p.reshape(1, nbp, seq, heads, HEAD_DIM),
            lf_p[:, :heads].reshape(1, nbp, seq, heads),
            k_s.reshape(1, nbs, dseq, heads, HEAD_DIM), v_s.reshape(1, nbs, dseq, heads, HEAD_DIM),
            lf_s16[None], sgu_v.reshape(1, nbs, dseq, sgu_width),
            conv_p[None], conv_s[None],
            st_p.swapaxes(-1, -2)[None], st_s.swapaxes(-1, -2)[None])
```

```python
import functools
import math

import jax
import jax.numpy as jnp
from jax import lax
from jax.experimental import pallas as pl
from jax.experimental.pallas import tpu as pltpu

F32 = jnp.float32
BF16 = jnp.bfloat16
EPS = 1e-6
NEG_INF = -1e30
HEAD_DIM = 128
LANES = 128
SGU_CHUNK = 128
GLA_TAU = 16.0
GLA_CHUNK_ROWS = 128
FOX_PAGES_PER_STEP = 4
VMEM_LIMIT_BYTES = 56 * 1024 * 1024


def _params(*semantics):
    return pltpu.CompilerParams(dimension_semantics=semantics,
                                vmem_limit_bytes=VMEM_LIMIT_BYTES)


def _split3(x):
    hi = x.astype(BF16)
    r = x - hi.astype(F32)
    mid = r.astype(BF16)
    lo = (r - mid.astype(F32)).astype(BF16)
    return hi, mid, lo


def _log_sigmoid(x):
    return jnp.minimum(x, 0.0) - jnp.log1p(jnp.exp(-jnp.abs(x)))


def _gelu_tanh(x):
    c = 0.7978845608028654
    return 0.5 * x * (1.0 + jnp.tanh(c * (x + 0.044715 * (x * x * x))))


def _ada_kernel(c_ref, w_ref, b_ref, o_ref):
    o_ref[...] = jnp.dot(c_ref[...], w_ref[...].astype(BF16),
                         preferred_element_type=F32) + b_ref[...]


def _ada(c_rows, ada_w, ada_b, tn=512):
    n_layers, d, n = ada_w.shape
    r = c_rows.shape[0]
    tn = min(tn, n)
    return pl.pallas_call(
        _ada_kernel,
        grid=(n_layers, n // tn),
        in_specs=[pl.BlockSpec((r, d), lambda l, j: (0, 0)),
                  pl.BlockSpec((None, d, tn), lambda l, j: (l, 0, j)),
                  pl.BlockSpec((None, 1, tn), lambda l, j: (l, 0, j))],
        out_specs=pl.BlockSpec((None, r, tn), lambda l, j: (l, 0, j)),
        out_shape=jax.ShapeDtypeStruct((n_layers, r, n), F32),
        compiler_params=_params("arbitrary", "arbitrary"),
        name="ada",
    )(c_rows, ada_w, ada_b.reshape(n_layers, 1, n))


def _modulate_kernel(x_ref, sh_ref, sc_ref, o_ref):
    x = x_ref[...]
    y = x * lax.rsqrt(jnp.mean(x * x, axis=-1, keepdims=True) + EPS)
    o_ref[...] = (y * (1.0 + sc_ref[...]) + sh_ref[...]).astype(o_ref.dtype)


def _modulate(x, sh, sc, rows_per_group, tr=256):
    rows, d = x.shape
    tr = min(tr, rows)
    r = sh.shape[1]
    grp = lambda i: ((i * tr) // rows_per_group, 0, 0)
    return pl.pallas_call(
        _modulate_kernel,
        grid=(rows // tr,),
        in_specs=[pl.BlockSpec((tr, d), lambda i: (i, 0)),
                  pl.BlockSpec((None, r, d), grp),
                  pl.BlockSpec((None, r, d), grp)],
        out_specs=pl.BlockSpec((tr, d), lambda i: (i, 0)),
        out_shape=jax.ShapeDtypeStruct((rows, d), BF16),
        compiler_params=_params("arbitrary"),
        name="modulate",
    )(x, sh, sc)


def _proj_kernel(*refs, k_sizes, n_w, ext_names, n_out, epilogue, cast_rows, w_rows, n_scratch):
    n_lhs = len(k_sizes)
    pos = 0
    xp = refs[pos:pos + n_lhs]; pos += n_lhs
    xs = refs[pos:pos + n_lhs]; pos += n_lhs
    w = refs[pos:pos + n_w]; pos += n_w
    ext = dict(zip(ext_names, refs[pos:pos + len(ext_names)])); pos += len(ext_names)
    out_p = refs[pos:pos + n_out]; pos += n_out
    out_s = refs[pos:pos + n_out]; pos += n_out
    wbf = refs[pos:pos + n_w]; pos += n_w
    ext["scratch"] = refs[pos:pos + n_scratch]

    def accumulate(lhs):
        accs = []
        for m in range(n_w):
            acc = None
            off = 0
            for a, ka in enumerate(k_sizes):
                part = jnp.dot(lhs[a][...], wbf[m][off:off + ka, :],
                               preferred_element_type=F32)
                acc = part if acc is None else acc + part
                off += ka
            accs.append(acc)
        return accs

    def store(out_refs, vals):
        for o_ref, val in zip(out_refs, vals):
            o_ref[...] = val.astype(o_ref.dtype)

    @pl.when(pl.program_id(1) == 0)
    def _():
        accs_p, accs_s = [None] * n_w, [None] * n_w
        r0 = 0
        for a, ka in enumerate(k_sizes):
            for loc in range(0, ka, cast_rows):
                for m in range(n_w):
                    if w_rows:
                        chunk = w[m][:, r0:r0 + cast_rows].T.astype(BF16)
                    else:
                        chunk = w[m][r0:r0 + cast_rows, :].astype(BF16)
                    wbf[m][r0:r0 + cast_rows, :] = chunk
                    pp = jnp.dot(xp[a][:, loc:loc + cast_rows], chunk, preferred_element_type=F32)
                    ps = jnp.dot(xs[a][:, loc:loc + cast_rows], chunk, preferred_element_type=F32)
                    accs_p[m] = pp if accs_p[m] is None else accs_p[m] + pp
                    accs_s[m] = ps if accs_s[m] is None else accs_s[m] + ps
                r0 += cast_rows
        store(out_s, epilogue(accs_s, ext, True))
        store(out_p, epilogue(accs_p, ext, False))

    @pl.when(pl.program_id(1) != 0)
    def _():
        store(out_p, epilogue(accumulate(xp), ext, False))


def _proj(xp, xs, weights, n_cols, epilogue, out_dtypes, ext=(), *, tm=1024, tn=512,
          rows_per_group=None, w_rows=False, tail_rows=(), scratch=None, name="proj"):
    mp, ms = xp[0].shape[0], xs[0].shape[0]
    k_sizes = tuple(int(a.shape[1]) for a in xp)
    k_total = sum(k_sizes)
    tm = min(tm, mp, rows_per_group or mp)
    tn = min(tn, n_cols)
    assert mp % tm == 0 and n_cols % tn == 0
    for warr, _, c0 in weights:
        assert warr.ndim == 3
        if w_rows:
            assert warr.shape[2] == k_total and c0 % 8 == 0 and len(k_sizes) == 1
        else:
            assert warr.shape[1] == k_total and c0 % tn == 0
    in_specs, args = [], []
    for a in xp:
        in_specs.append(pl.BlockSpec((tm, a.shape[1]), lambda j, i: (i, 0)))
        args.append(a)
    for a in xs:
        in_specs.append(pl.BlockSpec((ms, a.shape[1]), lambda j, i: (0, 0)))
        args.append(a)
    for warr, layer, c0 in weights:
        if w_rows:
            spec = pl.BlockSpec((None, pl.Element(tn), pl.Element(k_total)),
                                lambda j, i, c=c0, l=layer: (l, pl.multiple_of(c + j * tn, 8), 0))
        else:
            spec = pl.BlockSpec((None, k_total, tn),
                                lambda j, i, cb=c0 // tn, l=layer: (l, 0, j + cb))
        in_specs.append(spec)
        args.append(warr)
    ext_names = []
    for ename, kind, arr in ext:
        ext_names.append(ename)
        if kind == "col":
            spec = pl.BlockSpec((arr.shape[0], tn), lambda j, i: (0, j))
        elif kind == "prow":
            spec = pl.BlockSpec((tm, tn), lambda j, i: (i, j))
        elif kind == "pgrp":
            spec = pl.BlockSpec((None, 1, tn),
                                lambda j, i: ((i * tm) // rows_per_group, 0, j))
        elif kind == "srow":
            spec = pl.BlockSpec((ms, tn), lambda j, i: (0, j))
        else:
            raise ValueError(kind)
        in_specs.append(spec)
        args.append(arr)
    n_out = len(out_dtypes)
    rows_p = [(8 * (mp // tm), 8) if o in tail_rows else (mp, tm) for o in range(n_out)]
    out_specs = ([pl.BlockSpec((br, tn), lambda j, i: (i, j)) for _, br in rows_p]
                 + [pl.BlockSpec((ms, tn), lambda j, i: (0, j))] * n_out)
    out_shape = ([jax.ShapeDtypeStruct((r, n_cols), dt) for (r, _), dt in zip(rows_p, out_dtypes)]
                 + [jax.ShapeDtypeStruct((ms, n_cols), dt) for dt in out_dtypes])
    cast_rows = min(512, *k_sizes)
    assert all(ka % cast_rows == 0 for ka in k_sizes)
    extra_scratch = scratch(tm, tn) if scratch else []
    outs = pl.pallas_call(
        functools.partial(_proj_kernel, k_sizes=k_sizes, n_w=len(weights),
                          ext_names=tuple(ext_names), n_out=n_out, epilogue=epilogue,
                          cast_rows=cast_rows, w_rows=w_rows, n_scratch=len(extra_scratch)),
        grid=(n_cols // tn, mp // tm),
        in_specs=in_specs,
        out_specs=out_specs,
        out_shape=out_shape,
        scratch_shapes=[pltpu.VMEM((k_total, tn), BF16) for _ in weights] + extra_scratch,
        compiler_params=_params("arbitrary", "arbitrary"),
        name=name,
    )(*args)
    return outs[:n_out], outs[n_out:]


def _epi_plain(accs, ext, is_sample):
    return (accs[0],)


def _epi_head_norm(accs, ext, is_sample):
    acc = accs[0]
    wn = ext["norm"][...]
    parts = []
    for c0 in range(0, acc.shape[1], HEAD_DIM):
        a = acc[:, c0:c0 + HEAD_DIM]
        y = a * lax.rsqrt(jnp.mean(a * a, axis=-1, keepdims=True) + EPS)
        parts.append(y * wn[:, c0:c0 + HEAD_DIM])
    return (jnp.concatenate(parts, axis=1),)


def _epi_log_forget(accs, ext, is_sample):
    return (_log_sigmoid(accs[0] + ext["bias"][...]),)


def _epi_gelu(accs, ext, is_sample):
    return (_gelu_tanh(accs[0]),)


def _epi_silu(accs, ext, is_sample):
    a = accs[0]
    return (a * jax.nn.sigmoid(a),)


def _epi_relu2(accs, ext, is_sample):
    r = jnp.maximum(accs[0], 0.0)
    return (r * r,)


def _make_epi_conv(seq):
    def epilogue(accs, ext, is_sample):
        z = accs[1] * accs[0]
        if is_sample:
            return accs[2], z
        (zbuf,) = ext["scratch"]
        tm = z.shape[0]

        @pl.when(pl.program_id(1) % (seq // tm) == 0)
        def _():
            zbuf[0:8, :] = jnp.zeros((8, zbuf.shape[1]), F32)

        zbuf[8:8 + tm, :] = z
        w = ext["taps"][...]
        y = w[0:1] * zbuf[6:6 + tm, :] + w[1:2] * zbuf[7:7 + tm, :] + w[2:3] * z
        tail = z[tm - 8:tm, :]
        zbuf[0:8, :] = tail
        return accs[2] * y, tail

    return epilogue


def _epi_log_decay(accs, ext, is_sample):
    return (_log_sigmoid(accs[0] + ext["bias"][...]) / GLA_TAU,)


def _epi_residual(accs, ext, is_sample):
    if is_sample:
        return (ext["res_s"][...] + ext["gate_s"][...] * accs[0],)
    return (ext["res_p"][...] + ext["gate_p"][...] * accs[0],)


def _down_kernel(xp_ref, xs_ref, w_ref, resp_ref, gp_ref, ress_ref, gs_ref,
                 op_ref, os_ref, accs_ref, wbf, *, nk, cast_rows):
    i = pl.program_id(1)
    k = pl.program_id(2)

    @pl.when(k == 0)
    def _():
        op_ref[...] = jnp.zeros_like(op_ref)

    total = None
    for r0 in range(0, w_ref.shape[0], cast_rows):
        wbf[r0:r0 + cast_rows, :] = w_ref[r0:r0 + cast_rows, :].astype(BF16)
        part = jnp.dot(xp_ref[:, r0:r0 + cast_rows], wbf[r0:r0 + cast_rows, :],
                       preferred_element_type=F32)
        total = part if total is None else total + part
    op_ref[...] += total

    @pl.when(k == nk - 1)
    def _():
        op_ref[...] = resp_ref[...] + gp_ref[...] * op_ref[...]

    @pl.when(i == 0)
    def _():
        ps = jnp.dot(xs_ref[...], wbf[...], preferred_element_type=F32)

        @pl.when(k == 0)
        def _():
            accs_ref[...] = ps

        @pl.when(k > 0)
        def _():
            accs_ref[...] += ps

        @pl.when(k == nk - 1)
        def _():
            os_ref[...] = ress_ref[...] + gs_ref[...] * accs_ref[...]


def _down(xp, xs, w, layer, res_p, gate_p, res_s, gate_s, rows_per_group, *,
          tm=1024, tn=1024, tk=2048):
    mp, kdim = xp.shape
    ms = xs.shape[0]
    n = w.shape[2]
    tm, tn, tk = min(tm, rows_per_group, mp), min(tn, n), min(tk, kdim)
    nk = kdim // tk
    cast_rows = min(512, tk)
    assert tk % cast_rows == 0
    return pl.pallas_call(
        functools.partial(_down_kernel, nk=nk, cast_rows=cast_rows),
        grid=(n // tn, mp // tm, nk),
        in_specs=[pl.BlockSpec((tm, tk), lambda j, i, k: (i, k)),
                  pl.BlockSpec((ms, tk), lambda j, i, k: (0, k)),
                  pl.BlockSpec((None, tk, tn), lambda j, i, k: (layer, k, j)),
                  pl.BlockSpec((tm, tn), lambda j, i, k: (i, j)),
                  pl.BlockSpec((None, 1, tn),
                               lambda j, i, k: ((i * tm) // rows_per_group, 0, j)),
                  pl.BlockSpec((ms, tn), lambda j, i, k: (0, j)),
                  pl.BlockSpec((ms, tn), lambda j, i, k: (0, j))],
        out_specs=[pl.BlockSpec((tm, tn), lambda j, i, k: (i, j)),
                   pl.BlockSpec((ms, tn), lambda j, i, k: (0, j))],
        out_shape=[jax.ShapeDtypeStruct((mp, n), F32),
                   jax.ShapeDtypeStruct((ms, n), F32)],
        scratch_shapes=[pltpu.VMEM((ms, tn), F32), pltpu.VMEM((tk, tn), BF16)],
        compiler_params=_params("arbitrary", "arbitrary", "arbitrary"),
        name="mlp_down",
    )(xp, xs, w, res_p, gate_p, res_s, gate_s)


def _cumsum_kernel(x_ref, o_ref, carry_ref):
    @pl.when(pl.program_id(1) == 0)
    def _():
        carry_ref[...] = jnp.zeros_like(carry_ref)

    x = x_ref[...]
    t, w = x.shape
    r = lax.broadcasted_iota(jnp.int32, (t, t), 0)
    c = lax.broadcasted_iota(jnp.int32, (t, t), 1)
    tri = jnp.where(c <= r, 1.0, 0.0).astype(BF16)
    y = jnp.dot(tri, jnp.concatenate(_split3(x), axis=1), preferred_element_type=F32)
    out = y[:, :w] + y[:, w:2 * w] + y[:, 2 * w:] + carry_ref[...]
    o_ref[...] = out
    carry_ref[...] = out[t - 1:t, :]


def _cumsum_rows(x, n_groups, tc=256):
    rows, w = x.shape
    per = rows // n_groups
    tc = min(tc, per)
    nt = per // tc
    return pl.pallas_call(
        _cumsum_kernel,
        grid=(n_groups, nt),
        in_specs=[pl.BlockSpec((tc, w), lambda b, t: (b * nt + t, 0))],
        out_specs=pl.BlockSpec((tc, w), lambda b, t: (b * nt + t, 0)),
        out_shape=jax.ShapeDtypeStruct((rows, w), F32),
        scratch_shapes=[pltpu.VMEM((1, w), F32)],
        compiler_params=_params("arbitrary", "arbitrary"),
        name="cumsum_logf",
    )(x)


def _bias_lanes(col, own_first):
    hi, mid, lo = _split3(col)
    lane = lax.broadcasted_iota(jnp.int32, (col.shape[0], LANES), 1)
    own, other = (0, 3) if own_first else (3, 0)
    x = jnp.where(lane == own, hi.astype(F32),
                  jnp.where(lane == own + 1, mid.astype(F32),
                            jnp.where(lane == own + 2, lo.astype(F32), 0.0)))
    x = jnp.where((lane >= other) & (lane < other + 3), 1.0, x)
    return x.astype(BF16)


def _fox_prompt_step(q_ref, k_ref, v_ref, fq_ref, fk_ref, o_ref, k5_hbm, v5_hbm, kaug, vbf, sem,
                     *, tq, nq, scale, alongside):
    b = pl.program_id(0)
    h = pl.program_id(1)
    qi = pl.program_id(2)
    seq = k_ref.shape[0]

    def head_copy(src_ref, dst_hbm, slot):
        return pltpu.make_async_copy(src_ref, dst_hbm.at[pl.ds(b * seq, seq), h], sem.at[slot])

    def head_column(f):
        lane = lax.broadcasted_iota(jnp.int32, f.shape, 1)
        return jnp.sum(jnp.where(lane == h, f, 0.0), axis=1, keepdims=True)

    @pl.when(qi == 0)
    def _():
        head_copy(k_ref, k5_hbm, 0).start()
        head_copy(v_ref, v5_hbm, 1).start(priority=1)
        kaug[:, :HEAD_DIM] = k_ref[...].astype(BF16)
        kaug[:, HEAD_DIM:] = _bias_lanes(head_column(fk_ref[...]) * (-1.0 / scale), False)
        vbf[...] = v_ref[...].astype(BF16)

    q = jnp.concatenate(
        [q_ref[...], _bias_lanes(head_column(fq_ref[...]) * (1.0 / scale), True)], axis=1)
    c2 = scale * 1.4426950408889634

    def step(j, carry, masked):
        m, l, acc = carry
        start = j * tq
        s = lax.dot_general(q, kaug[pl.ds(start, tq), :], (((1,), (1,)), ((), ())),
                            preferred_element_type=F32) * c2
        if masked:
            r = lax.broadcasted_iota(jnp.int32, (tq, tq), 0)
            c = lax.broadcasted_iota(jnp.int32, (tq, tq), 1)
            s = jnp.where(c <= r, s, NEG_INF)
        m_new = jnp.maximum(m, jnp.max(s, axis=1, keepdims=True))
        alpha = jnp.exp2(m - m_new)
        p = jnp.exp2(s - m_new)
        l = alpha * l + jnp.sum(p, axis=1, keepdims=True)
        acc = alpha * acc + jnp.dot(p.astype(BF16), vbf[pl.ds(start, tq), :],
                                    preferred_element_type=F32)
        return m_new, l, acc

    init = (jnp.full((tq, 1), NEG_INF, F32), jnp.zeros((tq, 1), F32),
            jnp.zeros((tq, HEAD_DIM), F32))
    for n_full in range(nq):
        @pl.when(qi == n_full)
        def _(n_full=n_full):
            alongside()
            carry = init
            for j in range(n_full):
                carry = step(j, carry, False)
            _, l, acc = step(n_full, carry, True)
            o_ref[...] = (acc / l).astype(o_ref.dtype)
            if n_full == 0:
                head_copy(k_ref, k5_hbm, 0).wait()
                head_copy(v_ref, v5_hbm, 1).wait()


def _fox_sample_step(p, live, q_ref, *refs, n_steps, pps, heads, scale):
    ck_refs, cv_refs, lf_refs = refs[:pps], refs[pps:2 * pps], refs[2 * pps:3 * pps]
    kn_ref, vn_ref, fn_ref, fs_ref, o_ref, m_ref, l_ref, acc_ref, tail_ref = refs[3 * pps:]
    first = p == 0

    def carried(ref, start):
        return jnp.where(first, start, ref[...])

    def keep(ref, new, old):
        ref[...] = jnp.where(live, new, old)

    q = q_ref[...]
    rows = q.shape[0]
    fn = fn_ref[...]
    row = lax.broadcasted_iota(jnp.int32, (rows, LANES), 0)
    lane = lax.broadcasted_iota(jnp.int32, (rows, LANES), 1)
    own_head = (lane & (heads - 1)) == (row & (heads - 1))

    log2e = 1.4426950408889634

    def attend(keys, vals, lane_bias, visible, paged):
        s = lax.dot_general(q, keys, (((1,), (1,)), ((), ())), preferred_element_type=F32)
        bias2 = lane_bias * log2e
        base = jnp.where(visible, fn * log2e, NEG_INF)
        s = jnp.concatenate(
            [s[:, g * LANES:(g + 1) * LANES] * (scale * log2e) + (base + bias2[g:g + 1, :])
             for g in range(keys.shape[0] // LANES)], axis=1)
        if paged:
            m_old, l_old, acc_old = (carried(m_ref, NEG_INF), carried(l_ref, 0.0),
                                     carried(acc_ref, 0.0))
        else:
            m_old, l_old, acc_old = m_ref[...], l_ref[...], acc_ref[...]
        m_new = jnp.maximum(m_old, jnp.max(s, axis=1, keepdims=True))
        alpha = jnp.exp2(m_old - m_new)
        pr = jnp.exp2(s - m_new)
        l_new = alpha * l_old + jnp.sum(pr, axis=1, keepdims=True)
        acc_new = alpha * acc_old + jnp.dot(pr.astype(BF16), vals, preferred_element_type=F32)
        if paged:
            keep(m_ref, m_new, m_ref[...])
            keep(l_ref, l_new, l_ref[...])
            keep(acc_ref, acc_new, acc_ref[...])
        else:
            m_ref[...], l_ref[...], acc_ref[...] = m_new, l_new, acc_new

    a = lax.broadcasted_iota(jnp.int32, (LANES, LANES), 0)
    b = lax.broadcasted_iota(jnp.int32, (LANES, LANES), 1)
    same_head = (a & (heads - 1)) == (b & (heads - 1))
    later_in_row = jnp.where(same_head & (a > b), 1.0, 0.0).astype(BF16)
    whole_row = jnp.where(same_head, 1.0, 0.0).astype(BF16)
    sums = jnp.concatenate([later_in_row, whole_row], axis=1)
    n_rows = lf_refs[0].shape[0]
    ra = lax.broadcasted_iota(jnp.int32, (n_rows, n_rows), 0)
    rb = lax.broadcasted_iota(jnp.int32, (n_rows, n_rows), 1)
    later_rows = jnp.where(rb > ra, 1.0, 0.0).astype(BF16)

    tail = carried(tail_ref, 0.0)
    biases = []
    for c in range(pps):
        lf = lf_refs[c][...]
        y = jnp.dot(jnp.concatenate(_split3(lf), axis=0), sums, preferred_element_type=F32)
        y = y[:n_rows] + y[n_rows:2 * n_rows] + y[2 * n_rows:]
        within, row_tot = y[:, :LANES], y[:, LANES:]
        z = jnp.dot(later_rows, jnp.concatenate(_split3(row_tot), axis=1),
                    preferred_element_type=F32)
        biases.append(within + z[:, :LANES] + z[:, LANES:2 * LANES] + z[:, 2 * LANES:] + tail)
        tail = tail + jnp.sum(row_tot, axis=0, keepdims=True)
    keep(tail_ref, tail, tail_ref[...])

    attend(jnp.concatenate([r[...].astype(BF16) for r in ck_refs], axis=0),
           jnp.concatenate([r[...].astype(BF16) for r in cv_refs], axis=0),
           jnp.concatenate(biases, axis=0), own_head, True)

    @pl.when((p == n_steps - 1) & live)
    def _():
        attend(kn_ref[...], vn_ref[...], -fs_ref[...],
               own_head & (lane - (lane & (heads - 1)) <= row - (row & (heads - 1))), False)
        o_ref[...] = acc_ref[...] / l_ref[...]


def _fox_kernel(pt_ref, *refs, n_prompt_in, n_sample_in, nq, heads, n_steps, n_live, tq, pps, scale):
    del pt_ref
    prompt_in = refs[:n_prompt_in]
    sample_in = refs[n_prompt_in:n_prompt_in + n_sample_in]
    (op_ref, os_ref, k5_hbm, v5_hbm, kaug, vbf, m_ref, l_ref, acc_ref, tail_ref,
     sem) = refs[n_prompt_in + n_sample_in:]
    g = (pl.program_id(0) * heads + pl.program_id(1)) * nq + pl.program_id(2)

    @pl.when(g == 0)
    def _():
        m_ref[...] = jnp.full(m_ref.shape, NEG_INF, F32)
        l_ref[...] = jnp.zeros_like(l_ref)
        acc_ref[...] = jnp.zeros_like(acc_ref)
        tail_ref[...] = jnp.zeros_like(tail_ref)

    live = g < n_live
    p = jnp.minimum(g, n_live - 1) % n_steps

    def sample_step():
        _fox_sample_step(p, live, *sample_in, os_ref, m_ref, l_ref, acc_ref, tail_ref,
                         n_steps=n_steps, pps=pps, heads=heads, scale=scale)

    _fox_prompt_step(*prompt_in, op_ref, k5_hbm, v5_hbm, kaug, vbf, sem, tq=tq, nq=nq,
                     scale=scale, alongside=sample_step)


def _fox(q, k, v, f_col, batch, seq, heads, page_table, q_rows, cache_k, cache_v, cache_lf,
         k_new, v_new, fn_col, fs_lane, tq=512):
    tq = min(tq, seq)
    nq = seq // tq
    nbs, n_pages = page_table.shape
    rows, hd = q_rows.shape[1], q_rows.shape[2]
    lf_rows = cache_lf.shape[1]
    page_rows = lf_rows * LANES
    assert heads & (heads - 1) == 0 and LANES % heads == 0
    pps = math.gcd(n_pages, FOX_PAGES_PER_STEP)
    n_steps = n_pages // pps
    n_live = nbs * n_steps
    assert n_live <= batch * heads * nq

    def walk(b, h, i):
        g = jnp.minimum((b * heads + h) * nq + i, n_live - 1)
        return g // n_steps, g % n_steps

    def per_bs(*tail):
        return lambda b, h, i, pt: (walk(b, h, i)[0],) + tail

    def page_of(c):
        def f(b, h, i, pt):
            bs, p = walk(b, h, i)
            return pt[bs, n_pages - 1 - (p * pps + c)]
        return f

    kv_specs = [pl.BlockSpec((page_rows, hd), lambda b, h, i, pt, f=page_of(c): (f(b, h, i, pt), 0))
                for c in range(pps)]
    lf_specs = [pl.BlockSpec((None, lf_rows, LANES),
                             lambda b, h, i, pt, f=page_of(c): (f(b, h, i, pt), 0, 0))
                for c in range(pps)]
    prompt_specs = [pl.BlockSpec((tq, HEAD_DIM), lambda b, h, i, pt: (b * nq + i, h)),
                    pl.BlockSpec((seq, HEAD_DIM), lambda b, h, i, pt: (b, h)),
                    pl.BlockSpec((seq, HEAD_DIM), lambda b, h, i, pt: (b, h)),
                    pl.BlockSpec((tq, LANES), lambda b, h, i, pt: (b * nq + i, 0)),
                    pl.BlockSpec((seq, LANES), lambda b, h, i, pt: (b, 0))]
    sample_specs = ([pl.BlockSpec((None, rows, hd), per_bs(0, 0))] + kv_specs + kv_specs + lf_specs
                    + [pl.BlockSpec((None, LANES, hd), per_bs(0, 0)),
                       pl.BlockSpec((None, LANES, hd), per_bs(0, 0)),
                       pl.BlockSpec((None, rows, 1), per_bs(0, 0)),
                       pl.BlockSpec((None, 1, LANES), per_bs(0, 0))])
    grid_spec = pltpu.PrefetchScalarGridSpec(
        num_scalar_prefetch=1,
        grid=(batch, heads, nq),
        in_specs=prompt_specs + sample_specs,
        out_specs=[pl.BlockSpec((tq, HEAD_DIM), lambda b, h, i, pt: (b * nq + i, h)),
                   pl.BlockSpec((None, rows, hd), per_bs(0, 0)),
                   pl.BlockSpec(memory_space=pl.ANY), pl.BlockSpec(memory_space=pl.ANY)],
        scratch_shapes=[pltpu.VMEM((seq, 2 * HEAD_DIM), BF16), pltpu.VMEM((seq, HEAD_DIM), BF16),
                        pltpu.VMEM((rows, 1), F32), pltpu.VMEM((rows, 1), F32),
                        pltpu.VMEM((rows, hd), F32), pltpu.VMEM((1, LANES), F32),
                        pltpu.SemaphoreType.DMA((2,))],
    )
    kv_heads = jax.ShapeDtypeStruct((batch * seq, heads, HEAD_DIM), F32)
    return pl.pallas_call(
        functools.partial(_fox_kernel, n_prompt_in=len(prompt_specs),
                          n_sample_in=len(sample_specs), nq=nq, heads=heads, n_steps=n_steps,
                          n_live=n_live, tq=tq, pps=pps, scale=HEAD_DIM ** -0.5),
        grid_spec=grid_spec,
        out_shape=[jax.ShapeDtypeStruct(q.shape, BF16),
                   jax.ShapeDtypeStruct((nbs, rows, hd), F32), kv_heads, kv_heads],
        compiler_params=_params("arbitrary", "arbitrary", "arbitrary"),
        name="fox",
    )(page_table, q, k, v, f_col, f_col, q_rows, *([cache_k] * pps), *([cache_v] * pps),
      *([cache_lf] * pps), k_new, v_new, fn_col, fs_lane)


def _sgu_kernel(u_ref, vg_ref, nw_ref, w_ref, bt_ref, *out_refs, groups):
    o_ref = out_refs[0]
    g = vg_ref[...].astype(F32)
    v = g * lax.rsqrt(jnp.mean(g * g, axis=-1, keepdims=True) + EPS) * nw_ref[...]
    if len(out_refs) > 1:
        out_refs[1][...] = v
    vb = v.astype(BF16)
    rows = v.shape[0]
    cw = v.shape[1] // groups
    r = lax.broadcasted_iota(jnp.int32, (rows, rows), 0)
    c = lax.broadcasted_iota(jnp.int32, (rows, rows), 1)
    bt = bt_ref[...]
    for gi in range(groups):
        wm = jnp.where(c <= r, w_ref[gi], 0.0).astype(BF16)
        z = jnp.dot(wm, vb[:, gi * cw:(gi + 1) * cw], preferred_element_type=F32)
        z = z + bt[:, gi:gi + 1]
        u = u_ref[:, gi * cw:(gi + 1) * cw].astype(F32)
        o_ref[:, gi * cw:(gi + 1) * cw] = (u * z).astype(o_ref.dtype)


def _sgu(uv, norm_w, w_pos, bias_t, rows, emit_v):
    m, two_w = uv.shape
    width = two_w // 2
    groups = w_pos.shape[0]
    out_shape = [jax.ShapeDtypeStruct((m, width), BF16)]
    out_specs = [pl.BlockSpec((rows, width), lambda i: (i, 0))]
    if emit_v:
        out_shape.append(jax.ShapeDtypeStruct((m, width), F32))
        out_specs.append(pl.BlockSpec((rows, width), lambda i: (i, 0)))
    return pl.pallas_call(
        functools.partial(_sgu_kernel, groups=groups),
        grid=(m // rows,),
        in_specs=[pl.BlockSpec((rows, width), lambda i: (i, 0)),
                  pl.BlockSpec((rows, width), lambda i: (i, 1)),
                  pl.BlockSpec((1, width), lambda i: (0, 0)),
                  pl.BlockSpec((groups, rows, rows), lambda i: (0, 0, 0)),
                  pl.BlockSpec((rows, groups), lambda i: (0, 0))],
        out_specs=out_specs,
        out_shape=out_shape,
        compiler_params=_params("arbitrary"),
        name="sgu",
    )(uv, uv, norm_w, w_pos, bias_t)


def _conv_sample_kernel(z0_ref, z1_ref, z2_ref, gb_ref, w_ref, o_ref):
    w = w_ref[...]
    y = w[0:1] * z0_ref[...] + w[1:2] * z1_ref[...] + w[2:3] * z2_ref[...]
    o_ref[...] = (gb_ref[...].astype(F32) * y).astype(o_ref.dtype)


def _conv_sample(z0, z1, z2, gb, conv_w):
    return pl.pallas_call(
        _conv_sample_kernel,
        out_shape=jax.ShapeDtypeStruct(z0.shape, BF16),
        name="conv_sample",
    )(z0, z1, z2, gb, conv_w)


def _gla_kernel(q_ref, k_ref, v_ref, la_ref, gate_ref, nw_ref, s0_ref, o_ref, st_ref, st, *, nc):
    ci = pl.program_id(1)

    @pl.when(ci == 0)
    def _():
        st[...] = s0_ref[...]

    heads, dv, dk = st.shape
    la = la_ref[...]
    c, width = la.shape
    r = lax.broadcasted_iota(jnp.int32, (c, c), 0)
    cc = lax.broadcasted_iota(jnp.int32, (c, c), 1)
    tril = cc <= r
    y = jnp.dot(jnp.where(tril, 1.0, 0.0).astype(BF16),
                jnp.concatenate(_split3(la), axis=1), preferred_element_type=F32)
    bc = y[:, :width] + y[:, width:2 * width] + y[:, 2 * width:]
    q = q_ref[...].astype(F32) * dk ** -0.5
    k = k_ref[...].astype(F32)
    qt = (q * jnp.exp(bc)).astype(BF16)
    mid = bc[c // 2 - 1:c // 2, :]
    qa = (q * jnp.exp(bc - mid)).astype(BF16)
    ka = (k * jnp.exp(mid - bc)).astype(BF16)
    bl = bc[c - 1:c, :]
    kd = (k * jnp.exp(bl - bc)).astype(BF16)
    decay = jnp.exp(bl)
    for h in range(heads):
        kcol = slice(h * dk, (h + 1) * dk)
        vcol = slice(h * dv, (h + 1) * dv)
        v = v_ref[:, vcol]
        att = lax.dot_general(qa[:, kcol], ka[:, kcol], (((1,), (1,)), ((), ())),
                              preferred_element_type=F32)
        att = jnp.where(tril, att, 0.0)
        s_t = st[h]
        o = (lax.dot_general(qt[:, kcol], s_t.astype(BF16), (((1,), (1,)), ((), ())),
                             preferred_element_type=F32)
             + jnp.dot(att.astype(BF16), v, preferred_element_type=F32))
        st[h] = s_t * decay[:, kcol] + lax.dot_general(v, kd[:, kcol], (((0,), (0,)), ((), ())),
                                                       preferred_element_type=F32)
        on = o * lax.rsqrt(jnp.mean(o * o, axis=-1, keepdims=True) + EPS) * nw_ref[:, vcol]
        o_ref[:, vcol] = (on * gate_ref[:, vcol].astype(F32)).astype(o_ref.dtype)

    @pl.when(ci == nc - 1)
    def _():
        st_ref[...] = st[...]


def _gla(qkv, log_a, gate, norm_w, s0_t, batch, seq, heads, chunk):
    m = qkv.shape[0]
    key_w, val_w = log_a.shape[1], gate.shape[1]
    dk, dv = key_w // heads, val_w // heads
    assert (2 * key_w) % val_w == 0
    v0 = (2 * key_w) // val_w
    chunk = min(chunk, seq)
    nc = seq // chunk
    row = lambda b, c: b * nc + c
    return pl.pallas_call(
        functools.partial(_gla_kernel, nc=nc),
        grid=(batch, nc),
        in_specs=[pl.BlockSpec((chunk, key_w), lambda b, c: (row(b, c), 0)),
                  pl.BlockSpec((chunk, key_w), lambda b, c: (row(b, c), 1)),
                  pl.BlockSpec((chunk, val_w), lambda b, c: (row(b, c), v0)),
                  pl.BlockSpec((chunk, key_w), lambda b, c: (row(b, c), 0)),
                  pl.BlockSpec((chunk, val_w), lambda b, c: (row(b, c), 0)),
                  pl.BlockSpec((1, val_w), lambda b, c: (0, 0)),
                  pl.BlockSpec((None, heads, dv, dk), lambda b, c: (b, 0, 0, 0))],
        out_specs=[pl.BlockSpec((chunk, val_w), lambda b, c: (row(b, c), 0)),
                   pl.BlockSpec((None, heads, dv, dk), lambda b, c: (b, 0, 0, 0))],
        out_shape=[jax.ShapeDtypeStruct((m, val_w), BF16),
                   jax.ShapeDtypeStruct((batch, heads, dv, dk), F32)],
        scratch_shapes=[pltpu.VMEM((heads, dv, dk), F32)],
        compiler_params=_params("arbitrary", "arbitrary"),
        name="gla",
    )(qkv, qkv, qkv, log_a, gate, norm_w, s0_t)


def kernel(x_prompt, x_sample, cache_k, cache_v, cache_logf, state_conv, state_gla, page_table,
           c_prompt, c_sample, w_in0, f_bias, q_norm, k_norm, sgu_norm, sgu_w, sgu_b, w_out0,
           w_in1, conv_w, gla_a_w2, gla_a_b, gla_norm, w_out1, ada_w, ada_b, mlp_w1, mlp_w2):
    nbp, seq, d = x_prompt.shape
    nbs, dseq, _ = x_sample.shape
    mp, ms = nbp * seq, nbs * dseq
    fox_w = d // 2
    heads = fox_w // HEAD_DIM
    sgu_width = d // 2
    groups = sgu_w.shape[1]
    conv_width = state_conv.shape[-1]
    gla_heads, gla_dk, gla_dv = state_gla.shape[2], state_gla.shape[3], state_gla.shape[4]
    gla_key, gla_val = gla_heads * gla_dk, gla_heads * gla_dv
    gla_rank = gla_a_w2.shape[1]
    n_pool, page = cache_k.shape[1], cache_k.shape[2]
    n_pages = page_table.shape[1]

    r_c = nbp + nbs
    r_pad = -(-r_c // 16) * 16
    c_rows = jnp.pad(jnp.concatenate([c_prompt, c_sample], axis=0),
                     ((0, r_pad - r_c), (0, 0))).astype(BF16)
    mod = _ada(c_rows, ada_w, ada_b)

    def mods(layer):
        parts = jnp.split(mod[layer], 6, axis=-1)
        pp = [p[:nbp].reshape(nbp, 1, d) for p in parts]
        ps = [jnp.repeat(p[nbp:r_c], dseq, axis=0) for p in parts]
        return pp, ps

    xp = x_prompt.reshape(mp, d)
    xs = x_sample.reshape(ms, d)

    def modulate_both(xp, xs, shp, scp, shs, scs):
        hp = _modulate(xp, shp, scp, seq)
        hs = _modulate(xs, shs.reshape(1, ms, d), scs.reshape(1, ms, d), ms, tr=ms)
        return hp, hs

    def mlp(xp, xs, layer, shp, scp, gp, shs, scs, gs):
        hp, hs = modulate_both(xp, xs, shp, scp, shs, scs)
        (ap,), (as_,) = _proj([hp], [hs], [(mlp_w1, layer, 0)], mlp_w1.shape[2], _epi_relu2,
                              [BF16], name="mlp_up")
        return _down(ap, as_, mlp_w2, layer, xp, gp, xs, gs, seq)

    def residual_proj(lhs_p, lhs_s, w, xp, xs, gp, gs, name):
        (yp,), (ys,) = _proj(lhs_p, lhs_s, [(w, 0, 0)], d, _epi_residual, [F32],
                             ext=[("res_p", "prow", xp), ("gate_p", "pgrp", gp),
                                  ("res_s", "srow", xs), ("gate_s", "srow", gs)],
                             rows_per_group=seq, name=name)
        return yp, ys

    (sh1p, sc1p, g1p, sh2p, sc2p, g2p), (sh1s, sc1s, g1s, sh2s, sc2s, g2s) = mods(0)
    hp, hs = modulate_both(xp, xs, sh1p, sc1p, sh1s, sc1s)
    w0 = jnp.swapaxes(w_in0, 1, 2)
    qn = jnp.tile(q_norm[0], heads).reshape(1, fox_w)
    kn = jnp.tile(k_norm[0], heads).reshape(1, fox_w)
    (q_p,), (q_s,) = _proj([hp], [hs], [(w0, 0, 0)], fox_w, _epi_head_norm, [BF16],
                           ext=[("norm", "col", qn)], w_rows=True, name="in0_q")
    (k_p,), (k_s,) = _proj([hp], [hs], [(w0, 0, fox_w)], fox_w, _epi_head_norm, [F32],
                           ext=[("norm", "col", kn)], w_rows=True, name="in0_k")
    (v_p,), (v_s,) = _proj([hp], [hs], [(w0, 0, 2 * fox_w)], fox_w, _epi_plain, [F32],
                           w_rows=True, name="in0_v")
    fb = jnp.pad(f_bias[0], (0, LANES - heads)).reshape(1, LANES)
    (lf_p,), (lf_s,) = _proj([hp], [hs], [(w0, 0, 3 * fox_w)], LANES, _epi_log_forget, [F32],
                             ext=[("bias", "col", fb)], w_rows=True, name="in0_logf")
    (uv_p,), (uv_s,) = _proj([hp], [hs], [(w0, 0, 3 * fox_w + heads)], 2 * sgu_width, _epi_gelu,
                             [BF16], w_rows=True, name="in0_uv")

    f_col = _cumsum_rows(lf_p, nbp)
    lf_s16 = lf_s[:, :heads].reshape(nbs, dseq, heads)
    tpad = 8
    lf_s_pad = jnp.pad(lf_s.reshape(nbs, dseq, LANES), ((0, 0), (0, tpad - dseq), (0, 0)))
    fn = _cumsum_rows(lf_s_pad.reshape(nbs * tpad, LANES), nbs).reshape(
        nbs, tpad, LANES)[:, :dseq, :heads]
    rows_s = dseq * heads
    assert rows_s <= LANES

    def new_rows(a):
        a = a.reshape(nbs, rows_s, HEAD_DIM)
        return jnp.pad(a, ((0, 0), (0, LANES - rows_s), (0, 0))).astype(BF16)

    fs_lane = jnp.pad(fn.reshape(nbs, 1, rows_s), ((0, 0), (0, 0), (0, LANES - rows_s)))
    oa_p, oa_rows, k5_p, v5_p = _fox(q_p, k_p, v_p, f_col, nbp, seq, heads,
                         page_table, q_s.reshape(nbs, rows_s, HEAD_DIM),
                         cache_k[0].reshape(n_pool * page * heads, HEAD_DIM),
                         cache_v[0].reshape(n_pool * page * heads, HEAD_DIM),
                         cache_logf[0].reshape(n_pool, page * heads // LANES, LANES),
                         new_rows(k_s), new_rows(v_s), fn.reshape(nbs, rows_s, 1), fs_lane)
    oa_s = oa_rows.reshape(ms, fox_w).astype(BF16)

    sn = sgu_norm[0].reshape(1, sgu_width)
    rows_p = min(seq, SGU_CHUNK)
    (ob_p,) = _sgu(uv_p, sn, sgu_w[0][:, :rows_p, :rows_p], sgu_b[0][:, :rows_p].T, rows_p, False)
    rows_g = min(dseq, SGU_CHUNK)
    w_small = sgu_w[0][:, :rows_g, :rows_g]
    w_big = jnp.einsum("ab,gts->gatbs", jnp.eye(nbs, dtype=F32), w_small).reshape(groups, ms, ms)
    b_big = jnp.tile(sgu_b[0][:, :rows_g].T, (nbs, 1))
    ob_s, sgu_v = _sgu(uv_s, sn, w_big, b_big, ms, True)

    xp, xs = residual_proj([oa_p, ob_p], [oa_s, ob_s], w_out0, xp, xs, g1p, g1s, "out0")
    xp, xs = mlp(xp, xs, 0, sh2p, sc2p, g2p, sh2s, sc2s, g2s)

    (sh1p, sc1p, g1p, sh2p, sc2p, g2p), (sh1s, sc1s, g1s, sh2s, sc2s, g2s) = mods(1)
    hp, hs = modulate_both(xp, xs, sh1p, sc1p, sh1s, sc1s)
    w1 = jnp.swapaxes(w_in1, 1, 2)
    cw = conv_width
    cwt = conv_w[0]
    n_tap = cwt.shape[0]
    (oc_p, ztail_p), (gb_s, z_s) = _proj(
        [hp], [hs], [(w1, 0, 0), (w1, 0, cw), (w1, 0, 2 * cw)], cw, _make_epi_conv(seq),
        [BF16, F32], ext=[("taps", "col", cwt)], tm=512, tn=256, rows_per_group=seq, w_rows=True,
        tail_rows=(1,), scratch=lambda tm, tn: [pltpu.VMEM((tm + 8, tn), F32)], name="in1_conv")
    conv_p = ztail_p.reshape(nbp, -1, 8, cw)[:, -1, 8 - (n_tap - 1):, :]
    (qkv_p,), (qkv_s,) = _proj([hp], [hs], [(w1, 0, 3 * cw)], 2 * gla_key + gla_val, _epi_plain,
                               [BF16], w_rows=True, name="in1_qkv")
    c_ga = 3 * cw + 2 * gla_key + gla_val
    (ga_p,), (ga_s,) = _proj([hp], [hs], [(w1, 0, c_ga)], LANES, _epi_plain, [BF16],
                             w_rows=True, name="in1_ga")
    w_a2 = jnp.pad(gla_a_w2, ((0, 0), (0, LANES - gla_rank), (0, 0)))
    (la_p,), (la_s,) = _proj([ga_p], [ga_s], [(w_a2, 0, 0)], gla_key, _epi_log_decay, [F32],
                             ext=[("bias", "col", gla_a_b[0].reshape(1, gla_key))], name="in1_log_a")
    (gg_p,), (gg_s,) = _proj([hp], [hs], [(w1, 0, c_ga + gla_rank)], gla_val, _epi_silu, [BF16],
                             w_rows=True, name="in1_gate")

    zp_s = jnp.concatenate([state_conv[0], z_s.reshape(nbs, dseq, cw)], axis=1)
    shifted = [zp_s[:, i:i + dseq].reshape(ms, cw) for i in range(cwt.shape[0])]
    oc_s = _conv_sample(shifted[0], shifted[1], shifted[2], gb_s, cwt)
    conv_s = zp_s[:, dseq:, :]

    gn = gla_norm[0].reshape(1, gla_val)
    s0_p = jnp.zeros((nbp, gla_heads, gla_dv, gla_dk), F32)
    od_p, st_p = _gla(qkv_p, la_p, gg_p, gn, s0_p, nbp, seq, gla_heads, GLA_CHUNK_ROWS)
    pad_t = 16

    def pad_rows(a):
        return jnp.pad(a.reshape(nbs, dseq, -1), ((0, 0), (0, pad_t - dseq), (0, 0))).reshape(
            nbs * pad_t, -1)

    od_s_pad, st_s = _gla(pad_rows(qkv_s), pad_rows(la_s), pad_rows(gg_s), gn,
                          state_gla[0].swapaxes(-1, -2), nbs, pad_t, gla_heads, pad_t)
    od_s = od_s_pad.reshape(nbs, pad_t, gla_val)[:, :dseq].reshape(ms, gla_val)

    xp, xs = residual_proj([oc_p, od_p], [oc_s, od_s], w_out1, xp, xs, g1p, g1s, "out1")
    xp, xs = mlp(xp, xs, 1, sh2p, sc2p, g2p, sh2s, sc2s, g2s)

    y_prompt = xp.reshape(nbp, seq, d)
    y_sample = xs.reshape(nbs, dseq, d)
    return (y_prompt, y_sample,
            k5_p.reshape(1, nbp, seq, heads, HEAD_DIM), v5_p.reshape(1, nbp, seq, heads, HEAD_DIM),
            lf_p[:, :heads].reshape(1, nbp, seq, heads),
            k_s.reshape(1, nbs, dseq, heads, HEAD_DIM), v_s.reshape(1, nbs, dseq, heads, HEAD_DIM),
            lf_s16[None], sgu_v.reshape(1, nbs, dseq, sgu_width),
            conv_p[None], conv_s[None],
            st_p.swapaxes(-1, -2)[None], st_s.swapaxes(-1, -2)[None])
```
